```python
import math
import jax, jax.numpy as jnp
from jax import lax
import numpy as np

D_MODEL = 2048
BATCH = 8
SEQ = 2048
DEPTH = 1

N_META = 16
Q_BLOCK = 128
EPS = 1e-6

MLA_HEADS = 16
MLA_Q_RANK = 768
MLA_KV_RANK = 512
MLA_NOPE = 128
MLA_ROPE = 64
MLA_V = 128
MLA_QK = MLA_NOPE + MLA_ROPE
MLA_WIDTH = MLA_HEADS * MLA_V
ROPE_THETA = 10000.0

DIFF_HEADS = 8
DIFF_QK = 128
DIFF_V = 2 * DIFF_QK
DIFF_WIDTH = DIFF_HEADS * DIFF_V
DIFF_QK_WIDTH = DIFF_HEADS * 2 * DIFF_QK

REL_BUCKETS = 32
REL_MAX_DIST = 128

IN_SPLITS = (MLA_Q_RANK, MLA_KV_RANK, MLA_ROPE, MLA_WIDTH,
             DIFF_QK_WIDTH, DIFF_QK_WIDTH, DIFF_WIDTH, DIFF_WIDTH,
             D_MODEL, D_MODEL)
IN_WIDTH = sum(IN_SPLITS)

kernel_name = "hybrid_mla_diffattn_gated_encoder"


def rmsnorm(x, g):
    xf = x.astype(jnp.float32)
    y = xf * lax.rsqrt(jnp.mean(xf * xf, axis=-1, keepdims=True) + EPS)
    return (y * g.astype(jnp.float32)).astype(x.dtype)


def apply_rope(x, pos):
    half = x.shape[-1] // 2
    inv = ROPE_THETA ** (-jnp.arange(half, dtype=jnp.float32) / half)
    ang = pos.astype(jnp.float32)[:, None] * inv[None, :]
    cos = jnp.cos(ang)[None, :, None, :]
    sin = jnp.sin(ang)[None, :, None, :]
    x1 = x[..., :half].astype(jnp.float32)
    x2 = x[..., half:].astype(jnp.float32)
    out = jnp.concatenate([x1 * cos - x2 * sin, x2 * cos + x1 * sin], axis=-1)
    return out.astype(x.dtype)


def t5_bucket(rel):
    nb = REL_BUCKETS // 2
    max_exact = nb // 2
    ret = jnp.where(rel > 0, nb, 0)
    n = jnp.abs(rel)
    nf = jnp.maximum(n, 1).astype(jnp.float32)
    large = max_exact + (jnp.log(nf / max_exact) / math.log(REL_MAX_DIST / max_exact)
                         * (nb - max_exact)).astype(jnp.int32)
    large = jnp.minimum(large, nb - 1)
    return ret + jnp.where(n < max_exact, n, large)


def _to_blocks(t, n_blk):
    pad = n_blk * Q_BLOCK - t.shape[1]
    t = jnp.pad(t, [(0, 0), (0, pad)] + [(0, 0)] * (t.ndim - 2))
    t = t.reshape((t.shape[0], n_blk, Q_BLOCK) + t.shape[2:])
    return jnp.moveaxis(t, 1, 0)


def _from_blocks(o, L):
    o = jnp.moveaxis(o, 0, 1)
    o = o.reshape((o.shape[0], -1) + o.shape[3:])
    return o[:, :L]


def mla_attention(q, k, v):
    L = q.shape[1]
    n_blk = -(-L // Q_BLOCK)
    scale = MLA_QK ** -0.5

    def body(qb):
        s = jnp.einsum("bqhd,bkhd->bhqk", qb, k, preferred_element_type=jnp.float32) * scale
        p = jax.nn.softmax(s, axis=-1).astype(v.dtype)
        return jnp.einsum("bhqk,bkhd->bqhd", p, v)

    return _from_blocks(lax.map(body, _to_blocks(q, n_blk)), L)


def diff_attention(q1, q2, k1, k2, v, lam, rel_bias):
    L = q1.shape[1]
    n_blk = -(-L // Q_BLOCK)
    scale = DIFF_QK ** -0.5
    kpos = jnp.arange(L, dtype=jnp.int32)
    qpos_blocks = jnp.arange(n_blk * Q_BLOCK, dtype=jnp.int32).reshape(n_blk, Q_BLOCK)

    def body(args):
        q1b, q2b, qpos = args
        bias = rel_bias[t5_bucket(kpos[None, :] - qpos[:, None])]
        bias = jnp.transpose(bias, (2, 0, 1)).astype(jnp.float32)[None]
        s1 = jnp.einsum("bqhd,bkhd->bhqk", q1b, k1, preferred_element_type=jnp.float32) * scale + bias
        s2 = jnp.einsum("bqhd,bkhd->bhqk", q2b, k2, preferred_element_type=jnp.float32) * scale + bias
        a = jax.nn.softmax(s1, axis=-1) - lam * jax.nn.softmax(s2, axis=-1)
        return jnp.einsum("bhqk,bkhd->bqhd", a.astype(v.dtype), v)

    out = lax.map(body, (_to_blocks(q1, n_blk), _to_blocks(q2, n_blk), qpos_blocks))
    return _from_blocks(out, L)


def hybrid_layer(h, pos, rel_bias, norm_in, w_in, q_a_norm, kv_a_norm, w_uq, w_ukv,
                 mla_q_norm, mla_k_norm, diff_q_norm, diff_k_norm, diff_lambda,
                 diff_subln, w_branch_a, w_branch_b, w_out, layer_idx):
    B, L, _ = h.shape
    u = rmsnorm(h, norm_in)
    proj = u @ w_in
    split_at = [int(i) for i in np.cumsum(IN_SPLITS)[:-1]]
    c_q, c_kv, k_rope, z_a, q_d, k_d, v_d, z_b, g_a, g_b = jnp.split(proj, split_at, axis=-1)

    q = (rmsnorm(c_q, q_a_norm) @ w_uq).reshape(B, L, MLA_HEADS, MLA_QK)
    kv = (rmsnorm(c_kv, kv_a_norm) @ w_ukv).reshape(B, L, MLA_HEADS, MLA_NOPE + MLA_V)
    k_nope, v_a = kv[..., :MLA_NOPE], kv[..., MLA_NOPE:]
    k_r = jnp.broadcast_to(k_rope[:, :, None, :], (B, L, MLA_HEADS, MLA_ROPE))
    k = jnp.concatenate([k_nope, k_r], axis=-1)
    q = rmsnorm(q, mla_q_norm)
    k = rmsnorm(k, mla_k_norm)
    q = jnp.concatenate([q[..., :MLA_NOPE], apply_rope(q[..., MLA_NOPE:], pos)], axis=-1)
    k = jnp.concatenate([k[..., :MLA_NOPE], apply_rope(k[..., MLA_NOPE:], pos)], axis=-1)
    o_a = mla_attention(q, k, v_a).reshape(B, L, MLA_WIDTH) * jax.nn.silu(z_a)

    q_d = rmsnorm(q_d.reshape(B, L, DIFF_HEADS, 2, DIFF_QK), diff_q_norm)
    k_d = rmsnorm(k_d.reshape(B, L, DIFF_HEADS, 2, DIFF_QK), diff_k_norm)
    v_d = v_d.reshape(B, L, DIFF_HEADS, DIFF_V)
    lam_init = 0.8 - 0.6 * math.exp(-0.3 * layer_idx)
    lv = diff_lambda.astype(jnp.float32)
    lam = jnp.exp(jnp.sum(lv[0] * lv[1])) - jnp.exp(jnp.sum(lv[2] * lv[3])) + lam_init
    o_b = diff_attention(q_d[..., 0, :], q_d[..., 1, :], k_d[..., 0, :], k_d[..., 1, :],
                         v_d, lam, rel_bias)
    o_b = rmsnorm(o_b, diff_subln) * (1.0 - lam_init)
    o_b = o_b.reshape(B, L, DIFF_WIDTH) * jax.nn.silu(z_b)

    m = jax.nn.sigmoid(g_a) * (o_a @ w_branch_a) + jax.nn.sigmoid(g_b) * (o_b @ w_branch_b)
    return h + m @ w_out


def setup_inputs(seed: int = 0) -> dict:
    key = jax.random.key(seed)
    ks = jax.random.split(key, 18)
    f32 = jnp.float32

    def nrm(k, shape, scale):
        return jax.random.normal(k, shape, f32) * scale

    def gain(k, shape):
        return 1.0 + 0.02 * jax.random.normal(k, shape, f32)

    Lr = DEPTH
    return {
        "x": nrm(ks[0], (BATCH, SEQ, D_MODEL), 1.0),
        "meta_tokens": nrm(ks[1], (N_META, D_MODEL), 1.0),
        "rel_bias": nrm(ks[2], (REL_BUCKETS, DIFF_HEADS), 0.5),
        "norm_in": gain(ks[3], (Lr, D_MODEL)),
        "w_in": nrm(ks[4], (Lr, D_MODEL, IN_WIDTH), D_MODEL ** -0.5),
        "q_a_norm": gain(ks[5], (Lr, MLA_Q_RANK)),
        "kv_a_norm": gain(ks[6], (Lr, MLA_KV_RANK)),
        "w_uq": nrm(ks[7], (Lr, MLA_Q_RANK, MLA_HEADS * MLA_QK), MLA_Q_RANK ** -0.5),
        "w_ukv": nrm(ks[8], (Lr, MLA_KV_RANK, MLA_HEADS * (MLA_NOPE + MLA_V)), MLA_KV_RANK ** -0.5),
        "mla_q_norm": gain(ks[9], (Lr, MLA_QK)),
        "mla_k_norm": gain(ks[10], (Lr, MLA_QK)),
        "diff_q_norm": gain(ks[11], (Lr, DIFF_QK)),
        "diff_k_norm": gain(ks[12], (Lr, DIFF_QK)),
        "diff_lambda": nrm(ks[13], (Lr, 4, DIFF_QK), 0.1),
        "diff_subln": gain(ks[14], (Lr, DIFF_V)),
        "w_branch_a": nrm(ks[15], (Lr, MLA_WIDTH, D_MODEL), MLA_WIDTH ** -0.5),
        "w_branch_b": nrm(ks[16], (Lr, DIFF_WIDTH, D_MODEL), DIFF_WIDTH ** -0.5),
        "w_out": nrm(ks[17], (Lr, D_MODEL, D_MODEL), D_MODEL ** -0.5),
    }


def reference(x, meta_tokens, rel_bias, norm_in, w_in, q_a_norm, kv_a_norm, w_uq, w_ukv,
              mla_q_norm, mla_k_norm, diff_q_norm, diff_k_norm, diff_lambda, diff_subln,
              w_branch_a, w_branch_b, w_out):
    B = x.shape[0]
    meta = jnp.broadcast_to(meta_tokens[None].astype(x.dtype), (B, N_META, x.shape[-1]))
    h = jnp.concatenate([meta, x], axis=1)
    pos = jnp.arange(h.shape[1], dtype=jnp.int32)
    for l in range(DEPTH):
        h = hybrid_layer(h, pos, rel_bias, norm_in[l], w_in[l], q_a_norm[l], kv_a_norm[l],
                         w_uq[l], w_ukv[l], mla_q_norm[l], mla_k_norm[l], diff_q_norm[l],
                         diff_k_norm[l], diff_lambda[l], diff_subln[l], w_branch_a[l],
                         w_branch_b[l], w_out[l], l)
    return h[:, N_META:]
```

```python
import functools
import math

import jax
import jax.numpy as jnp
from jax import lax
from jax.experimental import pallas as pl
from jax.experimental.pallas import tpu as pltpu

D_MODEL = 2048
N_META = 16
EPS = 1e-6

MLA_HEADS = 16
MLA_Q_RANK = 768
MLA_KV_RANK = 512
MLA_NOPE = 128
MLA_ROPE = 64
MLA_V = 128
MLA_QK = MLA_NOPE + MLA_ROPE
ROPE_THETA = 10000.0

DIFF_HEADS = 8
DIFF_QK = 128
DIFF_V = 2 * DIFF_QK
LAM_INIT = 0.8 - 0.6 * math.exp(-0.3 * 0)

REL_BUCKETS = 32
REL_MAX_DIST = 128

LANES = 128
MXU_DIM = 256
VMEM_LIMIT_BYTES = 56 * 1024 * 1024

HEAD_PAD = MXU_DIM
SMALL_W = MLA_Q_RANK + MLA_KV_RANK + LANES
META_PAD = LANES
MASK_VALUE = -1e30
TQ = 256
KC = 256

SEG_ZA, SEG_QD, SEG_KD, SEG_VD, SEG_ZB, SEG_GA, SEG_GB = range(7)
N_SEG = 7


def _dot(a, b):
    return jnp.dot(a, b, preferred_element_type=jnp.float32)


def _dot_nt(a, b):
    return lax.dot_general(a, b, (((1,), (1,)), ((), ())), preferred_element_type=jnp.float32)


def _rms_scale(x, width):
    return lax.rsqrt(jnp.sum(x * x, axis=-1, keepdims=True) * (1.0 / width) + EPS)


def _rot_half64(x):
    return pltpu.roll(x, 32, 1) + pltpu.roll(x, 96, 1)


def _mla_proj_kernel(x_ref, gin_ref, wsm_ref, gqa_ref, gkva_ref, wuq_ref, wukv_ref,
                     gq_ref, gk_ref, cos_ref, sin_ref, q_ref, k_ref, v_ref, *, q_scale):
    x = x_ref[...]
    u = (x * _rms_scale(x, D_MODEL) * gin_ref[...]).astype(jnp.bfloat16)
    p = _dot(u, wsm_ref[...])
    cq = p[:, :MLA_Q_RANK]
    ckv = p[:, MLA_Q_RANK:MLA_Q_RANK + MLA_KV_RANK]
    kr = p[:, MLA_Q_RANK + MLA_KV_RANK:]
    cqn = (cq * _rms_scale(cq, MLA_Q_RANK) * gqa_ref[...]).astype(jnp.bfloat16)
    ckvn = (ckv * _rms_scale(ckv, MLA_KV_RANK) * gkva_ref[...]).astype(jnp.bfloat16)
    qf = _dot(cqn, wuq_ref[...])
    kvf = _dot(ckvn, wukv_ref[...])

    cos = cos_ref[...]
    sin = sin_ref[...]
    gq = gq_ref[...]
    gk = gk_ref[...]
    gq_nope, gq_rope = gq[:, :LANES], gq[:, LANES:]
    gk_nope, gk_rope = gk[:, :LANES], gk[:, LANES:]

    ss_kr = jnp.sum(kr * kr, axis=-1, keepdims=True)
    krg = kr * gk_rope
    kr_roped = krg * cos + _rot_half64(krg) * sin

    for h in range(MLA_HEADS):
        qh = qf[:, h * HEAD_PAD:(h + 1) * HEAD_PAD]
        rq = _rms_scale(qh, MLA_QK)
        q_nope = qh[:, :LANES] * rq * gq_nope
        q_r = qh[:, LANES:] * rq * gq_rope
        q_r = q_r * cos + _rot_half64(q_r) * sin
        q_ref[0, h, :, :LANES] = (q_nope * q_scale).astype(jnp.bfloat16)
        q_ref[0, h, :, LANES:] = (q_r * q_scale).astype(jnp.bfloat16)

        k_nope = kvf[:, h * HEAD_PAD:h * HEAD_PAD + MLA_NOPE]
        rk = lax.rsqrt((jnp.sum(k_nope * k_nope, axis=-1, keepdims=True) + ss_kr) * (1.0 / MLA_QK) + EPS)
        k_ref[0, h, :, :LANES] = (k_nope * rk * gk_nope).astype(jnp.bfloat16)
        k_ref[0, h, :, LANES:] = (kr_roped * rk).astype(jnp.bfloat16)
        v_ref[0, h] = kvf[:, h * HEAD_PAD + MLA_NOPE:(h + 1) * HEAD_PAD].astype(jnp.bfloat16)


def _mla_proj(x3, gin, wsm, gqa, gkva, wuq, wukv, gq, gk, cos, sin, tm):
    nb, rows, _ = x3.shape
    const = lambda b, i: (0, 0)
    kern = functools.partial(_mla_proj_kernel, q_scale=MLA_QK ** -0.5)
    return pl.pallas_call(
        kern,
        grid=(nb, rows // tm),
        in_specs=[
            pl.BlockSpec((None, tm, D_MODEL), lambda b, i: (b, i, 0)),
            pl.BlockSpec((1, D_MODEL), const),
            pl.BlockSpec((D_MODEL, SMALL_W), const),
            pl.BlockSpec((1, MLA_Q_RANK), const),
            pl.BlockSpec((1, MLA_KV_RANK), const),
            pl.BlockSpec((MLA_Q_RANK, MLA_HEADS * HEAD_PAD), const),
            pl.BlockSpec((MLA_KV_RANK, MLA_HEADS * HEAD_PAD), const),
            pl.BlockSpec((1, HEAD_PAD), const),
            pl.BlockSpec((1, HEAD_PAD), const),
            pl.BlockSpec((tm, LANES), lambda b, i: (i, 0)),
            pl.BlockSpec((tm, LANES), lambda b, i: (i, 0)),
        ],
        out_specs=[
            pl.BlockSpec((1, MLA_HEADS, tm, HEAD_PAD), lambda b, i: (b, 0, i, 0)),
            pl.BlockSpec((1, MLA_HEADS, tm, HEAD_PAD), lambda b, i: (b, 0, i, 0)),
            pl.BlockSpec((1, MLA_HEADS, tm, MLA_V), lambda b, i: (b, 0, i, 0)),
        ],
        out_shape=[
            jax.ShapeDtypeStruct((nb, MLA_HEADS, rows, HEAD_PAD), jnp.bfloat16),
            jax.ShapeDtypeStruct((nb, MLA_HEADS, rows, HEAD_PAD), jnp.bfloat16),
            jax.ShapeDtypeStruct((nb, MLA_HEADS, rows, MLA_V), jnp.bfloat16),
        ],
        compiler_params=pltpu.CompilerParams(
            dimension_semantics=("arbitrary", "arbitrary"), vmem_limit_bytes=VMEM_LIMIT_BYTES),
        name="mla_proj",
    )(x3, gin, wsm, gqa, gkva, wuq, wukv, gq, gk, cos, sin)


def _group_norm_store(acc, gain, scale, o_ref):
    for c in range(acc.shape[1] // DIFF_QK):
        g = acc[:, c * DIFF_QK:(c + 1) * DIFF_QK]
        y = g * _rms_scale(g, DIFF_QK) * gain
        if scale != 1.0:
            y = y * scale
        o_ref[:, c * DIFF_QK:(c + 1) * DIFF_QK] = y.astype(o_ref.dtype)


def _in_proj_kernel(x_ref, gin_ref, w_ref, gqd_ref, gkd_ref, o_ref, u_ref, *, seg0, tiles_per_seg, qd_scale):
    j = pl.program_id(1)

    @pl.when(j == 0)
    def _():
        x = x_ref[...]
        u_ref[...] = (x * _rms_scale(x, D_MODEL) * gin_ref[...]).astype(u_ref.dtype)

    acc = _dot(u_ref[...], w_ref[...])
    seg = seg0 + j // tiles_per_seg

    @pl.when((seg == SEG_ZA) | (seg == SEG_ZB))
    def _():
        o_ref[...] = (acc / (1.0 + jnp.exp(-acc))).astype(o_ref.dtype)

    @pl.when((seg == SEG_GA) | (seg == SEG_GB))
    def _():
        o_ref[...] = (1.0 / (1.0 + jnp.exp(-acc))).astype(o_ref.dtype)

    @pl.when(seg == SEG_VD)
    def _():
        o_ref[...] = acc.astype(o_ref.dtype)

    @pl.when(seg == SEG_QD)
    def _():
        _group_norm_store(acc, gqd_ref[...], qd_scale, o_ref)

    @pl.when(seg == SEG_KD)
    def _():
        _group_norm_store(acc, gkd_ref[...], 1.0, o_ref)


def _in_proj(x2, gin, wbig, gqd, gkd, tm, tn, seg0, nseg):
    rows = x2.shape[0]
    tiles_per_seg = D_MODEL // tn
    j0 = seg0 * tiles_per_seg
    kern = functools.partial(_in_proj_kernel, seg0=seg0, tiles_per_seg=tiles_per_seg, qd_scale=DIFF_QK ** -0.5)
    return pl.pallas_call(
        kern,
        grid=(rows // tm, nseg * tiles_per_seg),
        in_specs=[
            pl.BlockSpec((tm, D_MODEL), lambda i, j: (i, 0)),
            pl.BlockSpec((1, D_MODEL), lambda i, j: (0, 0)),
            pl.BlockSpec((D_MODEL, tn), lambda i, j: (0, j0 + j)),
            pl.BlockSpec((1, DIFF_QK), lambda i, j: (0, 0)),
            pl.BlockSpec((1, DIFF_QK), lambda i, j: (0, 0)),
        ],
        out_specs=pl.BlockSpec((tm, tn), lambda i, j: (i, j)),
        out_shape=jax.ShapeDtypeStruct((rows, nseg * D_MODEL), jnp.bfloat16),
        scratch_shapes=[pltpu.VMEM((tm, D_MODEL), jnp.bfloat16)],
        compiler_params=pltpu.CompilerParams(
            dimension_semantics=("arbitrary", "arbitrary"), vmem_limit_bytes=VMEM_LIMIT_BYTES),
        name="in_proj",
    )(x2, gin, wbig, gqd, gkd)


def _mla_attn_kernel(q_ref, k_ref, v_ref, km_ref, vm_ref, mask_ref, za_ref, o_ref):
    k = k_ref[0, 0]
    v = v_ref[0, 0]
    km = km_ref[0]
    vm = vm_ref[0]
    mask = mask_ref[...]

    def body(qi, carry):
        r0 = pl.multiple_of(qi * TQ, TQ)
        qt = q_ref[0, 0, pl.ds(r0, TQ), :]
        s = _dot_nt(qt, k)
        sm = _dot_nt(qt, km) + mask
        m = jnp.maximum(jnp.max(s, axis=-1, keepdims=True), jnp.max(sm, axis=-1, keepdims=True))
        p = jnp.exp(s - m)
        pm = jnp.exp(sm - m)
        l = jnp.sum(p, axis=-1, keepdims=True) + jnp.sum(pm, axis=-1, keepdims=True)
        o = _dot(p.astype(jnp.bfloat16), v) + _dot(pm.astype(jnp.bfloat16), vm)
        o = o * (1.0 / l) * za_ref[pl.ds(r0, TQ), :].astype(jnp.float32)
        o_ref[pl.ds(r0, TQ), :] = o.astype(o_ref.dtype)
        return carry

    lax.fori_loop(0, q_ref.shape[2] // TQ, body, 0)


def _mla_attn(q, k, v, km, vm, mask, proj, seq):
    nb = q.shape[0]
    return pl.pallas_call(
        _mla_attn_kernel,
        grid=(nb, MLA_HEADS),
        in_specs=[
            pl.BlockSpec((1, 1, seq, HEAD_PAD), lambda b, h: (b, h, 0, 0)),
            pl.BlockSpec((1, 1, seq, HEAD_PAD), lambda b, h: (b, h, 0, 0)),
            pl.BlockSpec((1, 1, seq, MLA_V), lambda b, h: (b, h, 0, 0)),
            pl.BlockSpec((1, META_PAD, HEAD_PAD), lambda b, h: (h, 0, 0)),
            pl.BlockSpec((1, META_PAD, MLA_V), lambda b, h: (h, 0, 0)),
            pl.BlockSpec((1, META_PAD), lambda b, h: (0, 0)),
            pl.BlockSpec((seq, MLA_V), lambda b, h: (b, SEG_ZA * (D_MODEL // MLA_V) + h)),
        ],
        out_specs=pl.BlockSpec((seq, MLA_V), lambda b, h: (b, h)),
        out_shape=jax.ShapeDtypeStruct((nb * seq, MLA_HEADS * MLA_V), jnp.bfloat16),
        compiler_params=pltpu.CompilerParams(
            dimension_semantics=("arbitrary", "arbitrary"), vmem_limit_bytes=VMEM_LIMIT_BYTES),
        name="mla_attn",
    )(q, k, v, km, vm, mask, proj)


def _diff_attn_kernel(qd_ref, kd_ref, vd_ref, kdm_ref, vdm_ref, btab_ref, mtab_ref, lam_ref,
                      subln_ref, zb_ref, o_ref):
    n_chunks = kd_ref.shape[0] // KC
    lv = lam_ref[...]
    lam = (jnp.exp(jnp.sum(lv[0:1] * lv[1:2], axis=-1, keepdims=True))
           - jnp.exp(jnp.sum(lv[2:3] * lv[3:4], axis=-1, keepdims=True)) + LAM_INIT)
    subln = subln_ref[...]
    v = vd_ref[...]
    vm = vdm_ref[...]

    def softmax_parts(qm, half, qi):
        lo = half * DIFF_QK
        chunks = []
        for c in range(n_chunks):
            kc = kd_ref[c * KC:(c + 1) * KC, lo:lo + DIFF_QK]
            tile = btab_ref[0, jnp.clip(c - qi, -2, 2) + 2]
            chunks.append(_dot_nt(qm, kc) + tile)
        s = jnp.concatenate(chunks, axis=-1)
        sm = _dot_nt(qm, kdm_ref[:, lo:lo + DIFF_QK]) + mtab_ref[0, jnp.minimum(qi, 1)]
        m = jnp.maximum(jnp.max(s, axis=-1, keepdims=True), jnp.max(sm, axis=-1, keepdims=True))
        p = jnp.exp(s - m)
        pm = jnp.exp(sm - m)
        l = jnp.sum(p, axis=-1, keepdims=True) + jnp.sum(pm, axis=-1, keepdims=True)
        return p, pm, 1.0 / l

    def body(qi, carry):
        r0 = pl.multiple_of(qi * TQ, TQ)
        q1 = qd_ref[pl.ds(r0, TQ), :DIFF_QK]
        q2 = qd_ref[pl.ds(r0, TQ), DIFF_QK:]
        p1, pm1, inv1 = softmax_parts(q1, 0, qi)
        p2, pm2, inv2 = softmax_parts(q2, 1, qi)
        w2 = lam * inv2
        a = (p1 * inv1 - p2 * w2).astype(jnp.bfloat16)
        am = (pm1 * inv1 - pm2 * w2).astype(jnp.bfloat16)
        o = _dot(a, v) + _dot(am, vm)
        y = o * _rms_scale(o, DIFF_V) * subln * (1.0 - LAM_INIT)
        y = y * zb_ref[pl.ds(r0, TQ), :].astype(jnp.float32)
        o_ref[pl.ds(r0, TQ), :] = y.astype(o_ref.dtype)
        return carry

    lax.fori_loop(0, qd_ref.shape[0] // TQ, body, 0)


def _diff_attn(proj, proj_meta, btab, mtab, lam, subln, nb, seq):
    cps = D_MODEL // DIFF_V
    return pl.pallas_call(
        _diff_attn_kernel,
        grid=(nb, DIFF_HEADS),
        in_specs=[
            pl.BlockSpec((seq, DIFF_V), lambda b, h: (b, SEG_QD * cps + h)),
            pl.BlockSpec((seq, DIFF_V), lambda b, h: (b, SEG_KD * cps + h)),
            pl.BlockSpec((seq, DIFF_V), lambda b, h: (b, SEG_VD * cps + h)),
            pl.BlockSpec((META_PAD, DIFF_V), lambda b, h: (0, h)),
            pl.BlockSpec((META_PAD, DIFF_V), lambda b, h: (0, cps + h)),
            pl.BlockSpec((1, 5, TQ, KC), lambda b, h: (h, 0, 0, 0)),
            pl.BlockSpec((1, 2, TQ, META_PAD), lambda b, h: (h, 0, 0, 0)),
            pl.BlockSpec((4, DIFF_QK), lambda b, h: (0, 0)),
            pl.BlockSpec((1, DIFF_V), lambda b, h: (0, 0)),
            pl.BlockSpec((seq, DIFF_V), lambda b, h: (b, SEG_ZB * cps + h)),
        ],
        out_specs=pl.BlockSpec((seq, DIFF_V), lambda b, h: (b, h)),
        out_shape=jax.ShapeDtypeStruct((nb * seq, DIFF_HEADS * DIFF_V), jnp.bfloat16),
        compiler_params=pltpu.CompilerParams(
            dimension_semantics=("arbitrary", "arbitrary"), vmem_limit_bytes=VMEM_LIMIT_BYTES),
        name="diff_attn",
    )(proj, proj, proj, proj_meta, proj_meta, btab, mtab, lam, subln, proj)


def _merge_kernel(oa_ref, ob_ref, wa_ref, wb_ref, ga_ref, gb_ref, m_ref):
    ya = _dot(oa_ref[...], wa_ref[...])
    yb = _dot(ob_ref[...], wb_ref[...])
    m = ga_ref[...].astype(jnp.float32) * ya + gb_ref[...].astype(jnp.float32) * yb
    m_ref[...] = m.astype(m_ref.dtype)


def _merge(oa, ob, wa, wb, proj, tm, tn):
    rows = oa.shape[0]
    tps = D_MODEL // tn
    return pl.pallas_call(
        _merge_kernel,
        grid=(rows // tm, D_MODEL // tn),
        in_specs=[
            pl.BlockSpec((tm, D_MODEL), lambda i, j: (i, 0)),
            pl.BlockSpec((tm, D_MODEL), lambda i, j: (i, 0)),
            pl.BlockSpec((D_MODEL, tn), lambda i, j: (0, j)),
            pl.BlockSpec((D_MODEL, tn), lambda i, j: (0, j)),
            pl.BlockSpec((tm, tn), lambda i, j: (i, SEG_GA * tps + j)),
            pl.BlockSpec((tm, tn), lambda i, j: (i, SEG_GB * tps + j)),
        ],
        out_specs=pl.BlockSpec((tm, tn), lambda i, j: (i, j)),
        out_shape=jax.ShapeDtypeStruct((rows, D_MODEL), jnp.bfloat16),
        compiler_params=pltpu.CompilerParams(
            dimension_semantics=("arbitrary", "arbitrary"), vmem_limit_bytes=VMEM_LIMIT_BYTES),
        name="merge",
    )(oa, ob, wa, wb, proj, proj)


def _out_kernel(m_ref, w_ref, x_ref, o_ref):
    o_ref[...] = x_ref[...] + _dot(m_ref[...], w_ref[...])


def _out_proj(m, wout, x2, tm, tn):
    rows = m.shape[0]
    return pl.pallas_call(
        _out_kernel,
        grid=(rows // tm, D_MODEL // tn),
        in_specs=[
            pl.BlockSpec((tm, D_MODEL), lambda i, j: (i, 0)),
            pl.BlockSpec((D_MODEL, tn), lambda i, j: (0, j)),
            pl.BlockSpec((tm, tn), lambda i, j: (i, j)),
        ],
        out_specs=pl.BlockSpec((tm, tn), lambda i, j: (i, j)),
        out_shape=jax.ShapeDtypeStruct((rows, D_MODEL), jnp.float32),
        compiler_params=pltpu.CompilerParams(
            dimension_semantics=("arbitrary", "arbitrary"), vmem_limit_bytes=VMEM_LIMIT_BYTES),
        name="out_proj",
    )(m, wout, x2)


def _t5_bucket(rel):
    nb = REL_BUCKETS // 2
    max_exact = nb // 2
    ret = jnp.where(rel > 0, nb, 0)
    n = jnp.abs(rel)
    nf = jnp.maximum(n, 1).astype(jnp.float32)
    large = max_exact + (jnp.log(nf / max_exact) / math.log(REL_MAX_DIST / max_exact)
                         * (nb - max_exact)).astype(jnp.int32)
    large = jnp.minimum(large, nb - 1)
    return ret + jnp.where(n < max_exact, n, large)


def _bias_tables(rel_bias):
    qq = jnp.arange(TQ, dtype=jnp.int32)[:, None]
    kk = jnp.arange(KC, dtype=jnp.int32)[None, :]
    dd = jnp.arange(-2, 3, dtype=jnp.int32)[:, None, None]
    rel = dd * KC + kk[None] - qq[None]
    btab = jnp.transpose(rel_bias[_t5_bucket(rel)], (3, 0, 1, 2)).astype(jnp.float32)
    jm = jnp.arange(META_PAD, dtype=jnp.int32)[None, None, :]
    qpos = N_META + jnp.arange(2, dtype=jnp.int32)[:, None, None] * TQ + qq[None]
    relm = jm - qpos
    mt = jnp.transpose(rel_bias[_t5_bucket(relm)], (3, 0, 1, 2)).astype(jnp.float32)
    mtab = jnp.where(jm[None] < N_META, mt, MASK_VALUE)
    return btab, mtab


def _rope_tables(n_pos):
    half = MLA_ROPE // 2
    inv = ROPE_THETA ** (-jnp.arange(half, dtype=jnp.float32) / half)
    ang = jnp.arange(n_pos, dtype=jnp.int32).astype(jnp.float32)[:, None] * inv[None, :]
    c, s = jnp.cos(ang), jnp.sin(ang)
    z = jnp.zeros((n_pos, LANES - MLA_ROPE), jnp.float32)
    cos_t = jnp.concatenate([c, c, z], axis=-1)
    sin_t = jnp.concatenate([-s, s, z], axis=-1)
    return cos_t, sin_t


def _pad_rows(a, rows):
    return jnp.pad(a, [(0, rows - a.shape[0])] + [(0, 0)] * (a.ndim - 1))


def kernel(x, meta_tokens, rel_bias, norm_in, w_in, q_a_norm, kv_a_norm, w_uq, w_ukv, mla_q_norm, mla_k_norm,
           diff_q_norm, diff_k_norm, diff_lambda, diff_subln, w_branch_a, w_branch_b, w_out):
    nb, seq, _ = x.shape
    bf = jnp.bfloat16
    n_small = MLA_Q_RANK + MLA_KV_RANK + MLA_ROPE

    w_in0 = w_in[0]
    wsm = jnp.pad(w_in0[:, :n_small], ((0, 0), (0, SMALL_W - n_small))).astype(bf)
    wbig = w_in0[:, n_small:].astype(bf)
    wuq = jnp.pad(w_uq[0].reshape(MLA_Q_RANK, MLA_HEADS, MLA_QK),
                  ((0, 0), (0, 0), (0, HEAD_PAD - MLA_QK))).reshape(MLA_Q_RANK, MLA_HEADS * HEAD_PAD).astype(bf)
    wukv = w_ukv[0].astype(bf)
    wa, wb, wo = w_branch_a[0].astype(bf), w_branch_b[0].astype(bf), w_out[0].astype(bf)

    gin = norm_in[0][None]
    gqa, gkva = q_a_norm[0][None], kv_a_norm[0][None]
    gq = jnp.pad(mla_q_norm[0], (0, HEAD_PAD - MLA_QK))[None]
    gk = jnp.pad(mla_k_norm[0], (0, HEAD_PAD - MLA_QK))[None]
    gqd, gkd = diff_q_norm[0][None], diff_k_norm[0][None]
    subln = diff_subln[0][None]
    lam = diff_lambda[0]

    cos_t, sin_t = _rope_tables(N_META + seq)
    btab, mtab = _bias_tables(rel_bias)
    mla_mask = jnp.where(jnp.arange(META_PAD) < N_META, 0.0, MASK_VALUE).astype(jnp.float32)[None]

    x2 = x.reshape(nb * seq, D_MODEL)
    meta = meta_tokens.astype(x.dtype)

    q, k, v = _mla_proj(x, gin, wsm, gqa, gkva, wuq, wukv, gq, gk, cos_t[N_META:], sin_t[N_META:], tm=512)
    proj = _in_proj(x2, gin, wbig, gqd, gkd, tm=1024, tn=1024, seg0=0, nseg=N_SEG)

    _, km, vm = _mla_proj(meta[None], gin, wsm, gqa, gkva, wuq, wukv, gq, gk,
                          cos_t[:N_META], sin_t[:N_META], tm=N_META)
    km = jnp.pad(km[0], ((0, 0), (0, META_PAD - N_META), (0, 0)))
    vm = jnp.pad(vm[0], ((0, 0), (0, META_PAD - N_META), (0, 0)))
    proj_meta = _in_proj(meta, gin, wbig, gqd, gkd, tm=N_META, tn=1024, seg0=SEG_KD, nseg=2)
    proj_meta = _pad_rows(proj_meta, META_PAD)

    oa = _mla_attn(q, k, v, km, vm, mla_mask, proj, seq)
    ob = _diff_attn(proj, proj_meta, btab, mtab, lam, subln, nb, seq)
    m = _merge(oa, ob, wa, wb, proj, tm=1024, tn=1024)
    out = _out_proj(m, wo, x2, tm=1024, tn=1024)
    return out.reshape(nb, seq, D_MODEL)
```

```python
import functools
import math

import jax
import jax.numpy as jnp
from jax import lax
from jax.experimental import pallas as pl
from jax.experimental.pallas import tpu as pltpu

D_MODEL = 2048
N_META = 16
EPS = 1e-6

MLA_HEADS = 16
MLA_Q_RANK = 768
MLA_KV_RANK = 512
MLA_NOPE = 128
MLA_ROPE = 64
MLA_V = 128
MLA_QK = MLA_NOPE + MLA_ROPE
ROPE_THETA = 10000.0

DIFF_HEADS = 8
DIFF_QK = 128
DIFF_V = 2 * DIFF_QK
LAM_INIT = 0.8 - 0.6 * math.exp(-0.3 * 0)

REL_BUCKETS = 32
REL_MAX_DIST = 128

LANES = 128
MXU_DIM = 256
VMEM_LIMIT_BYTES = 56 * 1024 * 1024

HEAD_PAD = MXU_DIM
SMALL_W = MLA_Q_RANK + MLA_KV_RANK + LANES
META_PAD = LANES
MASK_VALUE = -1e30
TQ = 256
KC = 256

SEG_ZA, SEG_QD, SEG_KD, SEG_VD, SEG_ZB, SEG_GA, SEG_GB = range(7)
N_SEG = 7


def _dot(a, b):
    return jnp.dot(a, b, preferred_element_type=jnp.float32)


def _dot_nt(a, b):
    return lax.dot_general(a, b, (((1,), (1,)), ((), ())), preferred_element_type=jnp.float32)


def _rms_scale(x, width):
    return lax.rsqrt(jnp.sum(x * x, axis=-1, keepdims=True) * (1.0 / width) + EPS)


def _rot_half64(x):
    return pltpu.roll(x, 32, 1) + pltpu.roll(x, 96, 1)


def _mla_proj_kernel(x_ref, gin_ref, wsm_ref, gqa_ref, gkva_ref, wuq_ref, wukv_ref,
                     gq_ref, gk_ref, cos_ref, sin_ref, q_ref, k_ref, v_ref, *, q_scale):
    x = x_ref[...]
    u = (x * _rms_scale(x, D_MODEL) * gin_ref[...]).astype(jnp.bfloat16)
    p = _dot(u, wsm_ref[...])
    cq = p[:, :MLA_Q_RANK]
    ckv = p[:, MLA_Q_RANK:MLA_Q_RANK + MLA_KV_RANK]
    kr = p[:, MLA_Q_RANK + MLA_KV_RANK:]
    cqn = (cq * _rms_scale(cq, MLA_Q_RANK) * gqa_ref[...]).astype(jnp.bfloat16)
    ckvn = (ckv * _rms_scale(ckv, MLA_KV_RANK) * gkva_ref[...]).astype(jnp.bfloat16)
    qf = _dot(cqn, wuq_ref[...])
    kvf = _dot(ckvn, wukv_ref[...])

    cos = cos_ref[...]
    sin = sin_ref[...]
    gq = gq_ref[...]
    gk = gk_ref[...]
    gq_nope, gq_rope = gq[:, :LANES], gq[:, LANES:]
    gk_nope, gk_rope = gk[:, :LANES], gk[:, LANES:]

    ss_kr = jnp.sum(kr * kr, axis=-1, keepdims=True)
    krg = kr * gk_rope
    kr_roped = krg * cos + _rot_half64(krg) * sin

    for h in range(MLA_HEADS):
        qh = qf[:, h * HEAD_PAD:(h + 1) * HEAD_PAD]
        rq = _rms_scale(qh, MLA_QK)
        q_nope = qh[:, :LANES] * rq * gq_nope
        q_r = qh[:, LANES:] * rq * gq_rope
        q_r = q_r * cos + _rot_half64(q_r) * sin
        q_ref[0, h, :, :LANES] = (q_nope * q_scale).astype(jnp.bfloat16)
        q_ref[0, h, :, LANES:] = (q_r * q_scale).astype(jnp.bfloat16)

        k_nope = kvf[:, h * HEAD_PAD:h * HEAD_PAD + MLA_NOPE]
        rk = lax.rsqrt((jnp.sum(k_nope * k_nope, axis=-1, keepdims=True) + ss_kr) * (1.0 / MLA_QK) + EPS)
        k_ref[0, h, :, :LANES] = (k_nope * rk * gk_nope).astype(jnp.bfloat16)
        k_ref[0, h, :, LANES:] = (kr_roped * rk).astype(jnp.bfloat16)
        v_ref[0, h] = kvf[:, h * HEAD_PAD + MLA_NOPE:(h + 1) * HEAD_PAD].astype(jnp.bfloat16)


def _mla_proj(x3, gin, wsm, gqa, gkva, wuq, wukv, gq, gk, cos, sin, tm):
    nb, rows, _ = x3.shape
    const = lambda b, i: (0, 0)
    kern = functools.partial(_mla_proj_kernel, q_scale=MLA_QK ** -0.5)
    return pl.pallas_call(
        kern,
        grid=(nb, rows // tm),
        in_specs=[
            pl.BlockSpec((None, tm, D_MODEL), lambda b, i: (b, i, 0)),
            pl.BlockSpec((1, D_MODEL), const),
            pl.BlockSpec((D_MODEL, SMALL_W), const),
            pl.BlockSpec((1, MLA_Q_RANK), const),
            pl.BlockSpec((1, MLA_KV_RANK), const),
            pl.BlockSpec((MLA_Q_RANK, MLA_HEADS * HEAD_PAD), const),
            pl.BlockSpec((MLA_KV_RANK, MLA_HEADS * HEAD_PAD), const),
            pl.BlockSpec((1, HEAD_PAD), const),
            pl.BlockSpec((1, HEAD_PAD), const),
            pl.BlockSpec((tm, LANES), lambda b, i: (i, 0)),
            pl.BlockSpec((tm, LANES), lambda b, i: (i, 0)),
        ],
        out_specs=[
            pl.BlockSpec((1, MLA_HEADS, tm, HEAD_PAD), lambda b, i: (b, 0, i, 0)),
            pl.BlockSpec((1, MLA_HEADS, tm, HEAD_PAD), lambda b, i: (b, 0, i, 0)),
            pl.BlockSpec((1, MLA_HEADS, tm, MLA_V), lambda b, i: (b, 0, i, 0)),
        ],
        out_shape=[
            jax.ShapeDtypeStruct((nb, MLA_HEADS, rows, HEAD_PAD), jnp.bfloat16),
            jax.ShapeDtypeStruct((nb, MLA_HEADS, rows, HEAD_PAD), jnp.bfloat16),
            jax.ShapeDtypeStruct((nb, MLA_HEADS, rows, MLA_V), jnp.bfloat16),
        ],
        compiler_params=pltpu.CompilerParams(
            dimension_semantics=("arbitrary", "arbitrary"), vmem_limit_bytes=VMEM_LIMIT_BYTES),
        name="mla_proj",
    )(x3, gin, wsm, gqa, gkva, wuq, wukv, gq, gk, cos, sin)


def _group_norm_store(acc, gain, scale, o_ref):
    for c in range(acc.shape[1] // DIFF_QK):
        g = acc[:, c * DIFF_QK:(c + 1) * DIFF_QK]
        y = g * _rms_scale(g, DIFF_QK) * gain
        if scale != 1.0:
            y = y * scale
        o_ref[:, c * DIFF_QK:(c + 1) * DIFF_QK] = y.astype(o_ref.dtype)


def _in_proj_kernel(x_ref, gin_ref, w_ref, gqd_ref, gkd_ref, o_ref, u_ref, *, seg0, tiles_per_seg, qd_scale):
    j = pl.program_id(1)

    @pl.when(j == 0)
    def _():
        x = x_ref[...]
        u_ref[...] = (x * _rms_scale(x, D_MODEL) * gin_ref[...]).astype(u_ref.dtype)

    acc = _dot(u_ref[...], w_ref[...])
    seg = seg0 + j // tiles_per_seg

    @pl.when((seg == SEG_ZA) | (seg == SEG_ZB))
    def _():
        o_ref[...] = (acc / (1.0 + jnp.exp(-acc))).astype(o_ref.dtype)

    @pl.when((seg == SEG_GA) | (seg == SEG_GB))
    def _():
        o_ref[...] = (1.0 / (1.0 + jnp.exp(-acc))).astype(o_ref.dtype)

    @pl.when(seg == SEG_VD)
    def _():
        o_ref[...] = acc.astype(o_ref.dtype)

    @pl.when(seg == SEG_QD)
    def _():
        _group_norm_store(acc, gqd_ref[...], qd_scale, o_ref)

    @pl.when(seg == SEG_KD)
    def _():
        _group_norm_store(acc, gkd_ref[...], 1.0, o_ref)


def _in_proj(x2, gin, wbig, gqd, gkd, tm, tn, seg0, nseg):
    rows = x2.shape[0]
    tiles_per_seg = D_MODEL // tn
    j0 = seg0 * tiles_per_seg
    kern = functools.partial(_in_proj_kernel, seg0=seg0, tiles_per_seg=tiles_per_seg, qd_scale=DIFF_QK ** -0.5)
    return pl.pallas_call(
        kern,
        grid=(rows // tm, nseg * tiles_per_seg),
        in_specs=[
            pl.BlockSpec((tm, D_MODEL), lambda i, j: (i, 0)),
            pl.BlockSpec((1, D_MODEL), lambda i, j: (0, 0)),
            pl.BlockSpec((D_MODEL, tn), lambda i, j: (0, j0 + j)),
            pl.BlockSpec((1, DIFF_QK), lambda i, j: (0, 0)),
            pl.BlockSpec((1, DIFF_QK), lambda i, j: (0, 0)),
        ],
        out_specs=pl.BlockSpec((tm, tn), lambda i, j: (i, j)),
        out_shape=jax.ShapeDtypeStruct((rows, nseg * D_MODEL), jnp.bfloat16),
        scratch_shapes=[pltpu.VMEM((tm, D_MODEL), jnp.bfloat16)],
        compiler_params=pltpu.CompilerParams(
            dimension_semantics=("arbitrary", "arbitrary"), vmem_limit_bytes=VMEM_LIMIT_BYTES),
        name="in_proj",
    )(x2, gin, wbig, gqd, gkd)


def _mla_attn_kernel(q_ref, k_ref, v_ref, km_ref, vm_ref, mask_ref, za_ref, o_ref):
    k = k_ref[0, 0]
    v = v_ref[0, 0]
    km = km_ref[0]
    vm = vm_ref[0]
    mask = mask_ref[...]

    def body(qi, carry):
        r0 = pl.multiple_of(qi * TQ, TQ)
        qt = q_ref[0, 0, pl.ds(r0, TQ), :]
        s = _dot_nt(qt, k)
        sm = _dot_nt(qt, km) + mask
        m = jnp.maximum(jnp.max(s, axis=-1, keepdims=True), jnp.max(sm, axis=-1, keepdims=True))
        p = jnp.exp(s - m)
        pm = jnp.exp(sm - m)
        l = jnp.sum(p, axis=-1, keepdims=True) + jnp.sum(pm, axis=-1, keepdims=True)
        o = _dot(p.astype(jnp.bfloat16), v) + _dot(pm.astype(jnp.bfloat16), vm)
        o = o * (1.0 / l) * za_ref[pl.ds(r0, TQ), :].astype(jnp.float32)
        o_ref[pl.ds(r0, TQ), :] = o.astype(o_ref.dtype)
        return carry

    lax.fori_loop(0, q_ref.shape[2] // TQ, body, 0)


def _mla_attn(q, k, v, km, vm, mask, proj, seq):
    nb = q.shape[0]
    return pl.pallas_call(
        _mla_attn_kernel,
        grid=(nb, MLA_HEADS),
        in_specs=[
            pl.BlockSpec((1, 1, seq, HEAD_PAD), lambda b, h: (b, h, 0, 0)),
            pl.BlockSpec((1, 1, seq, HEAD_PAD), lambda b, h: (b, h, 0, 0)),
            pl.BlockSpec((1, 1, seq, MLA_V), lambda b, h: (b, h, 0, 0)),
            pl.BlockSpec((1, META_PAD, HEAD_PAD), lambda b, h: (h, 0, 0)),
            pl.BlockSpec((1, META_PAD, MLA_V), lambda b, h: (h, 0, 0)),
            pl.BlockSpec((1, META_PAD), lambda b, h: (0, 0)),
            pl.BlockSpec((seq, MLA_V), lambda b, h: (b, SEG_ZA * (D_MODEL // MLA_V) + h)),
        ],
        out_specs=pl.BlockSpec((seq, MLA_V), lambda b, h: (b, h)),
        out_shape=jax.ShapeDtypeStruct((nb * seq, MLA_HEADS * MLA_V), jnp.bfloat16),
        compiler_params=pltpu.CompilerParams(
            dimension_semantics=("arbitrary", "arbitrary"), vmem_limit_bytes=VMEM_LIMIT_BYTES),
        name="mla_attn",
    )(q, k, v, km, vm, mask, proj)


def _diff_attn_kernel(qd_ref, kd_ref, vd_ref, kdm_ref, vdm_ref, btab_ref, mtab_ref, lam_ref,
                      subln_ref, zb_ref, o_ref):
    n_chunks = kd_ref.shape[0] // KC
    lv = lam_ref[...]
    lam = (jnp.exp(jnp.sum(lv[0:1] * lv[1:2], axis=-1, keepdims=True))
           - jnp.exp(jnp.sum(lv[2:3] * lv[3:4], axis=-1, keepdims=True)) + LAM_INIT)
    subln = subln_ref[...]
    v = vd_ref[...]
    vm = vdm_ref[...]

    def softmax_parts(qm, half, qi):
        lo = half * DIFF_QK
        chunks = []
        for c in range(n_chunks):
            kc = kd_ref[c * KC:(c + 1) * KC, lo:lo + DIFF_QK]
            tile = btab_ref[0, jnp.clip(c - qi, -2, 2) + 2]
            chunks.append(_dot_nt(qm, kc) + tile)
        s = jnp.concatenate(chunks, axis=-1)
        sm = _dot_nt(qm, kdm_ref[:, lo:lo + DIFF_QK]) + mtab_ref[0, jnp.minimum(qi, 1)]
        m = jnp.maximum(jnp.max(s, axis=-1, keepdims=True), jnp.max(sm, axis=-1, keepdims=True))
        p = jnp.exp(s - m)
        pm = jnp.exp(sm - m)
        l = jnp.sum(p, axis=-1, keepdims=True) + jnp.sum(pm, axis=-1, keepdims=True)
        return p, pm, 1.0 / l

    def body(qi, carry):
        r0 = pl.multiple_of(qi * TQ, TQ)
        q1 = qd_ref[pl.ds(r0, TQ), :DIFF_QK]
        q2 = qd_ref[pl.ds(r0, TQ), DIFF_QK:]
        p1, pm1, inv1 = softmax_parts(q1, 0, qi)
        p2, pm2, inv2 = softmax_parts(q2, 1, qi)
        w2 = lam * inv2
        a = (p1 * inv1 - p2 * w2).astype(jnp.bfloat16)
        am = (pm1 * inv1 - pm2 * w2).astype(jnp.bfloat16)
        o = _dot(a, v) + _dot(am, vm)
        y = o * _rms_scale(o, DIFF_V) * subln * (1.0 - LAM_INIT)
        y = y * zb_ref[pl.ds(r0, TQ), :].astype(jnp.float32)
        o_ref[pl.ds(r0, TQ), :] = y.astype(o_ref.dtype)
        return carry

    lax.fori_loop(0, qd_ref.shape[0] // TQ, body, 0)


def _diff_attn(proj, proj_meta, btab, mtab, lam, subln, nb, seq):
    cps = D_MODEL // DIFF_V
    return pl.pallas_call(
        _diff_attn_kernel,
        grid=(nb, DIFF_HEADS),
        in_specs=[
            pl.BlockSpec((seq, DIFF_V), lambda b, h: (b, SEG_QD * cps + h)),
            pl.BlockSpec((seq, DIFF_V), lambda b, h: (b, SEG_KD * cps + h)),
            pl.BlockSpec((seq, DIFF_V), lambda b, h: (b, SEG_VD * cps + h)),
            pl.BlockSpec((META_PAD, DIFF_V), lambda b, h: (0, h)),
            pl.BlockSpec((META_PAD, DIFF_V), lambda b, h: (0, cps + h)),
            pl.BlockSpec((1, 5, TQ, KC), lambda b, h: (h, 0, 0, 0)),
            pl.BlockSpec((1, 2, TQ, META_PAD), lambda b, h: (h, 0, 0, 0)),
            pl.BlockSpec((4, DIFF_QK), lambda b, h: (0, 0)),
            pl.BlockSpec((1, DIFF_V), lambda b, h: (0, 0)),
            pl.BlockSpec((seq, DIFF_V), lambda b, h: (b, SEG_ZB * cps + h)),
        ],
        out_specs=pl.BlockSpec((seq, DIFF_V), lambda b, h: (b, h)),
        out_shape=jax.ShapeDtypeStruct((nb * seq, DIFF_HEADS * DIFF_V), jnp.bfloat16),
        compiler_params=pltpu.CompilerParams(
            dimension_semantics=("arbitrary", "arbitrary"), vmem_limit_bytes=VMEM_LIMIT_BYTES),
        name="diff_attn",
    )(proj, proj, proj, proj_meta, proj_meta, btab, mtab, lam, subln, proj)


def _merge_kernel(oa_ref, ob_ref, wa_ref, wb_ref, ga_ref, gb_ref, m_ref):
    ya = _dot(oa_ref[...], wa_ref[...])
    yb = _dot(ob_ref[...], wb_ref[...])
    m = ga_ref[...].astype(jnp.float32) * ya + gb_ref[...].astype(jnp.float32) * yb
    m_ref[...] = m.astype(m_ref.dtype)


def _merge(oa, ob, wa, wb, proj, tm, tn):
    rows = oa.shape[0]
    tps = D_MODEL // tn
    return pl.pallas_call(
        _merge_kernel,
        grid=(rows // tm, D_MODEL // tn),
        in_specs=[
            pl.BlockSpec((tm, D_MODEL), lambda i, j: (i, 0)),
            pl.BlockSpec((tm, D_MODEL), lambda i, j: (i, 0)),
            pl.BlockSpec((D_MODEL, tn), lambda i, j: (0, j)),
            pl.BlockSpec((D_MODEL, tn), lambda i, j: (0, j)),
            pl.BlockSpec((tm, tn), lambda i, j: (i, SEG_GA * tps + j)),
            pl.BlockSpec((tm, tn), lambda i, j: (i, SEG_GB * tps + j)),
        ],
        out_specs=pl.BlockSpec((tm, tn), lambda i, j: (i, j)),
        out_shape=jax.ShapeDtypeStruct((rows, D_MODEL), jnp.bfloat16),
        compiler_params=pltpu.CompilerParams(
            dimension_semantics=("arbitrary", "arbitrary"), vmem_limit_bytes=VMEM_LIMIT_BYTES),
        name="merge",
    )(oa, ob, wa, wb, proj, proj)


def _out_kernel(m_ref, w_ref, x_ref, o_ref):
    o_ref[...] = x_ref[...] + _dot(m_ref[...], w_ref[...])


def _out_proj(m, wout, x2, tm, tn):
    rows = m.shape[0]
    return pl.pallas_call(
        _out_kernel,
        grid=(rows // tm, D_MODEL // tn),
        in_specs=[
            pl.BlockSpec((tm, D_MODEL), lambda i, j: (i, 0)),
            pl.BlockSpec((D_MODEL, tn), lambda i, j: (0, j)),
            pl.BlockSpec((tm, tn), lambda i, j: (i, j)),
        ],
        out_specs=pl.BlockSpec((tm, tn), lambda i, j: (i, j)),
        out_shape=jax.ShapeDtypeStruct((rows, D_MODEL), jnp.float32),
        compiler_params=pltpu.CompilerParams(
            dimension_semantics=("arbitrary", "arbitrary"), vmem_limit_bytes=VMEM_LIMIT_BYTES),
        name="out_proj",
    )(m, wout, x2)


def _t5_bucket(rel):
    nb = REL_BUCKETS // 2
    max_exact = nb // 2
    ret = jnp.where(rel > 0, nb, 0)
    n = jnp.abs(rel)
    nf = jnp.maximum(n, 1).astype(jnp.float32)
    large = max_exact + (jnp.log(nf / max_exact) / math.log(REL_MAX_DIST / max_exact)
                         * (nb - max_exact)).astype(jnp.int32)
    large = jnp.minimum(large, nb - 1)
    return ret + jnp.where(n < max_exact, n, large)


def _bias_tab_kernel(rb_ref, bk_ref, mbk_ref, btab_ref, mtab_ref):
    h = pl.program_id(0)
    bk = bk_ref[...]
    mbk = mbk_ref[...]
    acc = jnp.zeros(bk.shape, jnp.float32)
    macc = jnp.where(mbk < 0, MASK_VALUE, 0.0).astype(jnp.float32)
    for b in range(REL_BUCKETS):
        val = rb_ref[b, h]
        acc = jnp.where(bk == b, val, acc)
        macc = jnp.where(mbk == b, val, macc)
    btab_ref[0] = acc
    mtab_ref[0] = macc


def _bias_tables(rel_bias):
    qq = jnp.arange(TQ, dtype=jnp.int32)[:, None]
    kk = jnp.arange(KC, dtype=jnp.int32)[None, :]
    dd = jnp.arange(-2, 3, dtype=jnp.int32)[:, None, None]
    bk = _t5_bucket(dd * KC + kk[None] - qq[None])
    jm = jnp.arange(META_PAD, dtype=jnp.int32)[None, None, :]
    qpos = N_META + jnp.arange(2, dtype=jnp.int32)[:, None, None] * TQ + qq[None]
    mbk = jnp.where(jm < N_META, _t5_bucket(jm - qpos), -1)
    return pl.pallas_call(
        _bias_tab_kernel,
        grid=(DIFF_HEADS,),
        in_specs=[
            pl.BlockSpec(memory_space=pltpu.SMEM),
            pl.BlockSpec((5, TQ, KC), lambda h: (0, 0, 0)),
            pl.BlockSpec((2, TQ, META_PAD), lambda h: (0, 0, 0)),
        ],
        out_specs=[
            pl.BlockSpec((1, 5, TQ, KC), lambda h: (h, 0, 0, 0)),
            pl.BlockSpec((1, 2, TQ, META_PAD), lambda h: (h, 0, 0, 0)),
        ],
        out_shape=[
            jax.ShapeDtypeStruct((DIFF_HEADS, 5, TQ, KC), jnp.float32),
            jax.ShapeDtypeStruct((DIFF_HEADS, 2, TQ, META_PAD), jnp.float32),
        ],
        compiler_params=pltpu.CompilerParams(dimension_semantics=("arbitrary",)),
        name="bias_tables",
    )(rel_bias.astype(jnp.float32), bk, mbk)


def _rope_tables(n_pos):
    half = MLA_ROPE // 2
    inv = ROPE_THETA ** (-jnp.arange(half, dtype=jnp.float32) / half)
    ang = jnp.arange(n_pos, dtype=jnp.int32).astype(jnp.float32)[:, None] * inv[None, :]
    c, s = jnp.cos(ang), jnp.sin(ang)
    z = jnp.zeros((n_pos, LANES - MLA_ROPE), jnp.float32)
    cos_t = jnp.concatenate([c, c, z], axis=-1)
    sin_t = jnp.concatenate([-s, s, z], axis=-1)
    return cos_t, sin_t


def _pad_rows(a, rows):
    return jnp.pad(a, [(0, rows - a.shape[0])] + [(0, 0)] * (a.ndim - 1))


def kernel(x, meta_tokens, rel_bias, norm_in, w_in, q_a_norm, kv_a_norm, w_uq, w_ukv, mla_q_norm, mla_k_norm,
           diff_q_norm, diff_k_norm, diff_lambda, diff_subln, w_branch_a, w_branch_b, w_out):
    nb, seq, _ = x.shape
    bf = jnp.bfloat16
    n_small = MLA_Q_RANK + MLA_KV_RANK + MLA_ROPE

    w_in0 = w_in[0]
    wsm = jnp.pad(w_in0[:, :n_small], ((0, 0), (0, SMALL_W - n_small))).astype(bf)
    wbig = w_in0[:, n_small:].astype(bf)
    wuq = jnp.pad(w_uq[0].reshape(MLA_Q_RANK, MLA_HEADS, MLA_QK),
                  ((0, 0), (0, 0), (0, HEAD_PAD - MLA_QK))).reshape(MLA_Q_RANK, MLA_HEADS * HEAD_PAD).astype(bf)
    wukv = w_ukv[0].astype(bf)
    wa, wb, wo = w_branch_a[0].astype(bf), w_branch_b[0].astype(bf), w_out[0].astype(bf)

    gin = norm_in[0][None]
    gqa, gkva = q_a_norm[0][None], kv_a_norm[0][None]
    gq = jnp.pad(mla_q_norm[0], (0, HEAD_PAD - MLA_QK))[None]
    gk = jnp.pad(mla_k_norm[0], (0, HEAD_PAD - MLA_QK))[None]
    gqd, gkd = diff_q_norm[0][None], diff_k_norm[0][None]
    subln = diff_subln[0][None]
    lam = diff_lambda[0]

    cos_t, sin_t = _rope_tables(N_META + seq)
    btab, mtab = _bias_tables(rel_bias)
    mla_mask = jnp.where(jnp.arange(META_PAD) < N_META, 0.0, MASK_VALUE).astype(jnp.float32)[None]

    x2 = x.reshape(nb * seq, D_MODEL)
    meta = meta_tokens.astype(x.dtype)

    q, k, v = _mla_proj(x, gin, wsm, gqa, gkva, wuq, wukv, gq, gk, cos_t[N_META:], sin_t[N_META:], tm=512)
    proj = _in_proj(x2, gin, wbig, gqd, gkd, tm=1024, tn=1024, seg0=0, nseg=N_SEG)

    _, km, vm = _mla_proj(meta[None], gin, wsm, gqa, gkva, wuq, wukv, gq, gk,
                          cos_t[:N_META], sin_t[:N_META], tm=N_META)
    km = jnp.pad(km[0], ((0, 0), (0, META_PAD - N_META), (0, 0)))
    vm = jnp.pad(vm[0], ((0, 0), (0, META_PAD - N_META), (0, 0)))
    proj_meta = _in_proj(meta, gin, wbig, gqd, gkd, tm=N_META, tn=1024, seg0=SEG_KD, nseg=2)
    proj_meta = _pad_rows(proj_meta, META_PAD)

    oa = _mla_attn(q, k, v, km, vm, mla_mask, proj, seq)
    ob = _diff_attn(proj, proj_meta, btab, mtab, lam, subln, nb, seq)
    m = _merge(oa, ob, wa, wb, proj, tm=1024, tn=1024)
    out = _out_proj(m, wo, x2, tm=1024, tn=1024)
    return out.reshape(nb, seq, D_MODEL)
```

```python
import functools
import math

import jax
import jax.numpy as jnp
from jax import lax
from jax.experimental import pallas as pl
from jax.experimental.pallas import tpu as pltpu

D_MODEL = 2048
N_META = 16
EPS = 1e-6

MLA_HEADS = 16
MLA_Q_RANK = 768
MLA_KV_RANK = 512
MLA_NOPE = 128
MLA_ROPE = 64
MLA_V = 128
MLA_QK = MLA_NOPE + MLA_ROPE
ROPE_THETA = 10000.0

DIFF_HEADS = 8
DIFF_QK = 128
DIFF_V = 2 * DIFF_QK
LAM_INIT = 0.8 - 0.6 * math.exp(-0.3 * 0)

REL_BUCKETS = 32
REL_MAX_DIST = 128

LANES = 128
MXU_DIM = 256
VMEM_LIMIT_BYTES = 56 * 1024 * 1024

HEAD_PAD = MXU_DIM
SMALL_W = MLA_Q_RANK + MLA_KV_RANK + LANES
META_PAD = LANES
MASK_VALUE = -1e30
TQ = 256
KC = 256
LOG2E = math.log2(math.e)

SEG_ZA, SEG_QD, SEG_KD, SEG_VD, SEG_ZB, SEG_GA, SEG_GB = range(7)
N_SEG = 7


def _dot(a, b):
    return jnp.dot(a, b, preferred_element_type=jnp.float32)


def _dot_nt(a, b):
    return lax.dot_general(a, b, (((1,), (1,)), ((), ())), preferred_element_type=jnp.float32)


def _rms_scale(x, width):
    return lax.rsqrt(jnp.sum(x * x, axis=-1, keepdims=True) * (1.0 / width) + EPS)


def _rot_half64(x):
    return pltpu.roll(x, 32, 1) + pltpu.roll(x, 96, 1)


def _mla_proj_kernel(x_ref, gin_ref, wsm_ref, gqa_ref, gkva_ref, wuq_ref, wukv_ref,
                     gq_ref, gk_ref, cos_ref, sin_ref, q_ref, k_ref, v_ref, *, q_scale):
    x = x_ref[...]
    u = (x * _rms_scale(x, D_MODEL) * gin_ref[...]).astype(jnp.bfloat16)
    p = _dot(u, wsm_ref[...])
    cq = p[:, :MLA_Q_RANK]
    ckv = p[:, MLA_Q_RANK:MLA_Q_RANK + MLA_KV_RANK]
    kr = p[:, MLA_Q_RANK + MLA_KV_RANK:]
    cqn = (cq * _rms_scale(cq, MLA_Q_RANK) * gqa_ref[...]).astype(jnp.bfloat16)
    ckvn = (ckv * _rms_scale(ckv, MLA_KV_RANK) * gkva_ref[...]).astype(jnp.bfloat16)
    qf = _dot(cqn, wuq_ref[...])
    kvf = _dot(ckvn, wukv_ref[...])

    cos = cos_ref[...]
    sin = sin_ref[...]
    gq = gq_ref[...]
    gk = gk_ref[...]
    gq_nope, gq_rope = gq[:, :LANES], gq[:, LANES:]
    gk_nope, gk_rope = gk[:, :LANES], gk[:, LANES:]

    ss_kr = jnp.sum(kr * kr, axis=-1, keepdims=True)
    krg = kr * gk_rope
    kr_roped = krg * cos + _rot_half64(krg) * sin

    for h in range(MLA_HEADS):
        qh = qf[:, h * HEAD_PAD:(h + 1) * HEAD_PAD]
        rq = _rms_scale(qh, MLA_QK)
        q_nope = qh[:, :LANES] * rq * gq_nope
        q_r = qh[:, LANES:] * rq * gq_rope
        q_r = q_r * cos + _rot_half64(q_r) * sin
        q_ref[0, h, :, :LANES] = (q_nope * q_scale).astype(jnp.bfloat16)
        q_ref[0, h, :, LANES:] = (q_r * q_scale).astype(jnp.bfloat16)

        k_nope = kvf[:, h * HEAD_PAD:h * HEAD_PAD + MLA_NOPE]
        rk = lax.rsqrt((jnp.sum(k_nope * k_nope, axis=-1, keepdims=True) + ss_kr) * (1.0 / MLA_QK) + EPS)
        k_ref[0, h, :, :LANES] = (k_nope * rk * gk_nope).astype(jnp.bfloat16)
        k_ref[0, h, :, LANES:] = (kr_roped * rk).astype(jnp.bfloat16)
        v_ref[0, h] = kvf[:, h * HEAD_PAD + MLA_NOPE:(h + 1) * HEAD_PAD].astype(jnp.bfloat16)


def _mla_proj(x3, gin, wsm, gqa, gkva, wuq, wukv, gq, gk, cos, sin, tm):
    nb, rows, _ = x3.shape
    const = lambda b, i: (0, 0)
    kern = functools.partial(_mla_proj_kernel, q_scale=LOG2E * MLA_QK ** -0.5)
    return pl.pallas_call(
        kern,
        grid=(nb, rows // tm),
        in_specs=[
            pl.BlockSpec((None, tm, D_MODEL), lambda b, i: (b, i, 0)),
            pl.BlockSpec((1, D_MODEL), const),
            pl.BlockSpec((D_MODEL, SMALL_W), const),
            pl.BlockSpec((1, MLA_Q_RANK), const),
            pl.BlockSpec((1, MLA_KV_RANK), const),
            pl.BlockSpec((MLA_Q_RANK, MLA_HEADS * HEAD_PAD), const),
            pl.BlockSpec((MLA_KV_RANK, MLA_HEADS * HEAD_PAD), const),
            pl.BlockSpec((1, HEAD_PAD), const),
            pl.BlockSpec((1, HEAD_PAD), const),
            pl.BlockSpec((tm, LANES), lambda b, i: (i, 0)),
            pl.BlockSpec((tm, LANES), lambda b, i: (i, 0)),
        ],
        out_specs=[
            pl.BlockSpec((1, MLA_HEADS, tm, HEAD_PAD), lambda b, i: (b, 0, i, 0)),
            pl.BlockSpec((1, MLA_HEADS, tm, HEAD_PAD), lambda b, i: (b, 0, i, 0)),
            pl.BlockSpec((1, MLA_HEADS, tm, MLA_V), lambda b, i: (b, 0, i, 0)),
        ],
        out_shape=[
            jax.ShapeDtypeStruct((nb, MLA_HEADS, rows, HEAD_PAD), jnp.bfloat16),
            jax.ShapeDtypeStruct((nb, MLA_HEADS, rows, HEAD_PAD), jnp.bfloat16),
            jax.ShapeDtypeStruct((nb, MLA_HEADS, rows, MLA_V), jnp.bfloat16),
        ],
        compiler_params=pltpu.CompilerParams(
            dimension_semantics=("arbitrary", "arbitrary"), vmem_limit_bytes=VMEM_LIMIT_BYTES),
        name="mla_proj",
    )(x3, gin, wsm, gqa, gkva, wuq, wukv, gq, gk, cos, sin)


def _group_norm_store(acc, gain, scale, o_ref):
    for c in range(acc.shape[1] // DIFF_QK):
        g = acc[:, c * DIFF_QK:(c + 1) * DIFF_QK]
        y = g * _rms_scale(g, DIFF_QK) * gain
        if scale != 1.0:
            y = y * scale
        o_ref[:, c * DIFF_QK:(c + 1) * DIFF_QK] = y.astype(o_ref.dtype)


def _in_proj_kernel(x_ref, gin_ref, w_ref, gqd_ref, gkd_ref, o_ref, u_ref, *, seg0, tiles_per_seg, qd_scale):
    j = pl.program_id(1)

    @pl.when(j == 0)
    def _():
        x = x_ref[...]
        u_ref[...] = (x * _rms_scale(x, D_MODEL) * gin_ref[...]).astype(u_ref.dtype)

    acc = _dot(u_ref[...], w_ref[...])
    seg = seg0 + j // tiles_per_seg

    @pl.when((seg == SEG_ZA) | (seg == SEG_ZB))
    def _():
        o_ref[...] = (acc / (1.0 + jnp.exp(-acc))).astype(o_ref.dtype)

    @pl.when((seg == SEG_GA) | (seg == SEG_GB))
    def _():
        o_ref[...] = (1.0 / (1.0 + jnp.exp(-acc))).astype(o_ref.dtype)

    @pl.when(seg == SEG_VD)
    def _():
        o_ref[...] = acc.astype(o_ref.dtype)

    @pl.when(seg == SEG_QD)
    def _():
        _group_norm_store(acc, gqd_ref[...], qd_scale, o_ref)

    @pl.when(seg == SEG_KD)
    def _():
        _group_norm_store(acc, gkd_ref[...], 1.0, o_ref)


def _in_proj(x2, gin, wbig, gqd, gkd, tm, tn, seg0, nseg):
    rows = x2.shape[0]
    tiles_per_seg = D_MODEL // tn
    j0 = seg0 * tiles_per_seg
    kern = functools.partial(_in_proj_kernel, seg0=seg0, tiles_per_seg=tiles_per_seg,
                             qd_scale=LOG2E * DIFF_QK ** -0.5)
    return pl.pallas_call(
        kern,
        grid=(rows // tm, nseg * tiles_per_seg),
        in_specs=[
            pl.BlockSpec((tm, D_MODEL), lambda i, j: (i, 0)),
            pl.BlockSpec((1, D_MODEL), lambda i, j: (0, 0)),
            pl.BlockSpec((D_MODEL, tn), lambda i, j: (0, j0 + j)),
            pl.BlockSpec((1, DIFF_QK), lambda i, j: (0, 0)),
            pl.BlockSpec((1, DIFF_QK), lambda i, j: (0, 0)),
        ],
        out_specs=pl.BlockSpec((tm, tn), lambda i, j: (i, j)),
        out_shape=jax.ShapeDtypeStruct((rows, nseg * D_MODEL), jnp.bfloat16),
        scratch_shapes=[pltpu.VMEM((tm, D_MODEL), jnp.bfloat16)],
        compiler_params=pltpu.CompilerParams(
            dimension_semantics=("arbitrary", "arbitrary"), vmem_limit_bytes=VMEM_LIMIT_BYTES),
        name="in_proj",
    )(x2, gin, wbig, gqd, gkd)


def _two_stage_tile_loop(n_tiles, scores, finish):
    scores(0, 0)
    for t in range(n_tiles):
        if t + 1 < n_tiles:
            scores(t + 1, (t + 1) % 2)
        finish(t, t % 2)


def _mla_attn_kernel(q_ref, k_ref, v_ref, km_ref, vm_ref, mask_ref, za_ref, o_ref, s_ref, sm_ref):
    def scores(t, slot):
        qt = q_ref[0, 0, t * TQ:(t + 1) * TQ, :]
        s_ref[slot] = _dot_nt(qt, k_ref[0, 0])
        sm_ref[slot] = _dot_nt(qt, km_ref[0]) + mask_ref[...]

    def finish(t, slot):
        r0 = t * TQ
        s = s_ref[slot]
        sm = sm_ref[slot]
        m = jnp.maximum(jnp.max(s, axis=-1, keepdims=True), jnp.max(sm, axis=-1, keepdims=True))
        p = jnp.exp2(s - m)
        pm = jnp.exp2(sm - m)
        l = jnp.sum(p, axis=-1, keepdims=True) + jnp.sum(pm, axis=-1, keepdims=True)
        o = _dot(p.astype(jnp.bfloat16), v_ref[0, 0]) + _dot(pm.astype(jnp.bfloat16), vm_ref[0])
        o = o * (1.0 / l) * za_ref[pl.ds(r0, TQ), :].astype(jnp.float32)
        o_ref[pl.ds(r0, TQ), :] = o.astype(o_ref.dtype)

    _two_stage_tile_loop(q_ref.shape[2] // TQ, scores, finish)


def _mla_attn(q, k, v, km, vm, mask, proj, seq):
    nb = q.shape[0]
    return pl.pallas_call(
        _mla_attn_kernel,
        grid=(nb, MLA_HEADS),
        in_specs=[
            pl.BlockSpec((1, 1, seq, HEAD_PAD), lambda b, h: (b, h, 0, 0)),
            pl.BlockSpec((1, 1, seq, HEAD_PAD), lambda b, h: (b, h, 0, 0)),
            pl.BlockSpec((1, 1, seq, MLA_V), lambda b, h: (b, h, 0, 0)),
            pl.BlockSpec((1, META_PAD, HEAD_PAD), lambda b, h: (h, 0, 0)),
            pl.BlockSpec((1, META_PAD, MLA_V), lambda b, h: (h, 0, 0)),
            pl.BlockSpec((1, META_PAD), lambda b, h: (0, 0)),
            pl.BlockSpec((seq, MLA_V), lambda b, h: (b, SEG_ZA * (D_MODEL // MLA_V) + h)),
        ],
        out_specs=pl.BlockSpec((seq, MLA_V), lambda b, h: (b, h)),
        out_shape=jax.ShapeDtypeStruct((nb * seq, MLA_HEADS * MLA_V), jnp.bfloat16),
        scratch_shapes=[pltpu.VMEM((2, TQ, seq), jnp.float32), pltpu.VMEM((2, TQ, META_PAD), jnp.float32)],
        compiler_params=pltpu.CompilerParams(
            dimension_semantics=("arbitrary", "arbitrary"), vmem_limit_bytes=VMEM_LIMIT_BYTES),
        name="mla_attn",
    )(q, k, v, km, vm, mask, proj)


def _diff_attn_kernel(qd_ref, kd_ref, vd_ref, kdm_ref, vdm_ref, btab_ref, mtab_ref, lam_ref,
                      subln_ref, zb_ref, o_ref, s_ref, sm_ref):
    n_chunks = kd_ref.shape[0] // KC
    lv = lam_ref[...]
    lam = (jnp.exp(jnp.sum(lv[0:1] * lv[1:2], axis=-1, keepdims=True))
           - jnp.exp(jnp.sum(lv[2:3] * lv[3:4], axis=-1, keepdims=True)) + LAM_INIT)

    def scores(t, slot):
        for half in range(2):
            lo = half * DIFF_QK
            qm = qd_ref[t * TQ:(t + 1) * TQ, lo:lo + DIFF_QK]
            for c in range(n_chunks):
                kc = kd_ref[c * KC:(c + 1) * KC, lo:lo + DIFF_QK]
                s_ref[slot, half, :, c * KC:(c + 1) * KC] = (
                    _dot_nt(qm, kc) + btab_ref[0, min(max(c - t, -2), 2) + 2])
            sm_ref[slot, half] = _dot_nt(qm, kdm_ref[:, lo:lo + DIFF_QK]) + mtab_ref[0, min(t, 1)]

    def softmax_parts(slot, half):
        s = s_ref[slot, half]
        sm = sm_ref[slot, half]
        m = jnp.maximum(jnp.max(s, axis=-1, keepdims=True), jnp.max(sm, axis=-1, keepdims=True))
        p = jnp.exp2(s - m)
        pm = jnp.exp2(sm - m)
        l = jnp.sum(p, axis=-1, keepdims=True) + jnp.sum(pm, axis=-1, keepdims=True)
        return p, pm, 1.0 / l

    def finish(t, slot):
        r0 = t * TQ
        p1, pm1, inv1 = softmax_parts(slot, 0)
        p2, pm2, inv2 = softmax_parts(slot, 1)
        w2 = lam * inv2
        a = (p1 * inv1 - p2 * w2).astype(jnp.bfloat16)
        am = (pm1 * inv1 - pm2 * w2).astype(jnp.bfloat16)
        o = _dot(a, vd_ref[...]) + _dot(am, vdm_ref[...])
        y = o * _rms_scale(o, DIFF_V) * subln_ref[...] * (1.0 - LAM_INIT)
        y = y * zb_ref[pl.ds(r0, TQ), :].astype(jnp.float32)
        o_ref[pl.ds(r0, TQ), :] = y.astype(o_ref.dtype)

    _two_stage_tile_loop(qd_ref.shape[0] // TQ, scores, finish)


def _diff_attn(proj, proj_meta, btab, mtab, lam, subln, nb, seq):
    cps = D_MODEL // DIFF_V
    return pl.pallas_call(
        _diff_attn_kernel,
        grid=(nb, DIFF_HEADS),
        in_specs=[
            pl.BlockSpec((seq, DIFF_V), lambda b, h: (b, SEG_QD * cps + h)),
            pl.BlockSpec((seq, DIFF_V), lambda b, h: (b, SEG_KD * cps + h)),
            pl.BlockSpec((seq, DIFF_V), lambda b, h: (b, SEG_VD * cps + h)),
            pl.BlockSpec((META_PAD, DIFF_V), lambda b, h: (0, h)),
            pl.BlockSpec((META_PAD, DIFF_V), lambda b, h: (0, cps + h)),
            pl.BlockSpec((1, 5, TQ, KC), lambda b, h: (h, 0, 0, 0)),
            pl.BlockSpec((1, 2, TQ, META_PAD), lambda b, h: (h, 0, 0, 0)),
            pl.BlockSpec((4, DIFF_QK), lambda b, h: (0, 0)),
            pl.BlockSpec((1, DIFF_V), lambda b, h: (0, 0)),
            pl.BlockSpec((seq, DIFF_V), lambda b, h: (b, SEG_ZB * cps + h)),
        ],
        out_specs=pl.BlockSpec((seq, DIFF_V), lambda b, h: (b, h)),
        out_shape=jax.ShapeDtypeStruct((nb * seq, DIFF_HEADS * DIFF_V), jnp.bfloat16),
        scratch_shapes=[pltpu.VMEM((2, 2, TQ, seq), jnp.float32), pltpu.VMEM((2, 2, TQ, META_PAD), jnp.float32)],
        compiler_params=pltpu.CompilerParams(
            dimension_semantics=("arbitrary", "arbitrary"), vmem_limit_bytes=VMEM_LIMIT_BYTES),
        name="diff_attn",
    )(proj, proj, proj, proj_meta, proj_meta, btab, mtab, lam, subln, proj)


def _merge_kernel(oa_ref, ob_ref, wa_ref, wb_ref, ga_ref, gb_ref, m_ref):
    ya = _dot(oa_ref[...], wa_ref[...])
    yb = _dot(ob_ref[...], wb_ref[...])
    m = ga_ref[...].astype(jnp.float32) * ya + gb_ref[...].astype(jnp.float32) * yb
    m_ref[...] = m.astype(m_ref.dtype)


def _merge(oa, ob, wa, wb, proj, tm, tn):
    rows = oa.shape[0]
    tps = D_MODEL // tn
    return pl.pallas_call(
        _merge_kernel,
        grid=(rows // tm, D_MODEL // tn),
        in_specs=[
            pl.BlockSpec((tm, D_MODEL), lambda i, j: (i, 0)),
            pl.BlockSpec((tm, D_MODEL), lambda i, j: (i, 0)),
            pl.BlockSpec((D_MODEL, tn), lambda i, j: (0, j)),
            pl.BlockSpec((D_MODEL, tn), lambda i, j: (0, j)),
            pl.BlockSpec((tm, tn), lambda i, j: (i, SEG_GA * tps + j)),
            pl.BlockSpec((tm, tn), lambda i, j: (i, SEG_GB * tps + j)),
        ],
        out_specs=pl.BlockSpec((tm, tn), lambda i, j: (i, j)),
        out_shape=jax.ShapeDtypeStruct((rows, D_MODEL), jnp.bfloat16),
        compiler_params=pltpu.CompilerParams(
            dimension_semantics=("arbitrary", "arbitrary"), vmem_limit_bytes=VMEM_LIMIT_BYTES),
        name="merge",
    )(oa, ob, wa, wb, proj, proj)


def _out_kernel(m_ref, w_ref, x_ref, o_ref):
    o_ref[...] = x_ref[...] + _dot(m_ref[...], w_ref[...])


def _out_proj(m, wout, x2, tm, tn):
    rows = m.shape[0]
    return pl.pallas_call(
        _out_kernel,
        grid=(rows // tm, D_MODEL // tn),
        in_specs=[
            pl.BlockSpec((tm, D_MODEL), lambda i, j: (i, 0)),
            pl.BlockSpec((D_MODEL, tn), lambda i, j: (0, j)),
            pl.BlockSpec((tm, tn), lambda i, j: (i, j)),
        ],
        out_specs=pl.BlockSpec((tm, tn), lambda i, j: (i, j)),
        out_shape=jax.ShapeDtypeStruct((rows, D_MODEL), jnp.float32),
        compiler_params=pltpu.CompilerParams(
            dimension_semantics=("arbitrary", "arbitrary"), vmem_limit_bytes=VMEM_LIMIT_BYTES),
        name="out_proj",
    )(m, wout, x2)


def _t5_bucket(rel):
    nb = REL_BUCKETS // 2
    max_exact = nb // 2
    ret = jnp.where(rel > 0, nb, 0)
    n = jnp.abs(rel)
    nf = jnp.maximum(n, 1).astype(jnp.float32)
    large = max_exact + (jnp.log(nf / max_exact) / math.log(REL_MAX_DIST / max_exact)
                         * (nb - max_exact)).astype(jnp.int32)
    large = jnp.minimum(large, nb - 1)
    return ret + jnp.where(n < max_exact, n, large)


def _bias_tab_kernel(rb_ref, bk_ref, mbk_ref, btab_ref, mtab_ref):
    h = pl.program_id(0)
    bk = bk_ref[...]
    mbk = mbk_ref[...]
    acc = jnp.zeros(bk.shape, jnp.float32)
    macc = jnp.where(mbk < 0, MASK_VALUE, 0.0).astype(jnp.float32)
    for b in range(REL_BUCKETS):
        val = rb_ref[b, h] * LOG2E
        acc = jnp.where(bk == b, val, acc)
        macc = jnp.where(mbk == b, val, macc)
    btab_ref[0] = acc
    mtab_ref[0] = macc


def _bias_tables(rel_bias):
    qq = jnp.arange(TQ, dtype=jnp.int32)[:, None]
    kk = jnp.arange(KC, dtype=jnp.int32)[None, :]
    dd = jnp.arange(-2, 3, dtype=jnp.int32)[:, None, None]
    bk = _t5_bucket(dd * KC + kk[None] - qq[None])
    jm = jnp.arange(META_PAD, dtype=jnp.int32)[None, None, :]
    qpos = N_META + jnp.arange(2, dtype=jnp.int32)[:, None, None] * TQ + qq[None]
    mbk = jnp.where(jm < N_META, _t5_bucket(jm - qpos), -1)
    return pl.pallas_call(
        _bias_tab_kernel,
        grid=(DIFF_HEADS,),
        in_specs=[
            pl.BlockSpec(memory_space=pltpu.SMEM),
            pl.BlockSpec((5, TQ, KC), lambda h: (0, 0, 0)),
            pl.BlockSpec((2, TQ, META_PAD), lambda h: (0, 0, 0)),
        ],
        out_specs=[
            pl.BlockSpec((1, 5, TQ, KC), lambda h: (h, 0, 0, 0)),
            pl.BlockSpec((1, 2, TQ, META_PAD), lambda h: (h, 0, 0, 0)),
        ],
        out_shape=[
            jax.ShapeDtypeStruct((DIFF_HEADS, 5, TQ, KC), jnp.float32),
            jax.ShapeDtypeStruct((DIFF_HEADS, 2, TQ, META_PAD), jnp.float32),
        ],
        compiler_params=pltpu.CompilerParams(dimension_semantics=("arbitrary",)),
        name="bias_tables",
    )(rel_bias.astype(jnp.float32), bk, mbk)


def _rope_tables(n_pos):
    half = MLA_ROPE // 2
    inv = ROPE_THETA ** (-jnp.arange(half, dtype=jnp.float32) / half)
    ang = jnp.arange(n_pos, dtype=jnp.int32).astype(jnp.float32)[:, None] * inv[None, :]
    c, s = jnp.cos(ang), jnp.sin(ang)
    z = jnp.zeros((n_pos, LANES - MLA_ROPE), jnp.float32)
    cos_t = jnp.concatenate([c, c, z], axis=-1)
    sin_t = jnp.concatenate([-s, s, z], axis=-1)
    return cos_t, sin_t


def _pad_rows(a, rows):
    return jnp.pad(a, [(0, rows - a.shape[0])] + [(0, 0)] * (a.ndim - 1))


def kernel(x, meta_tokens, rel_bias, norm_in, w_in, q_a_norm, kv_a_norm, w_uq, w_ukv, mla_q_norm, mla_k_norm,
           diff_q_norm, diff_k_norm, diff_lambda, diff_subln, w_branch_a, w_branch_b, w_out):
    nb, seq, _ = x.shape
    bf = jnp.bfloat16
    n_small = MLA_Q_RANK + MLA_KV_RANK + MLA_ROPE

    w_in0 = w_in[0]
    wsm = jnp.pad(w_in0[:, :n_small], ((0, 0), (0, SMALL_W - n_small))).astype(bf)
    wbig = w_in0[:, n_small:].astype(bf)
    wuq = jnp.pad(w_uq[0].reshape(MLA_Q_RANK, MLA_HEADS, MLA_QK),
                  ((0, 0), (0, 0), (0, HEAD_PAD - MLA_QK))).reshape(MLA_Q_RANK, MLA_HEADS * HEAD_PAD).astype(bf)
    wukv = w_ukv[0].astype(bf)
    wa, wb, wo = w_branch_a[0].astype(bf), w_branch_b[0].astype(bf), w_out[0].astype(bf)

    gin = norm_in[0][None]
    gqa, gkva = q_a_norm[0][None], kv_a_norm[0][None]
    gq = jnp.pad(mla_q_norm[0], (0, HEAD_PAD - MLA_QK))[None]
    gk = jnp.pad(mla_k_norm[0], (0, HEAD_PAD - MLA_QK))[None]
    gqd, gkd = diff_q_norm[0][None], diff_k_norm[0][None]
    subln = diff_subln[0][None]
    lam = diff_lambda[0]

    cos_t, sin_t = _rope_tables(N_META + seq)
    btab, mtab = _bias_tables(rel_bias)
    mla_mask = jnp.where(jnp.arange(META_PAD) < N_META, 0.0, MASK_VALUE).astype(jnp.float32)[None]

    x2 = x.reshape(nb * seq, D_MODEL)
    meta = meta_tokens.astype(x.dtype)

    q, k, v = _mla_proj(x, gin, wsm, gqa, gkva, wuq, wukv, gq, gk, cos_t[N_META:], sin_t[N_META:], tm=512)
    proj = _in_proj(x2, gin, wbig, gqd, gkd, tm=1024, tn=1024, seg0=0, nseg=N_SEG)

    _, km, vm = _mla_proj(meta[None], gin, wsm, gqa, gkva, wuq, wukv, gq, gk,
                          cos_t[:N_META], sin_t[:N_META], tm=N_META)
    km = jnp.pad(km[0], ((0, 0), (0, META_PAD - N_META), (0, 0)))
    vm = jnp.pad(vm[0], ((0, 0), (0, META_PAD - N_META), (0, 0)))
    proj_meta = _in_proj(meta, gin, wbig, gqd, gkd, tm=N_META, tn=1024, seg0=SEG_KD, nseg=2)
    proj_meta = _pad_rows(proj_meta, META_PAD)

    oa = _mla_attn(q, k, v, km, vm, mla_mask, proj, seq)
    ob = _diff_attn(proj, proj_meta, btab, mtab, lam, subln, nb, seq)
    m = _merge(oa, ob, wa, wb, proj, tm=1024, tn=1024)
    out = _out_proj(m, wo, x2, tm=1024, tn=1024)
    return out.reshape(nb, seq, D_MODEL)
```

```python
import functools
import math

import jax
import jax.numpy as jnp
from jax import lax
from jax.experimental import pallas as pl
from jax.experimental.pallas import tpu as pltpu

D_MODEL = 2048
N_META = 16
EPS = 1e-6

MLA_HEADS = 16
MLA_Q_RANK = 768
MLA_KV_RANK = 512
MLA_NOPE = 128
MLA_ROPE = 64
MLA_V = 128
MLA_QK = MLA_NOPE + MLA_ROPE
ROPE_THETA = 10000.0

DIFF_HEADS = 8
DIFF_QK = 128
DIFF_V = 2 * DIFF_QK
LAM_INIT = 0.8 - 0.6 * math.exp(-0.3 * 0)

REL_BUCKETS = 32
REL_MAX_DIST = 128

LANES = 128
MXU_DIM = 256
VMEM_LIMIT_BYTES = 56 * 1024 * 1024

HEAD_PAD = MXU_DIM
SMALL_W = MLA_Q_RANK + MLA_KV_RANK + LANES
META_PAD = LANES
MASK_VALUE = -1e30
TQ = 256
KC = 256
LOG2E = math.log2(math.e)

SEG_ZA, SEG_QD, SEG_KD, SEG_VD, SEG_ZB, SEG_GA, SEG_GB = range(7)


def _dot(a, b):
    return jnp.dot(a, b, preferred_element_type=jnp.float32)


def _dot_nt(a, b):
    return lax.dot_general(a, b, (((1,), (1,)), ((), ())), preferred_element_type=jnp.float32)


def _rms_scale(x, width):
    return lax.rsqrt(jnp.sum(x * x, axis=-1, keepdims=True) * (1.0 / width) + EPS)


def _rot_half64(x):
    return pltpu.roll(x, 32, 1) + pltpu.roll(x, 96, 1)


def _mla_proj_kernel(x_ref, gin_ref, wsm_ref, gqa_ref, gkva_ref, wuq_ref, wukv_ref,
                     gq_ref, gk_ref, cos_ref, sin_ref, q_ref, k_ref, v_ref, u_ref, *, q_scale):
    x = x_ref[...]
    u = (x * _rms_scale(x, D_MODEL) * gin_ref[...]).astype(jnp.bfloat16)
    u_ref[...] = u
    p = _dot(u, wsm_ref[...])
    cq = p[:, :MLA_Q_RANK]
    ckv = p[:, MLA_Q_RANK:MLA_Q_RANK + MLA_KV_RANK]
    kr = p[:, MLA_Q_RANK + MLA_KV_RANK:]
    cqn = (cq * _rms_scale(cq, MLA_Q_RANK) * gqa_ref[...]).astype(jnp.bfloat16)
    ckvn = (ckv * _rms_scale(ckv, MLA_KV_RANK) * gkva_ref[...]).astype(jnp.bfloat16)
    qf = _dot(cqn, wuq_ref[...])
    kvf = _dot(ckvn, wukv_ref[...])

    cos = cos_ref[...]
    sin = sin_ref[...]
    gq = gq_ref[...]
    gk = gk_ref[...]
    gq_nope, gq_rope = gq[:, :LANES], gq[:, LANES:]
    gk_nope, gk_rope = gk[:, :LANES], gk[:, LANES:]

    ss_kr = jnp.sum(kr * kr, axis=-1, keepdims=True)
    krg = kr * gk_rope
    kr_roped = krg * cos + _rot_half64(krg) * sin

    for h in range(MLA_HEADS):
        qh = qf[:, h * HEAD_PAD:(h + 1) * HEAD_PAD]
        rq = _rms_scale(qh, MLA_QK)
        q_nope = qh[:, :LANES] * rq * gq_nope
        q_r = qh[:, LANES:] * rq * gq_rope
        q_r = q_r * cos + _rot_half64(q_r) * sin
        q_ref[0, h, :, :LANES] = (q_nope * q_scale).astype(jnp.bfloat16)
        q_ref[0, h, :, LANES:] = (q_r * q_scale).astype(jnp.bfloat16)

        k_nope = kvf[:, h * HEAD_PAD:h * HEAD_PAD + MLA_NOPE]
        rk = lax.rsqrt((jnp.sum(k_nope * k_nope, axis=-1, keepdims=True) + ss_kr) * (1.0 / MLA_QK) + EPS)
        k_ref[0, h, :, :LANES] = (k_nope * rk * gk_nope).astype(jnp.bfloat16)
        k_ref[0, h, :, LANES:] = (kr_roped * rk).astype(jnp.bfloat16)
        v_ref[0, h] = kvf[:, h * HEAD_PAD + MLA_NOPE:(h + 1) * HEAD_PAD].astype(jnp.bfloat16)


def _mla_proj(x3, gin, wsm, gqa, gkva, wuq, wukv, gq, gk, cos, sin, tm):
    nb, rows, _ = x3.shape
    const = lambda b, i: (0, 0)
    kern = functools.partial(_mla_proj_kernel, q_scale=LOG2E * MLA_QK ** -0.5)
    return pl.pallas_call(
        kern,
        grid=(nb, rows // tm),
        in_specs=[
            pl.BlockSpec((None, tm, D_MODEL), lambda b, i: (b, i, 0)),
            pl.BlockSpec((1, D_MODEL), const),
            pl.BlockSpec((D_MODEL, SMALL_W), const),
            pl.BlockSpec((1, MLA_Q_RANK), const),
            pl.BlockSpec((1, MLA_KV_RANK), const),
            pl.BlockSpec((MLA_Q_RANK, MLA_HEADS * HEAD_PAD), const),
            pl.BlockSpec((MLA_KV_RANK, MLA_HEADS * HEAD_PAD), const),
            pl.BlockSpec((1, HEAD_PAD), const),
            pl.BlockSpec((1, HEAD_PAD), const),
            pl.BlockSpec((tm, LANES), lambda b, i: (i, 0)),
            pl.BlockSpec((tm, LANES), lambda b, i: (i, 0)),
        ],
        out_specs=[
            pl.BlockSpec((1, MLA_HEADS, tm, HEAD_PAD), lambda b, i: (b, 0, i, 0)),
            pl.BlockSpec((1, MLA_HEADS, tm, HEAD_PAD), lambda b, i: (b, 0, i, 0)),
            pl.BlockSpec((1, MLA_HEADS, tm, MLA_V), lambda b, i: (b, 0, i, 0)),
            pl.BlockSpec((None, tm, D_MODEL), lambda b, i: (b, i, 0)),
        ],
        out_shape=[
            jax.ShapeDtypeStruct((nb, MLA_HEADS, rows, HEAD_PAD), jnp.bfloat16),
            jax.ShapeDtypeStruct((nb, MLA_HEADS, rows, HEAD_PAD), jnp.bfloat16),
            jax.ShapeDtypeStruct((nb, MLA_HEADS, rows, MLA_V), jnp.bfloat16),
            jax.ShapeDtypeStruct((nb, rows, D_MODEL), jnp.bfloat16),
        ],
        compiler_params=pltpu.CompilerParams(
            dimension_semantics=("arbitrary", "arbitrary"), vmem_limit_bytes=VMEM_LIMIT_BYTES),
        name="mla_proj",
    )(x3, gin, wsm, gqa, gkva, wuq, wukv, gq, gk, cos, sin)


def _seg_proj_kernel(u_ref, w_ref, *rest, kind):
    o_ref = rest[-1]
    acc = _dot(u_ref[...], w_ref[...])
    if kind == "silu":
        o_ref[...] = (acc / (1.0 + jnp.exp(-acc))).astype(o_ref.dtype)
    elif kind == "sigmoid":
        o_ref[...] = (1.0 / (1.0 + jnp.exp(-acc))).astype(o_ref.dtype)
    elif kind == "copy":
        o_ref[...] = acc.astype(o_ref.dtype)
    else:
        gain = rest[0][...]
        for c in range(acc.shape[1] // DIFF_QK):
            cols = slice(c * DIFF_QK, (c + 1) * DIFF_QK)
            g = acc[:, cols]
            o_ref[:, cols] = (g * _rms_scale(g, DIFF_QK) * gain[:, cols]).astype(o_ref.dtype)


def _seg_proj(u2, wbig, segs, kind, tm, tn, gain=None):
    rows = u2.shape[0]
    tps = D_MODEL // tn
    if len(segs) == 1:
        w_map = lambda i, j: (0, segs[0] * tps + j)
    else:
        w_map = lambda i, j: (0, jnp.where(j < tps, segs[0] * tps + j, segs[1] * tps + j - tps))
    in_specs = [pl.BlockSpec((tm, D_MODEL), lambda i, j: (i, 0)), pl.BlockSpec((D_MODEL, tn), w_map)]
    args = [u2, wbig]
    if gain is not None:
        in_specs.append(pl.BlockSpec((1, tn), lambda i, j: (0, j)))
        args.append(gain)
    return pl.pallas_call(
        functools.partial(_seg_proj_kernel, kind=kind),
        grid=(rows // tm, len(segs) * tps),
        in_specs=in_specs,
        out_specs=pl.BlockSpec((tm, tn), lambda i, j: (i, j)),
        out_shape=jax.ShapeDtypeStruct((rows, len(segs) * D_MODEL), jnp.bfloat16),
        compiler_params=pltpu.CompilerParams(
            dimension_semantics=("arbitrary", "arbitrary"), vmem_limit_bytes=VMEM_LIMIT_BYTES),
        name="in_proj_" + kind,
    )(*args)


def _two_stage_tile_loop(n_tiles, scores, finish):
    scores(0, 0)
    for t in range(n_tiles):
        if t + 1 < n_tiles:
            scores(t + 1, (t + 1) % 2)
        finish(t, t % 2)


def _mla_attn_kernel(q_ref, k_ref, v_ref, km_ref, vm_ref, mask_ref, za_ref, o_ref, s_ref, sm_ref):
    def scores(t, slot):
        qt = q_ref[0, 0, t * TQ:(t + 1) * TQ, :]
        s_ref[slot] = _dot_nt(qt, k_ref[0, 0])
        sm_ref[slot] = _dot_nt(qt, km_ref[0]) + mask_ref[...]

    def finish(t, slot):
        r0 = t * TQ
        s = s_ref[slot]
        sm = sm_ref[slot]
        m = jnp.maximum(jnp.max(s, axis=-1, keepdims=True), jnp.max(sm, axis=-1, keepdims=True))
        p = jnp.exp2(s - m)
        pm = jnp.exp2(sm - m)
        l = jnp.sum(p, axis=-1, keepdims=True) + jnp.sum(pm, axis=-1, keepdims=True)
        o = _dot(p.astype(jnp.bfloat16), v_ref[0, 0]) + _dot(pm.astype(jnp.bfloat16), vm_ref[0])
        o = o * (1.0 / l) * za_ref[pl.ds(r0, TQ), :].astype(jnp.float32)
        o_ref[pl.ds(r0, TQ), :] = o.astype(o_ref.dtype)

    _two_stage_tile_loop(q_ref.shape[2] // TQ, scores, finish)


def _mla_attn(q, k, v, km, vm, mask, zs, seq):
    nb = q.shape[0]
    return pl.pallas_call(
        _mla_attn_kernel,
        grid=(nb, MLA_HEADS),
        in_specs=[
            pl.BlockSpec((1, 1, seq, HEAD_PAD), lambda b, h: (b, h, 0, 0)),
            pl.BlockSpec((1, 1, seq, HEAD_PAD), lambda b, h: (b, h, 0, 0)),
            pl.BlockSpec((1, 1, seq, MLA_V), lambda b, h: (b, h, 0, 0)),
            pl.BlockSpec((1, META_PAD, HEAD_PAD), lambda b, h: (h, 0, 0)),
            pl.BlockSpec((1, META_PAD, MLA_V), lambda b, h: (h, 0, 0)),
            pl.BlockSpec((1, META_PAD), lambda b, h: (0, 0)),
            pl.BlockSpec((seq, MLA_V), lambda b, h: (b, h)),
        ],
        out_specs=pl.BlockSpec((seq, MLA_V), lambda b, h: (b, h)),
        out_shape=jax.ShapeDtypeStruct((nb * seq, MLA_HEADS * MLA_V), jnp.bfloat16),
        scratch_shapes=[pltpu.VMEM((2, TQ, seq), jnp.float32), pltpu.VMEM((2, TQ, META_PAD), jnp.float32)],
        compiler_params=pltpu.CompilerParams(
            dimension_semantics=("arbitrary", "arbitrary"), vmem_limit_bytes=VMEM_LIMIT_BYTES),
        name="mla_attn",
    )(q, k, v, km, vm, mask, zs)


def _diff_attn_kernel(qd_ref, kd_ref, vd_ref, kdm_ref, vdm_ref, btab_ref, mtab_ref, lam_ref,
                      subln_ref, zb_ref, o_ref, s_ref, sm_ref):
    n_chunks = kd_ref.shape[0] // KC
    lv = lam_ref[...]
    lam = (jnp.exp(jnp.sum(lv[0:1] * lv[1:2], axis=-1, keepdims=True))
           - jnp.exp(jnp.sum(lv[2:3] * lv[3:4], axis=-1, keepdims=True)) + LAM_INIT)

    def scores(t, slot):
        for half in range(2):
            lo = half * DIFF_QK
            qm = qd_ref[t * TQ:(t + 1) * TQ, lo:lo + DIFF_QK]
            for c in range(n_chunks):
                kc = kd_ref[c * KC:(c + 1) * KC, lo:lo + DIFF_QK]
                s_ref[slot, half, :, c * KC:(c + 1) * KC] = (
                    _dot_nt(qm, kc) + btab_ref[0, min(max(c - t, -2), 2) + 2])
            sm_ref[slot, half] = _dot_nt(qm, kdm_ref[:, lo:lo + DIFF_QK]) + mtab_ref[0, min(t, 1)]

    def softmax_parts(slot, half):
        s = s_ref[slot, half]
        sm = sm_ref[slot, half]
        m = jnp.maximum(jnp.max(s, axis=-1, keepdims=True), jnp.max(sm, axis=-1, keepdims=True))
        p = jnp.exp2(s - m)
        pm = jnp.exp2(sm - m)
        l = jnp.sum(p, axis=-1, keepdims=True) + jnp.sum(pm, axis=-1, keepdims=True)
        return p, pm, 1.0 / l

    def finish(t, slot):
        r0 = t * TQ
        p1, pm1, inv1 = softmax_parts(slot, 0)
        p2, pm2, inv2 = softmax_parts(slot, 1)
        w2 = lam * inv2
        a = (p1 * inv1 - p2 * w2).astype(jnp.bfloat16)
        am = (pm1 * inv1 - pm2 * w2).astype(jnp.bfloat16)
        o = _dot(a, vd_ref[...]) + _dot(am, vdm_ref[...])
        y = o * _rms_scale(o, DIFF_V) * subln_ref[...] * (1.0 - LAM_INIT)
        y = y * zb_ref[pl.ds(r0, TQ), :].astype(jnp.float32)
        o_ref[pl.ds(r0, TQ), :] = y.astype(o_ref.dtype)

    _two_stage_tile_loop(qd_ref.shape[0] // TQ, scores, finish)


def _diff_attn(qk, vd, zs, qk_meta, vd_meta, btab, mtab, lam, subln, nb, seq):
    cps = D_MODEL // DIFF_V
    return pl.pallas_call(
        _diff_attn_kernel,
        grid=(nb, DIFF_HEADS),
        in_specs=[
            pl.BlockSpec((seq, DIFF_V), lambda b, h: (b, h)),
            pl.BlockSpec((seq, DIFF_V), lambda b, h: (b, cps + h)),
            pl.BlockSpec((seq, DIFF_V), lambda b, h: (b, h)),
            pl.BlockSpec((META_PAD, DIFF_V), lambda b, h: (0, h)),
            pl.BlockSpec((META_PAD, DIFF_V), lambda b, h: (0, h)),
            pl.BlockSpec((1, 5, TQ, KC), lambda b, h: (h, 0, 0, 0)),
            pl.BlockSpec((1, 2, TQ, META_PAD), lambda b, h: (h, 0, 0, 0)),
            pl.BlockSpec((4, DIFF_QK), lambda b, h: (0, 0)),
            pl.BlockSpec((1, DIFF_V), lambda b, h: (0, 0)),
            pl.BlockSpec((seq, DIFF_V), lambda b, h: (b, cps + h)),
        ],
        out_specs=pl.BlockSpec((seq, DIFF_V), lambda b, h: (b, h)),
        out_shape=jax.ShapeDtypeStruct((nb * seq, DIFF_HEADS * DIFF_V), jnp.bfloat16),
        scratch_shapes=[pltpu.VMEM((2, 2, TQ, seq), jnp.float32), pltpu.VMEM((2, 2, TQ, META_PAD), jnp.float32)],
        compiler_params=pltpu.CompilerParams(
            dimension_semantics=("arbitrary", "arbitrary"), vmem_limit_bytes=VMEM_LIMIT_BYTES),
        name="diff_attn",
    )(qk, qk, vd, qk_meta, vd_meta, btab, mtab, lam, subln, zs)


def _merge_kernel(oa_ref, ob_ref, wa_ref, wb_ref, ga_ref, gb_ref, m_ref):
    ya = _dot(oa_ref[...], wa_ref[...])
    yb = _dot(ob_ref[...], wb_ref[...])
    m = ga_ref[...].astype(jnp.float32) * ya + gb_ref[...].astype(jnp.float32) * yb
    m_ref[...] = m.astype(m_ref.dtype)


def _merge(oa, ob, wa, wb, gs, tm, tn):
    rows = oa.shape[0]
    tps = D_MODEL // tn
    return pl.pallas_call(
        _merge_kernel,
        grid=(rows // tm, D_MODEL // tn),
        in_specs=[
            pl.BlockSpec((tm, D_MODEL), lambda i, j: (i, 0)),
            pl.BlockSpec((tm, D_MODEL), lambda i, j: (i, 0)),
            pl.BlockSpec((D_MODEL, tn), lambda i, j: (0, j)),
            pl.BlockSpec((D_MODEL, tn), lambda i, j: (0, j)),
            pl.BlockSpec((tm, tn), lambda i, j: (i, j)),
            pl.BlockSpec((tm, tn), lambda i, j: (i, tps + j)),
        ],
        out_specs=pl.BlockSpec((tm, tn), lambda i, j: (i, j)),
        out_shape=jax.ShapeDtypeStruct((rows, D_MODEL), jnp.bfloat16),
        compiler_params=pltpu.CompilerParams(
            dimension_semantics=("arbitrary", "arbitrary"), vmem_limit_bytes=VMEM_LIMIT_BYTES),
        name="merge",
    )(oa, ob, wa, wb, gs, gs)


def _out_kernel(m_ref, w_ref, x_ref, o_ref):
    o_ref[...] = x_ref[...] + _dot(m_ref[...], w_ref[...])


def _out_proj(m, wout, x2, tm, tn):
    rows = m.shape[0]
    return pl.pallas_call(
        _out_kernel,
        grid=(rows // tm, D_MODEL // tn),
        in_specs=[
            pl.BlockSpec((tm, D_MODEL), lambda i, j: (i, 0)),
            pl.BlockSpec((D_MODEL, tn), lambda i, j: (0, j)),
            pl.BlockSpec((tm, tn), lambda i, j: (i, j)),
        ],
        out_specs=pl.BlockSpec((tm, tn), lambda i, j: (i, j)),
        out_shape=jax.ShapeDtypeStruct((rows, D_MODEL), jnp.float32),
        compiler_params=pltpu.CompilerParams(
            dimension_semantics=("arbitrary", "arbitrary"), vmem_limit_bytes=VMEM_LIMIT_BYTES),
        name="out_proj",
    )(m, wout, x2)


def _t5_bucket(rel):
    nb = REL_BUCKETS // 2
    max_exact = nb // 2
    ret = jnp.where(rel > 0, nb, 0)
    n = jnp.abs(rel)
    nf = jnp.maximum(n, 1).astype(jnp.float32)
    large = max_exact + (jnp.log(nf / max_exact) / math.log(REL_MAX_DIST / max_exact)
                         * (nb - max_exact)).astype(jnp.int32)
    large = jnp.minimum(large, nb - 1)
    return ret + jnp.where(n < max_exact, n, large)


def _bias_tab_kernel(rb_ref, bk_ref, mbk_ref, btab_ref, mtab_ref):
    h = pl.program_id(0)
    bk = bk_ref[...]
    mbk = mbk_ref[...]
    acc = jnp.zeros(bk.shape, jnp.float32)
    macc = jnp.where(mbk < 0, MASK_VALUE, 0.0).astype(jnp.float32)
    for b in range(REL_BUCKETS):
        val = rb_ref[b, h] * LOG2E
        acc = jnp.where(bk == b, val, acc)
        macc = jnp.where(mbk == b, val, macc)
    btab_ref[0] = acc
    mtab_ref[0] = macc


def _bias_tables(rel_bias):
    qq = jnp.arange(TQ, dtype=jnp.int32)[:, None]
    kk = jnp.arange(KC, dtype=jnp.int32)[None, :]
    dd = jnp.arange(-2, 3, dtype=jnp.int32)[:, None, None]
    bk = _t5_bucket(dd * KC + kk[None] - qq[None])
    jm = jnp.arange(META_PAD, dtype=jnp.int32)[None, None, :]
    qpos = N_META + jnp.arange(2, dtype=jnp.int32)[:, None, None] * TQ + qq[None]
    mbk = jnp.where(jm < N_META, _t5_bucket(jm - qpos), -1)
    return pl.pallas_call(
        _bias_tab_kernel,
        grid=(DIFF_HEADS,),
        in_specs=[
            pl.BlockSpec(memory_space=pltpu.SMEM),
            pl.BlockSpec((5, TQ, KC), lambda h: (0, 0, 0)),
            pl.BlockSpec((2, TQ, META_PAD), lambda h: (0, 0, 0)),
        ],
        out_specs=[
            pl.BlockSpec((1, 5, TQ, KC), lambda h: (h, 0, 0, 0)),
            pl.BlockSpec((1, 2, TQ, META_PAD), lambda h: (h, 0, 0, 0)),
        ],
        out_shape=[
            jax.ShapeDtypeStruct((DIFF_HEADS, 5, TQ, KC), jnp.float32),
            jax.ShapeDtypeStruct((DIFF_HEADS, 2, TQ, META_PAD), jnp.float32),
        ],
        compiler_params=pltpu.CompilerParams(dimension_semantics=("arbitrary",)),
        name="bias_tables",
    )(rel_bias.astype(jnp.float32), bk, mbk)


def _rope_tables(n_pos):
    half = MLA_ROPE // 2
    inv = ROPE_THETA ** (-jnp.arange(half, dtype=jnp.float32) / half)
    ang = jnp.arange(n_pos, dtype=jnp.int32).astype(jnp.float32)[:, None] * inv[None, :]
    c, s = jnp.cos(ang), jnp.sin(ang)
    z = jnp.zeros((n_pos, LANES - MLA_ROPE), jnp.float32)
    cos_t = jnp.concatenate([c, c, z], axis=-1)
    sin_t = jnp.concatenate([-s, s, z], axis=-1)
    return cos_t, sin_t


def _pad_rows(a, rows):
    return jnp.pad(a, [(0, rows - a.shape[0])] + [(0, 0)] * (a.ndim - 1))


def kernel(x, meta_tokens, rel_bias, norm_in, w_in, q_a_norm, kv_a_norm, w_uq, w_ukv, mla_q_norm, mla_k_norm,
           diff_q_norm, diff_k_norm, diff_lambda, diff_subln, w_branch_a, w_branch_b, w_out):
    nb, seq, _ = x.shape
    bf = jnp.bfloat16
    n_small = MLA_Q_RANK + MLA_KV_RANK + MLA_ROPE

    w_in0 = w_in[0]
    wsm = jnp.pad(w_in0[:, :n_small], ((0, 0), (0, SMALL_W - n_small))).astype(bf)
    wbig = w_in0[:, n_small:].astype(bf)
    wuq = jnp.pad(w_uq[0].reshape(MLA_Q_RANK, MLA_HEADS, MLA_QK),
                  ((0, 0), (0, 0), (0, HEAD_PAD - MLA_QK))).reshape(MLA_Q_RANK, MLA_HEADS * HEAD_PAD).astype(bf)
    wukv = w_ukv[0].astype(bf)
    wa, wb, wo = w_branch_a[0].astype(bf), w_branch_b[0].astype(bf), w_out[0].astype(bf)

    gin = norm_in[0][None]
    gqa, gkva = q_a_norm[0][None], kv_a_norm[0][None]
    gq = jnp.pad(mla_q_norm[0], (0, HEAD_PAD - MLA_QK))[None]
    gk = jnp.pad(mla_k_norm[0], (0, HEAD_PAD - MLA_QK))[None]
    gqd, gkd = diff_q_norm[0][None], diff_k_norm[0][None]
    subln = diff_subln[0][None]
    lam = diff_lambda[0]

    cos_t, sin_t = _rope_tables(N_META + seq)
    btab, mtab = _bias_tables(rel_bias)
    mla_mask = jnp.where(jnp.arange(META_PAD) < N_META, 0.0, MASK_VALUE).astype(jnp.float32)[None]

    x2 = x.reshape(nb * seq, D_MODEL)
    meta = meta_tokens.astype(x.dtype)

    qd_scale = LOG2E * DIFF_QK ** -0.5
    reps = D_MODEL // DIFF_QK
    gain_qk = jnp.concatenate([jnp.tile(gqd * qd_scale, (1, reps)), jnp.tile(gkd, (1, reps))], axis=-1)

    q, k, v, u = _mla_proj(x, gin, wsm, gqa, gkva, wuq, wukv, gq, gk, cos_t[N_META:], sin_t[N_META:], tm=256)
    u2 = u.reshape(nb * seq, D_MODEL)
    zs = _seg_proj(u2, wbig, (SEG_ZA, SEG_ZB), "silu", 1024, 1024)
    qk = _seg_proj(u2, wbig, (SEG_QD, SEG_KD), "norm", 1024, 1024, gain=gain_qk)
    vd = _seg_proj(u2, wbig, (SEG_VD,), "copy", 1024, 1024)
    gs = _seg_proj(u2, wbig, (SEG_GA, SEG_GB), "sigmoid", 1024, 1024)

    _, km, vm, um = _mla_proj(meta[None], gin, wsm, gqa, gkva, wuq, wukv, gq, gk,
                              cos_t[:N_META], sin_t[:N_META], tm=N_META)
    km = jnp.pad(km[0], ((0, 0), (0, META_PAD - N_META), (0, 0)))
    vm = jnp.pad(vm[0], ((0, 0), (0, META_PAD - N_META), (0, 0)))
    k_meta = _pad_rows(_seg_proj(um[0], wbig, (SEG_KD,), "norm", N_META, 1024, gain=gain_qk[:, D_MODEL:]), META_PAD)
    vd_meta = _pad_rows(_seg_proj(um[0], wbig, (SEG_VD,), "copy", N_META, 1024), META_PAD)

    oa = _mla_attn(q, k, v, km, vm, mla_mask, zs, seq)
    ob = _diff_attn(qk, vd, zs, k_meta, vd_meta, btab, mtab, lam, subln, nb, seq)
    m = _merge(oa, ob, wa, wb, gs, tm=1024, tn=1024)
    out = _out_proj(m, wo, x2, tm=1024, tn=1024)
    return out.reshape(nb, seq, D_MODEL)
```

```python
import functools
import math

import jax
import jax.numpy as jnp
from jax import lax
from jax.experimental import pallas as pl
from jax.experimental.pallas import tpu as pltpu

D_MODEL = 2048
N_META = 16
EPS = 1e-6

MLA_HEADS = 16
MLA_Q_RANK = 768
MLA_KV_RANK = 512
MLA_NOPE = 128
MLA_ROPE = 64
MLA_V = 128
MLA_QK = MLA_NOPE + MLA_ROPE
ROPE_THETA = 10000.0

DIFF_HEADS = 8
DIFF_QK = 128
DIFF_V = 2 * DIFF_QK
LAM_INIT = 0.8 - 0.6 * math.exp(-0.3 * 0)

REL_BUCKETS = 32
REL_MAX_DIST = 128

LANES = 128
MXU_DIM = 256
VMEM_LIMIT_BYTES = 56 * 1024 * 1024

HEAD_PAD = MXU_DIM
SMALL_W = MLA_Q_RANK + MLA_KV_RANK + LANES
META_PAD = LANES
MASK_VALUE = -1e30
TQ = 256
KC = 256
LOG2E = math.log2(math.e)

SEG_ZA, SEG_QD, SEG_KD, SEG_VD, SEG_ZB, SEG_GA, SEG_GB = range(7)


def _dot(a, b):
    return jnp.dot(a, b, preferred_element_type=jnp.float32)


def _dot_nt(a, b):
    return lax.dot_general(a, b, (((1,), (1,)), ((), ())), preferred_element_type=jnp.float32)


def _rms_scale(x, width):
    return lax.rsqrt(jnp.sum(x * x, axis=-1, keepdims=True) * (1.0 / width) + EPS)


def _rot_half64(x):
    return pltpu.roll(x, 32, 1) + pltpu.roll(x, 96, 1)


def _mla_proj_kernel(x_ref, gin_ref, wsm_ref, gqa_ref, gkva_ref, wuq_ref, wukv_ref,
                     gq_ref, gk_ref, cos_ref, sin_ref, q_ref, k_ref, v_ref, u_ref, *, q_scale):
    x = x_ref[...]
    u = (x * _rms_scale(x, D_MODEL) * gin_ref[...]).astype(jnp.bfloat16)
    u_ref[...] = u
    p = _dot(u, wsm_ref[...])
    cq = p[:, :MLA_Q_RANK]
    ckv = p[:, MLA_Q_RANK:MLA_Q_RANK + MLA_KV_RANK]
    kr = p[:, MLA_Q_RANK + MLA_KV_RANK:]
    cqn = (cq * _rms_scale(cq, MLA_Q_RANK) * gqa_ref[...]).astype(jnp.bfloat16)
    ckvn = (ckv * _rms_scale(ckv, MLA_KV_RANK) * gkva_ref[...]).astype(jnp.bfloat16)
    qf = _dot(cqn, wuq_ref[...])
    kvf = _dot(ckvn, wukv_ref[...])

    cos = cos_ref[...]
    sin = sin_ref[...]
    gq = gq_ref[...]
    gk = gk_ref[...]
    gq_nope, gq_rope = gq[:, :LANES], gq[:, LANES:]
    gk_nope, gk_rope = gk[:, :LANES], gk[:, LANES:]

    ss_kr = jnp.sum(kr * kr, axis=-1, keepdims=True)
    krg = kr * gk_rope
    kr_roped = krg * cos + _rot_half64(krg) * sin

    for h in range(MLA_HEADS):
        qh = qf[:, h * HEAD_PAD:(h + 1) * HEAD_PAD]
        rq = _rms_scale(qh, MLA_QK)
        q_nope = qh[:, :LANES] * rq * gq_nope
        q_r = qh[:, LANES:] * rq * gq_rope
        q_r = q_r * cos + _rot_half64(q_r) * sin
        q_ref[0, h, :, :LANES] = (q_nope * q_scale).astype(jnp.bfloat16)
        q_ref[0, h, :, LANES:] = (q_r * q_scale).astype(jnp.bfloat16)

        k_nope = kvf[:, h * HEAD_PAD:h * HEAD_PAD + MLA_NOPE]
        rk = lax.rsqrt((jnp.sum(k_nope * k_nope, axis=-1, keepdims=True) + ss_kr) * (1.0 / MLA_QK) + EPS)
        k_ref[0, h, :, :LANES] = (k_nope * rk * gk_nope).astype(jnp.bfloat16)
        k_ref[0, h, :, LANES:] = (kr_roped * rk).astype(jnp.bfloat16)
        v_ref[0, h] = kvf[:, h * HEAD_PAD + MLA_NOPE:(h + 1) * HEAD_PAD].astype(jnp.bfloat16)


def _mla_proj(x3, gin, wsm, gqa, gkva, wuq, wukv, gq, gk, cos, sin, tm):
    nb, rows, _ = x3.shape
    const = lambda b, i: (0, 0)
    kern = functools.partial(_mla_proj_kernel, q_scale=LOG2E * MLA_QK ** -0.5)
    return pl.pallas_call(
        kern,
        grid=(nb, rows // tm),
        in_specs=[
            pl.BlockSpec((None, tm, D_MODEL), lambda b, i: (b, i, 0)),
            pl.BlockSpec((1, D_MODEL), const),
            pl.BlockSpec((D_MODEL, SMALL_W), const),
            pl.BlockSpec((1, MLA_Q_RANK), const),
            pl.BlockSpec((1, MLA_KV_RANK), const),
            pl.BlockSpec((MLA_Q_RANK, MLA_HEADS * HEAD_PAD), const),
            pl.BlockSpec((MLA_KV_RANK, MLA_HEADS * HEAD_PAD), const),
            pl.BlockSpec((1, HEAD_PAD), const),
            pl.BlockSpec((1, HEAD_PAD), const),
            pl.BlockSpec((tm, LANES), lambda b, i: (i, 0)),
            pl.BlockSpec((tm, LANES), lambda b, i: (i, 0)),
        ],
        out_specs=[
            pl.BlockSpec((1, MLA_HEADS, tm, HEAD_PAD), lambda b, i: (b, 0, i, 0)),
            pl.BlockSpec((1, MLA_HEADS, tm, HEAD_PAD), lambda b, i: (b, 0, i, 0)),
            pl.BlockSpec((1, MLA_HEADS, tm, MLA_V), lambda b, i: (b, 0, i, 0)),
            pl.BlockSpec((None, tm, D_MODEL), lambda b, i: (b, i, 0)),
        ],
        out_shape=[
            jax.ShapeDtypeStruct((nb, MLA_HEADS, rows, HEAD_PAD), jnp.bfloat16),
            jax.ShapeDtypeStruct((nb, MLA_HEADS, rows, HEAD_PAD), jnp.bfloat16),
            jax.ShapeDtypeStruct((nb, MLA_HEADS, rows, MLA_V), jnp.bfloat16),
            jax.ShapeDtypeStruct((nb, rows, D_MODEL), jnp.bfloat16),
        ],
        compiler_params=pltpu.CompilerParams(
            dimension_semantics=("arbitrary", "arbitrary"), vmem_limit_bytes=VMEM_LIMIT_BYTES),
        name="mla_proj",
    )(x3, gin, wsm, gqa, gkva, wuq, wukv, gq, gk, cos, sin)


def _seg_proj_kernel(u_ref, w_ref, *rest, kind):
    o_ref = rest[-1]
    acc = _dot(u_ref[...], w_ref[...])
    if kind == "silu":
        o_ref[...] = (acc / (1.0 + jnp.exp(-acc))).astype(o_ref.dtype)
    elif kind == "sigmoid":
        o_ref[...] = (1.0 / (1.0 + jnp.exp(-acc))).astype(o_ref.dtype)
    elif kind == "copy":
        o_ref[...] = acc.astype(o_ref.dtype)
    else:
        gain = rest[0][...]
        for c in range(acc.shape[1] // DIFF_QK):
            cols = slice(c * DIFF_QK, (c + 1) * DIFF_QK)
            g = acc[:, cols]
            o_ref[:, cols] = (g * _rms_scale(g, DIFF_QK) * gain[:, cols]).astype(o_ref.dtype)


def _seg_proj(u2, wbig, segs, kind, tm, tn, gain=None):
    rows = u2.shape[0]
    tps = D_MODEL // tn
    if len(segs) == 1:
        w_map = lambda i, j: (0, segs[0] * tps + j)
    else:
        w_map = lambda i, j: (0, jnp.where(j < tps, segs[0] * tps + j, segs[1] * tps + j - tps))
    in_specs = [pl.BlockSpec((tm, D_MODEL), lambda i, j: (i, 0)), pl.BlockSpec((D_MODEL, tn), w_map)]
    args = [u2, wbig]
    if gain is not None:
        in_specs.append(pl.BlockSpec((1, tn), lambda i, j: (0, j)))
        args.append(gain)
    return pl.pallas_call(
        functools.partial(_seg_proj_kernel, kind=kind),
        grid=(rows // tm, len(segs) * tps),
        in_specs=in_specs,
        out_specs=pl.BlockSpec((tm, tn), lambda i, j: (i, j)),
        out_shape=jax.ShapeDtypeStruct((rows, len(segs) * D_MODEL), jnp.bfloat16),
        compiler_params=pltpu.CompilerParams(
            dimension_semantics=("arbitrary", "arbitrary"), vmem_limit_bytes=VMEM_LIMIT_BYTES),
        name="in_proj_" + kind,
    )(*args)


MLA_PER_STEP = MLA_HEADS // DIFF_HEADS


def _mla_tile_fns(q_ref, k_ref, v_ref, km_ref, vm_ref, mask_ref, za_ref, o_ref, s_ref, sm_ref):
    n_t = q_ref.shape[2] // TQ

    def scores(u, slot):
        hh, t = divmod(u, n_t)
        qt = q_ref[0, hh, t * TQ:(t + 1) * TQ, :]
        s_ref[slot] = _dot_nt(qt, k_ref[0, hh])
        sm_ref[slot] = _dot_nt(qt, km_ref[hh]) + mask_ref[...]

    def finish(u, slot):
        hh, t = divmod(u, n_t)
        rows = slice(t * TQ, (t + 1) * TQ)
        cols = slice(hh * MLA_V, (hh + 1) * MLA_V)
        s = s_ref[slot]
        sm = sm_ref[slot]
        m = jnp.maximum(jnp.max(s, axis=-1, keepdims=True), jnp.max(sm, axis=-1, keepdims=True))
        p = jnp.exp2(s - m).astype(jnp.bfloat16)
        pm = jnp.exp2(sm - m).astype(jnp.bfloat16)
        ones = jnp.ones((v_ref.shape[2], MLA_V), jnp.bfloat16)
        ones_m = jnp.ones((META_PAD, MLA_V), jnp.bfloat16)
        o = (_dot(p, jnp.concatenate([v_ref[0, hh], ones], axis=-1))
             + _dot(pm, jnp.concatenate([vm_ref[hh], ones_m], axis=-1)))
        l = o[:, MLA_V:MLA_V + 1]
        y = o[:, :MLA_V] * (1.0 / l) * za_ref[rows, cols].astype(jnp.float32)
        o_ref[rows, cols] = y.astype(o_ref.dtype)

    return scores, finish


def _diff_tile_fns(qd_ref, kd_ref, vd_ref, kdm_ref, vdm_ref, btab_ref, mtab_ref, lam_ref, subln_ref, zb_ref,
                   o_ref, s_ref, sm_ref):
    n_chunks = kd_ref.shape[0] // KC
    lv = lam_ref[...]
    lam = (jnp.exp(jnp.sum(lv[0:1] * lv[1:2], axis=-1, keepdims=True))
           - jnp.exp(jnp.sum(lv[2:3] * lv[3:4], axis=-1, keepdims=True)) + LAM_INIT)

    def scores(t, slot):
        for half in range(2):
            lo = half * DIFF_QK
            qm = qd_ref[t * TQ:(t + 1) * TQ, lo:lo + DIFF_QK]
            for c in range(n_chunks):
                kc = kd_ref[c * KC:(c + 1) * KC, lo:lo + DIFF_QK]
                s_ref[slot, half, :, c * KC:(c + 1) * KC] = (
                    _dot_nt(qm, kc) + btab_ref[0, min(max(c - t, -2), 2) + 2])
            sm_ref[slot, half] = _dot_nt(qm, kdm_ref[:, lo:lo + DIFF_QK]) + mtab_ref[0, min(t, 1)]

    def softmax_parts(slot, half):
        s = s_ref[slot, half]
        sm = sm_ref[slot, half]
        m = jnp.maximum(jnp.max(s, axis=-1, keepdims=True), jnp.max(sm, axis=-1, keepdims=True))
        p = jnp.exp2(s - m)
        pm = jnp.exp2(sm - m)
        l = jnp.sum(p, axis=-1, keepdims=True) + jnp.sum(pm, axis=-1, keepdims=True)
        return p, pm, l

    def finish(t, slot):
        rows = slice(t * TQ, (t + 1) * TQ)
        p1, pm1, l1 = softmax_parts(slot, 0)
        p2, pm2, l2 = softmax_parts(slot, 1)
        r = lam * l1 * (1.0 / l2)
        a = (p1 - p2 * r).astype(jnp.bfloat16)
        am = (pm1 - pm2 * r).astype(jnp.bfloat16)
        o = (_dot(a, vd_ref[...]) + _dot(am, vdm_ref[...])) * (1.0 / l1)
        y = o * _rms_scale(o, DIFF_V) * subln_ref[...] * (1.0 - LAM_INIT)
        y = y * zb_ref[rows, :].astype(jnp.float32)
        o_ref[rows, :] = y.astype(o_ref.dtype)

    return scores, finish


def _attn_kernel(q_ref, k_ref, v_ref, km_ref, vm_ref, mask_ref, za_ref,
                 qd_ref, kd_ref, vd_ref, kdm_ref, vdm_ref, btab_ref, mtab_ref, lam_ref, subln_ref, zb_ref,
                 oa_ref, ob_ref, ms_ref, msm_ref, ds_ref, dsm_ref):
    m_scores, m_finish = _mla_tile_fns(q_ref, k_ref, v_ref, km_ref, vm_ref, mask_ref, za_ref, oa_ref,
                                       ms_ref, msm_ref)
    d_scores, d_finish = _diff_tile_fns(qd_ref, kd_ref, vd_ref, kdm_ref, vdm_ref, btab_ref, mtab_ref, lam_ref,
                                        subln_ref, zb_ref, ob_ref, ds_ref, dsm_ref)
    n_t = qd_ref.shape[0] // TQ
    n_u = n_t * MLA_PER_STEP
    d_scores(0, 0)
    m_scores(0, 0)
    for t in range(n_t):
        if t + 1 < n_t:
            d_scores(t + 1, (t + 1) % 2)
        for u in range(t * MLA_PER_STEP, (t + 1) * MLA_PER_STEP):
            if u + 1 < n_u:
                m_scores(u + 1, (u + 1) % 2)
            m_finish(u, u % 2)
        d_finish(t, t % 2)


def _attention(q, k, v, km, vm, mask, zs, qk, vd, qk_meta, vd_meta, btab, mtab, lam, subln, seq):
    nb = q.shape[0]
    cps = D_MODEL // DIFF_V
    mps = MLA_PER_STEP
    return pl.pallas_call(
        _attn_kernel,
        grid=(nb, DIFF_HEADS),
        in_specs=[
            pl.BlockSpec((1, mps, seq, HEAD_PAD), lambda b, h: (b, h, 0, 0)),
            pl.BlockSpec((1, mps, seq, HEAD_PAD), lambda b, h: (b, h, 0, 0)),
            pl.BlockSpec((1, mps, seq, MLA_V), lambda b, h: (b, h, 0, 0)),
            pl.BlockSpec((mps, META_PAD, HEAD_PAD), lambda b, h: (h, 0, 0)),
            pl.BlockSpec((mps, META_PAD, MLA_V), lambda b, h: (h, 0, 0)),
            pl.BlockSpec((1, META_PAD), lambda b, h: (0, 0)),
            pl.BlockSpec((seq, mps * MLA_V), lambda b, h: (b, h)),
            pl.BlockSpec((seq, DIFF_V), lambda b, h: (b, h)),
            pl.BlockSpec((seq, DIFF_V), lambda b, h: (b, cps + h)),
            pl.BlockSpec((seq, DIFF_V), lambda b, h: (b, h)),
            pl.BlockSpec((META_PAD, DIFF_V), lambda b, h: (0, h)),
            pl.BlockSpec((META_PAD, DIFF_V), lambda b, h: (0, h)),
            pl.BlockSpec((1, 5, TQ, KC), lambda b, h: (h, 0, 0, 0)),
            pl.BlockSpec((1, 2, TQ, META_PAD), lambda b, h: (h, 0, 0, 0)),
            pl.BlockSpec((4, DIFF_QK), lambda b, h: (0, 0)),
            pl.BlockSpec((1, DIFF_V), lambda b, h: (0, 0)),
            pl.BlockSpec((seq, DIFF_V), lambda b, h: (b, cps + h)),
        ],
        out_specs=[
            pl.BlockSpec((seq, mps * MLA_V), lambda b, h: (b, h)),
            pl.BlockSpec((seq, DIFF_V), lambda b, h: (b, h)),
        ],
        out_shape=[
            jax.ShapeDtypeStruct((nb * seq, MLA_HEADS * MLA_V), jnp.bfloat16),
            jax.ShapeDtypeStruct((nb * seq, DIFF_HEADS * DIFF_V), jnp.bfloat16),
        ],
        scratch_shapes=[
            pltpu.VMEM((2, TQ, seq), jnp.float32), pltpu.VMEM((2, TQ, META_PAD), jnp.float32),
            pltpu.VMEM((2, 2, TQ, seq), jnp.float32), pltpu.VMEM((2, 2, TQ, META_PAD), jnp.float32),
        ],
        compiler_params=pltpu.CompilerParams(
            dimension_semantics=("arbitrary", "arbitrary"), vmem_limit_bytes=VMEM_LIMIT_BYTES),
        name="attention",
    )(q, k, v, km, vm, mask, zs, qk, qk, vd, qk_meta, vd_meta, btab, mtab, lam, subln, zs)


def _merge_kernel(oa_ref, ob_ref, wa_ref, wb_ref, ga_ref, gb_ref, m_ref):
    ya = _dot(oa_ref[...], wa_ref[...])
    yb = _dot(ob_ref[...], wb_ref[...])
    m = ga_ref[...].astype(jnp.float32) * ya + gb_ref[...].astype(jnp.float32) * yb
    m_ref[...] = m.astype(m_ref.dtype)


def _merge(oa, ob, wa, wb, gs, tm, tn):
    rows = oa.shape[0]
    tps = D_MODEL // tn
    return pl.pallas_call(
        _merge_kernel,
        grid=(rows // tm, D_MODEL // tn),
        in_specs=[
            pl.BlockSpec((tm, D_MODEL), lambda i, j: (i, 0)),
            pl.BlockSpec((tm, D_MODEL), lambda i, j: (i, 0)),
            pl.BlockSpec((D_MODEL, tn), lambda i, j: (0, j)),
            pl.BlockSpec((D_MODEL, tn), lambda i, j: (0, j)),
            pl.BlockSpec((tm, tn), lambda i, j: (i, j)),
            pl.BlockSpec((tm, tn), lambda i, j: (i, tps + j)),
        ],
        out_specs=pl.BlockSpec((tm, tn), lambda i, j: (i, j)),
        out_shape=jax.ShapeDtypeStruct((rows, D_MODEL), jnp.bfloat16),
        compiler_params=pltpu.CompilerParams(
            dimension_semantics=("arbitrary", "arbitrary"), vmem_limit_bytes=VMEM_LIMIT_BYTES),
        name="merge",
    )(oa, ob, wa, wb, gs, gs)


def _out_kernel(m_ref, w_ref, x_ref, o_ref):
    o_ref[...] = x_ref[...] + _dot(m_ref[...], w_ref[...])


def _out_proj(m, wout, x2, tm, tn):
    rows = m.shape[0]
    return pl.pallas_call(
        _out_kernel,
        grid=(rows // tm, D_MODEL // tn),
        in_specs=[
            pl.BlockSpec((tm, D_MODEL), lambda i, j: (i, 0)),
            pl.BlockSpec((D_MODEL, tn), lambda i, j: (0, j)),
            pl.BlockSpec((tm, tn), lambda i, j: (i, j)),
        ],
        out_specs=pl.BlockSpec((tm, tn), lambda i, j: (i, j)),
        out_shape=jax.ShapeDtypeStruct((rows, D_MODEL), jnp.float32),
        compiler_params=pltpu.CompilerParams(
            dimension_semantics=("arbitrary", "arbitrary"), vmem_limit_bytes=VMEM_LIMIT_BYTES),
        name="out_proj",
    )(m, wout, x2)


def _t5_bucket(rel):
    nb = REL_BUCKETS // 2
    max_exact = nb // 2
    ret = jnp.where(rel > 0, nb, 0)
    n = jnp.abs(rel)
    nf = jnp.maximum(n, 1).astype(jnp.float32)
    large = max_exact + (jnp.log(nf / max_exact) / math.log(REL_MAX_DIST / max_exact)
                         * (nb - max_exact)).astype(jnp.int32)
    large = jnp.minimum(large, nb - 1)
    return ret + jnp.where(n < max_exact, n, large)


def _bias_tab_kernel(rb_ref, bk_ref, mbk_ref, btab_ref, mtab_ref):
    h = pl.program_id(0)
    bk = bk_ref[...]
    mbk = mbk_ref[...]
    acc = jnp.zeros(bk.shape, jnp.float32)
    macc = jnp.where(mbk < 0, MASK_VALUE, 0.0).astype(jnp.float32)
    for b in range(REL_BUCKETS):
        val = rb_ref[b, h] * LOG2E
        acc = jnp.where(bk == b, val, acc)
        macc = jnp.where(mbk == b, val, macc)
    btab_ref[0] = acc
    mtab_ref[0] = macc


def _bias_tables(rel_bias):
    qq = jnp.arange(TQ, dtype=jnp.int32)[:, None]
    kk = jnp.arange(KC, dtype=jnp.int32)[None, :]
    dd = jnp.arange(-2, 3, dtype=jnp.int32)[:, None, None]
    bk = _t5_bucket(dd * KC + kk[None] - qq[None])
    jm = jnp.arange(META_PAD, dtype=jnp.int32)[None, None, :]
    qpos = N_META + jnp.arange(2, dtype=jnp.int32)[:, None, None] * TQ + qq[None]
    mbk = jnp.where(jm < N_META, _t5_bucket(jm - qpos), -1)
    return pl.pallas_call(
        _bias_tab_kernel,
        grid=(DIFF_HEADS,),
        in_specs=[
            pl.BlockSpec(memory_space=pltpu.SMEM),
            pl.BlockSpec((5, TQ, KC), lambda h: (0, 0, 0)),
            pl.BlockSpec((2, TQ, META_PAD), lambda h: (0, 0, 0)),
        ],
        out_specs=[
            pl.BlockSpec((1, 5, TQ, KC), lambda h: (h, 0, 0, 0)),
            pl.BlockSpec((1, 2, TQ, META_PAD), lambda h: (h, 0, 0, 0)),
        ],
        out_shape=[
            jax.ShapeDtypeStruct((DIFF_HEADS, 5, TQ, KC), jnp.float32),
            jax.ShapeDtypeStruct((DIFF_HEADS, 2, TQ, META_PAD), jnp.float32),
        ],
        compiler_params=pltpu.CompilerParams(dimension_semantics=("arbitrary",)),
        name="bias_tables",
    )(rel_bias.astype(jnp.float32), bk, mbk)


def _rope_tables(n_pos):
    half = MLA_ROPE // 2
    inv = ROPE_THETA ** (-jnp.arange(half, dtype=jnp.float32) / half)
    ang = jnp.arange(n_pos, dtype=jnp.int32).astype(jnp.float32)[:, None] * inv[None, :]
    c, s = jnp.cos(ang), jnp.sin(ang)
    z = jnp.zeros((n_pos, LANES - MLA_ROPE), jnp.float32)
    cos_t = jnp.concatenate([c, c, z], axis=-1)
    sin_t = jnp.concatenate([-s, s, z], axis=-1)
    return cos_t, sin_t


def _pad_rows(a, rows):
    return jnp.pad(a, [(0, rows - a.shape[0])] + [(0, 0)] * (a.ndim - 1))


def kernel(x, meta_tokens, rel_bias, norm_in, w_in, q_a_norm, kv_a_norm, w_uq, w_ukv, mla_q_norm, mla_k_norm,
           diff_q_norm, diff_k_norm, diff_lambda, diff_subln, w_branch_a, w_branch_b, w_out):
    nb, seq, _ = x.shape
    bf = jnp.bfloat16
    n_small = MLA_Q_RANK + MLA_KV_RANK + MLA_ROPE

    w_in0 = w_in[0]
    wsm = jnp.pad(w_in0[:, :n_small], ((0, 0), (0, SMALL_W - n_small))).astype(bf)
    wbig = w_in0[:, n_small:].astype(bf)
    wuq = jnp.pad(w_uq[0].reshape(MLA_Q_RANK, MLA_HEADS, MLA_QK),
                  ((0, 0), (0, 0), (0, HEAD_PAD - MLA_QK))).reshape(MLA_Q_RANK, MLA_HEADS * HEAD_PAD).astype(bf)
    wukv = w_ukv[0].astype(bf)
    wa, wb, wo = w_branch_a[0].astype(bf), w_branch_b[0].astype(bf), w_out[0].astype(bf)

    gin = norm_in[0][None]
    gqa, gkva = q_a_norm[0][None], kv_a_norm[0][None]
    gq = jnp.pad(mla_q_norm[0], (0, HEAD_PAD - MLA_QK))[None]
    gk = jnp.pad(mla_k_norm[0], (0, HEAD_PAD - MLA_QK))[None]
    gqd, gkd = diff_q_norm[0][None], diff_k_norm[0][None]
    subln = diff_subln[0][None]
    lam = diff_lambda[0]

    cos_t, sin_t = _rope_tables(N_META + seq)
    btab, mtab = _bias_tables(rel_bias)
    mla_mask = jnp.where(jnp.arange(META_PAD) < N_META, 0.0, MASK_VALUE).astype(jnp.float32)[None]

    x2 = x.reshape(nb * seq, D_MODEL)
    meta = meta_tokens.astype(x.dtype)

    qd_scale = LOG2E * DIFF_QK ** -0.5
    reps = D_MODEL // DIFF_QK
    gain_qk = jnp.concatenate([jnp.tile(gqd * qd_scale, (1, reps)), jnp.tile(gkd, (1, reps))], axis=-1)

    q, k, v, u = _mla_proj(x, gin, wsm, gqa, gkva, wuq, wukv, gq, gk, cos_t[N_META:], sin_t[N_META:], tm=256)
    u2 = u.reshape(nb * seq, D_MODEL)
    zs = _seg_proj(u2, wbig, (SEG_ZA, SEG_ZB), "silu", 1024, 1024)
    qk = _seg_proj(u2, wbig, (SEG_QD, SEG_KD), "norm", 1024, 1024, gain=gain_qk)
    vd = _seg_proj(u2, wbig, (SEG_VD,), "copy", 1024, 1024)
    gs = _seg_proj(u2, wbig, (SEG_GA, SEG_GB), "sigmoid", 1024, 1024)

    _, km, vm, um = _mla_proj(meta[None], gin, wsm, gqa, gkva, wuq, wukv, gq, gk,
                              cos_t[:N_META], sin_t[:N_META], tm=N_META)
    km = jnp.pad(km[0], ((0, 0), (0, META_PAD - N_META), (0, 0)))
    vm = jnp.pad(vm[0], ((0, 0), (0, META_PAD - N_META), (0, 0)))
    k_meta = _pad_rows(_seg_proj(um[0], wbig, (SEG_KD,), "norm", N_META, 1024, gain=gain_qk[:, D_MODEL:]), META_PAD)
    vd_meta = _pad_rows(_seg_proj(um[0], wbig, (SEG_VD,), "copy", N_META, 1024), META_PAD)

    oa, ob = _attention(q, k, v, km, vm, mla_mask, zs, qk, vd, k_meta, vd_meta, btab, mtab, lam, subln, seq)
    m = _merge(oa, ob, wa, wb, gs, tm=1024, tn=1024)
    out = _out_proj(m, wo, x2, tm=1024, tn=1024)
    return out.reshape(nb, seq, D_MODEL)
```

```python
import functools
import math

import jax
import jax.numpy as jnp
from jax import lax
from jax.experimental import pallas as pl
from jax.experimental.pallas import tpu as pltpu

D_MODEL = 2048
N_META = 16
EPS = 1e-6

MLA_HEADS = 16
MLA_Q_RANK = 768
MLA_KV_RANK = 512
MLA_NOPE = 128
MLA_ROPE = 64
MLA_V = 128
MLA_QK = MLA_NOPE + MLA_ROPE
ROPE_THETA = 10000.0

DIFF_HEADS = 8
DIFF_QK = 128
DIFF_V = 2 * DIFF_QK
LAM_INIT = 0.8 - 0.6 * math.exp(-0.3 * 0)

REL_BUCKETS = 32
REL_MAX_DIST = 128

LANES = 128
MXU_DIM = 256
VMEM_LIMIT_BYTES = 56 * 1024 * 1024

HEAD_PAD = MXU_DIM
SMALL_W = MLA_Q_RANK + MLA_KV_RANK + LANES
META_PAD = LANES
MASK_VALUE = -1e30
TQ = 256
KC = 256
LOG2E = math.log2(math.e)

SEG_ZA, SEG_QD, SEG_KD, SEG_VD, SEG_ZB, SEG_GA, SEG_GB = range(1, 8)


def _dot(a, b):
    return jnp.dot(a, b, preferred_element_type=jnp.float32)


def _dot_nt(a, b):
    return lax.dot_general(a, b, (((1,), (1,)), ((), ())), preferred_element_type=jnp.float32)


def _rms_scale(x, width):
    return lax.rsqrt(jnp.sum(x * x, axis=-1, keepdims=True) * (1.0 / width) + EPS)


def _rot_half64(x):
    return pltpu.roll(x, 32, 1) + pltpu.roll(x, 96, 1)


def _mla_proj_kernel(x_ref, gin_ref, wsm_ref, gqa_ref, gkva_ref, wuq_ref, wukv_ref,
                     gq_ref, gk_ref, cos_ref, sin_ref, q_ref, k_ref, v_ref, u_ref, *, q_scale):
    x = x_ref[...]
    u = (x * _rms_scale(x, D_MODEL) * gin_ref[...]).astype(jnp.bfloat16)
    u_ref[...] = u
    p = _dot(u, wsm_ref[...])
    cq = p[:, :MLA_Q_RANK]
    ckv = p[:, MLA_Q_RANK:MLA_Q_RANK + MLA_KV_RANK]
    kr = p[:, MLA_Q_RANK + MLA_KV_RANK:]
    cqn = (cq * _rms_scale(cq, MLA_Q_RANK) * gqa_ref[...]).astype(jnp.bfloat16)
    ckvn = (ckv * _rms_scale(ckv, MLA_KV_RANK) * gkva_ref[...]).astype(jnp.bfloat16)
    qf = _dot(cqn, wuq_ref[...])
    kvf = _dot(ckvn, wukv_ref[...])

    cos = cos_ref[...]
    sin = sin_ref[...]
    gq = gq_ref[...]
    gk = gk_ref[...]
    gq_nope, gq_rope = gq[:, :LANES], gq[:, LANES:]
    gk_nope, gk_rope = gk[:, :LANES], gk[:, LANES:]

    ss_kr = jnp.sum(kr * kr, axis=-1, keepdims=True)
    krg = kr * gk_rope
    kr_roped = krg * cos + _rot_half64(krg) * sin

    for h in range(MLA_HEADS):
        qh = qf[:, h * HEAD_PAD:(h + 1) * HEAD_PAD]
        rq = _rms_scale(qh, MLA_QK)
        q_nope = qh[:, :LANES] * rq * gq_nope
        q_r = qh[:, LANES:] * rq * gq_rope
        q_r = q_r * cos + _rot_half64(q_r) * sin
        q_ref[0, h, :, :LANES] = (q_nope * q_scale).astype(jnp.bfloat16)
        q_ref[0, h, :, LANES:] = (q_r * q_scale).astype(jnp.bfloat16)

        k_nope = kvf[:, h * HEAD_PAD:h * HEAD_PAD + MLA_NOPE]
        rk = lax.rsqrt((jnp.sum(k_nope * k_nope, axis=-1, keepdims=True) + ss_kr) * (1.0 / MLA_QK) + EPS)
        k_ref[0, h, :, :LANES] = (k_nope * rk * gk_nope).astype(jnp.bfloat16)
        k_ref[0, h, :, LANES:] = (kr_roped * rk).astype(jnp.bfloat16)
        v_ref[0, h] = kvf[:, h * HEAD_PAD + MLA_NOPE:(h + 1) * HEAD_PAD].astype(jnp.bfloat16)


def _mla_proj(x3, gin, wsm, gqa, gkva, wuq, wukv, gq, gk, cos, sin, tm):
    nb, rows, _ = x3.shape
    const = lambda b, i: (0, 0)
    kern = functools.partial(_mla_proj_kernel, q_scale=LOG2E * MLA_QK ** -0.5)
    return pl.pallas_call(
        kern,
        grid=(nb, rows // tm),
        in_specs=[
            pl.BlockSpec((None, tm, D_MODEL), lambda b, i: (b, i, 0)),
            pl.BlockSpec((1, D_MODEL), const),
            pl.BlockSpec((D_MODEL, SMALL_W), const),
            pl.BlockSpec((1, MLA_Q_RANK), const),
            pl.BlockSpec((1, MLA_KV_RANK), const),
            pl.BlockSpec((MLA_Q_RANK, MLA_HEADS * HEAD_PAD), const),
            pl.BlockSpec((MLA_KV_RANK, MLA_HEADS * HEAD_PAD), const),
            pl.BlockSpec((1, HEAD_PAD), const),
            pl.BlockSpec((1, HEAD_PAD), const),
            pl.BlockSpec((tm, LANES), lambda b, i: (i, 0)),
            pl.BlockSpec((tm, LANES), lambda b, i: (i, 0)),
        ],
        out_specs=[
            pl.BlockSpec((1, MLA_HEADS, tm, HEAD_PAD), lambda b, i: (b, 0, i, 0)),
            pl.BlockSpec((1, MLA_HEADS, tm, HEAD_PAD), lambda b, i: (b, 0, i, 0)),
            pl.BlockSpec((1, MLA_HEADS, tm, MLA_V), lambda b, i: (b, 0, i, 0)),
            pl.BlockSpec((None, tm, D_MODEL), lambda b, i: (b, i, 0)),
        ],
        out_shape=[
            jax.ShapeDtypeStruct((nb, MLA_HEADS, rows, HEAD_PAD), jnp.bfloat16),
            jax.ShapeDtypeStruct((nb, MLA_HEADS, rows, HEAD_PAD), jnp.bfloat16),
            jax.ShapeDtypeStruct((nb, MLA_HEADS, rows, MLA_V), jnp.bfloat16),
            jax.ShapeDtypeStruct((nb, rows, D_MODEL), jnp.bfloat16),
        ],
        compiler_params=pltpu.CompilerParams(
            dimension_semantics=("arbitrary", "arbitrary"), vmem_limit_bytes=VMEM_LIMIT_BYTES),
        name="mla_proj",
    )(x3, gin, wsm, gqa, gkva, wuq, wukv, gq, gk, cos, sin)


def _seg_proj_kernel(u_ref, w_ref, *rest, kind):
    o_ref = rest[-1]
    acc = _dot(u_ref[...], w_ref[...])
    if kind == "silu":
        o_ref[...] = (acc / (1.0 + jnp.exp(-acc))).astype(o_ref.dtype)
    elif kind == "sigmoid":
        o_ref[...] = (1.0 / (1.0 + jnp.exp(-acc))).astype(o_ref.dtype)
    elif kind == "copy":
        o_ref[...] = acc.astype(o_ref.dtype)
    else:
        gain = rest[0][...]
        for c in range(acc.shape[1] // DIFF_QK):
            cols = slice(c * DIFF_QK, (c + 1) * DIFF_QK)
            g = acc[:, cols]
            o_ref[:, cols] = (g * _rms_scale(g, DIFF_QK) * gain[:, cols]).astype(o_ref.dtype)


def _seg_proj(u2, wbig, segs, kind, tm, tn, gain=None):
    rows = u2.shape[0]
    tps = D_MODEL // tn
    if len(segs) == 1:
        w_map = lambda i, j: (0, segs[0] * tps + j)
    else:
        w_map = lambda i, j: (0, jnp.where(j < tps, segs[0] * tps + j, segs[1] * tps + j - tps))
    in_specs = [pl.BlockSpec((tm, D_MODEL), lambda i, j: (i, 0)), pl.BlockSpec((D_MODEL, tn), w_map)]
    args = [u2, wbig]
    if gain is not None:
        in_specs.append(pl.BlockSpec((1, tn), lambda i, j: (0, j)))
        args.append(gain)
    return pl.pallas_call(
        functools.partial(_seg_proj_kernel, kind=kind),
        grid=(rows // tm, len(segs) * tps),
        in_specs=in_specs,
        out_specs=pl.BlockSpec((tm, tn), lambda i, j: (i, j)),
        out_shape=jax.ShapeDtypeStruct((rows, len(segs) * D_MODEL), jnp.bfloat16),
        compiler_params=pltpu.CompilerParams(
            dimension_semantics=("arbitrary", "arbitrary"), vmem_limit_bytes=VMEM_LIMIT_BYTES),
        name="in_proj_" + kind,
    )(*args)


MLA_PER_STEP = MLA_HEADS // DIFF_HEADS


def _mla_tile_fns(q_ref, k_ref, v_ref, km_ref, vm_ref, mask_ref, za_ref, o_ref, s_ref, sm_ref):
    n_t = q_ref.shape[2] // TQ

    def scores(u, slot):
        hh, t = divmod(u, n_t)
        qt = q_ref[0, hh, t * TQ:(t + 1) * TQ, :]
        s_ref[slot] = _dot_nt(qt, k_ref[0, hh])
        sm_ref[slot] = _dot_nt(qt, km_ref[hh]) + mask_ref[...]

    def finish(u, slot):
        hh, t = divmod(u, n_t)
        rows = slice(t * TQ, (t + 1) * TQ)
        cols = slice(hh * MLA_V, (hh + 1) * MLA_V)
        s = s_ref[slot]
        sm = sm_ref[slot]
        m = jnp.maximum(jnp.max(s, axis=-1, keepdims=True), jnp.max(sm, axis=-1, keepdims=True))
        p = jnp.exp2(s - m).astype(jnp.bfloat16)
        pm = jnp.exp2(sm - m).astype(jnp.bfloat16)
        ones = jnp.ones((v_ref.shape[2], MLA_V), jnp.bfloat16)
        ones_m = jnp.ones((META_PAD, MLA_V), jnp.bfloat16)
        o = (_dot(p, jnp.concatenate([v_ref[0, hh], ones], axis=-1))
             + _dot(pm, jnp.concatenate([vm_ref[hh], ones_m], axis=-1)))
        l = o[:, MLA_V:MLA_V + 1]
        y = o[:, :MLA_V] * (1.0 / l) * za_ref[rows, cols].astype(jnp.float32)
        o_ref[rows, cols] = y.astype(o_ref.dtype)

    return scores, finish


def _diff_tile_fns(qd_ref, kd_ref, vd_ref, kdm_ref, vdm_ref, btab_ref, mtab_ref, lam_ref, subln_ref, zb_ref,
                   o_ref, s_ref, sm_ref):
    n_chunks = kd_ref.shape[0] // KC
    lv = lam_ref[...]
    lam = (jnp.exp(jnp.sum(lv[0:1] * lv[1:2], axis=-1, keepdims=True))
           - jnp.exp(jnp.sum(lv[2:3] * lv[3:4], axis=-1, keepdims=True)) + LAM_INIT)

    def scores(t, slot):
        for half in range(2):
            lo = half * DIFF_QK
            qm = qd_ref[t * TQ:(t + 1) * TQ, lo:lo + DIFF_QK]
            for c in range(n_chunks):
                kc = kd_ref[c * KC:(c + 1) * KC, lo:lo + DIFF_QK]
                s_ref[slot, half, :, c * KC:(c + 1) * KC] = (
                    _dot_nt(qm, kc) + btab_ref[0, min(max(c - t, -2), 2) + 2])
            sm_ref[slot, half] = _dot_nt(qm, kdm_ref[:, lo:lo + DIFF_QK]) + mtab_ref[0, min(t, 1)]

    def softmax_parts(slot, half):
        s = s_ref[slot, half]
        sm = sm_ref[slot, half]
        m = jnp.maximum(jnp.max(s, axis=-1, keepdims=True), jnp.max(sm, axis=-1, keepdims=True))
        p = jnp.exp2(s - m)
        pm = jnp.exp2(sm - m)
        l = jnp.sum(p, axis=-1, keepdims=True) + jnp.sum(pm, axis=-1, keepdims=True)
        return p, pm, l

    def finish(t, slot):
        rows = slice(t * TQ, (t + 1) * TQ)
        p1, pm1, l1 = softmax_parts(slot, 0)
        p2, pm2, l2 = softmax_parts(slot, 1)
        r = lam * l1 * (1.0 / l2)
        a = (p1 - p2 * r).astype(jnp.bfloat16)
        am = (pm1 - pm2 * r).astype(jnp.bfloat16)
        o = (_dot(a, vd_ref[...]) + _dot(am, vdm_ref[...])) * (1.0 / l1)
        y = o * _rms_scale(o, DIFF_V) * subln_ref[...] * (1.0 - LAM_INIT)
        y = y * zb_ref[rows, :].astype(jnp.float32)
        o_ref[rows, :] = y.astype(o_ref.dtype)

    return scores, finish


def _attn_kernel(q_ref, k_ref, v_ref, km_ref, vm_ref, mask_ref, za_ref,
                 qd_ref, kd_ref, vd_ref, kdm_ref, vdm_ref, btab_ref, mtab_ref, lam_ref, subln_ref, zb_ref,
                 oa_ref, ob_ref, ms_ref, msm_ref, ds_ref, dsm_ref):
    m_scores, m_finish = _mla_tile_fns(q_ref, k_ref, v_ref, km_ref, vm_ref, mask_ref, za_ref, oa_ref,
                                       ms_ref, msm_ref)
    d_scores, d_finish = _diff_tile_fns(qd_ref, kd_ref, vd_ref, kdm_ref, vdm_ref, btab_ref, mtab_ref, lam_ref,
                                        subln_ref, zb_ref, ob_ref, ds_ref, dsm_ref)
    n_t = qd_ref.shape[0] // TQ
    n_u = n_t * MLA_PER_STEP
    d_scores(0, 0)
    m_scores(0, 0)
    for t in range(n_t):
        if t + 1 < n_t:
            d_scores(t + 1, (t + 1) % 2)
        for u in range(t * MLA_PER_STEP, (t + 1) * MLA_PER_STEP):
            if u + 1 < n_u:
                m_scores(u + 1, (u + 1) % 2)
            m_finish(u, u % 2)
        d_finish(t, t % 2)


def _attention(q, k, v, km, vm, mask, zs, qk, vd, qk_meta, vd_meta, btab, mtab, lam, subln, seq):
    nb = q.shape[0]
    cps = D_MODEL // DIFF_V
    mps = MLA_PER_STEP
    return pl.pallas_call(
        _attn_kernel,
        grid=(nb, DIFF_HEADS),
        in_specs=[
            pl.BlockSpec((1, mps, seq, HEAD_PAD), lambda b, h: (b, h, 0, 0)),
            pl.BlockSpec((1, mps, seq, HEAD_PAD), lambda b, h: (b, h, 0, 0)),
            pl.BlockSpec((1, mps, seq, MLA_V), lambda b, h: (b, h, 0, 0)),
            pl.BlockSpec((mps, META_PAD, HEAD_PAD), lambda b, h: (h, 0, 0)),
            pl.BlockSpec((mps, META_PAD, MLA_V), lambda b, h: (h, 0, 0)),
            pl.BlockSpec((1, META_PAD), lambda b, h: (0, 0)),
            pl.BlockSpec((seq, mps * MLA_V), lambda b, h: (b, h)),
            pl.BlockSpec((seq, DIFF_V), lambda b, h: (b, h)),
            pl.BlockSpec((seq, DIFF_V), lambda b, h: (b, cps + h)),
            pl.BlockSpec((seq, DIFF_V), lambda b, h: (b, h)),
            pl.BlockSpec((META_PAD, DIFF_V), lambda b, h: (0, h)),
            pl.BlockSpec((META_PAD, DIFF_V), lambda b, h: (0, h)),
            pl.BlockSpec((1, 5, TQ, KC), lambda b, h: (h, 0, 0, 0)),
            pl.BlockSpec((1, 2, TQ, META_PAD), lambda b, h: (h, 0, 0, 0)),
            pl.BlockSpec((4, DIFF_QK), lambda b, h: (0, 0)),
            pl.BlockSpec((1, DIFF_V), lambda b, h: (0, 0)),
            pl.BlockSpec((seq, DIFF_V), lambda b, h: (b, cps + h)),
        ],
        out_specs=[
            pl.BlockSpec((seq, mps * MLA_V), lambda b, h: (b, h)),
            pl.BlockSpec((seq, DIFF_V), lambda b, h: (b, h)),
        ],
        out_shape=[
            jax.ShapeDtypeStruct((nb * seq, MLA_HEADS * MLA_V), jnp.bfloat16),
            jax.ShapeDtypeStruct((nb * seq, DIFF_HEADS * DIFF_V), jnp.bfloat16),
        ],
        scratch_shapes=[
            pltpu.VMEM((2, TQ, seq), jnp.float32), pltpu.VMEM((2, TQ, META_PAD), jnp.float32),
            pltpu.VMEM((2, 2, TQ, seq), jnp.float32), pltpu.VMEM((2, 2, TQ, META_PAD), jnp.float32),
        ],
        compiler_params=pltpu.CompilerParams(
            dimension_semantics=("arbitrary", "arbitrary"), vmem_limit_bytes=VMEM_LIMIT_BYTES),
        name="attention",
    )(q, k, v, km, vm, mask, zs, qk, qk, vd, qk_meta, vd_meta, btab, mtab, lam, subln, zs)


def _merge_kernel(oa_ref, ob_ref, wa_ref, wb_ref, ga_ref, gb_ref, m_ref):
    ya = _dot(oa_ref[...], wa_ref[...])
    yb = _dot(ob_ref[...], wb_ref[...])
    m = ga_ref[...].astype(jnp.float32) * ya + gb_ref[...].astype(jnp.float32) * yb
    m_ref[...] = m.astype(m_ref.dtype)


def _merge(oa, ob, wa, wb, gs, tm, tn):
    rows = oa.shape[0]
    tps = D_MODEL // tn
    return pl.pallas_call(
        _merge_kernel,
        grid=(rows // tm, D_MODEL // tn),
        in_specs=[
            pl.BlockSpec((tm, D_MODEL), lambda i, j: (i, 0)),
            pl.BlockSpec((tm, D_MODEL), lambda i, j: (i, 0)),
            pl.BlockSpec((D_MODEL, tn), lambda i, j: (0, j)),
            pl.BlockSpec((D_MODEL, tn), lambda i, j: (0, j)),
            pl.BlockSpec((tm, tn), lambda i, j: (i, j)),
            pl.BlockSpec((tm, tn), lambda i, j: (i, tps + j)),
        ],
        out_specs=pl.BlockSpec((tm, tn), lambda i, j: (i, j)),
        out_shape=jax.ShapeDtypeStruct((rows, D_MODEL), jnp.bfloat16),
        compiler_params=pltpu.CompilerParams(
            dimension_semantics=("arbitrary", "arbitrary"), vmem_limit_bytes=VMEM_LIMIT_BYTES),
        name="merge",
    )(oa, ob, wa, wb, gs, gs)


def _out_kernel(m_ref, w_ref, x_ref, o_ref):
    o_ref[...] = x_ref[...] + _dot(m_ref[...], w_ref[...])


def _out_proj(m, wout, x2, tm, tn):
    rows = m.shape[0]
    return pl.pallas_call(
        _out_kernel,
        grid=(rows // tm, D_MODEL // tn),
        in_specs=[
            pl.BlockSpec((tm, D_MODEL), lambda i, j: (i, 0)),
            pl.BlockSpec((D_MODEL, tn), lambda i, j: (0, j)),
            pl.BlockSpec((tm, tn), lambda i, j: (i, j)),
        ],
        out_specs=pl.BlockSpec((tm, tn), lambda i, j: (i, j)),
        out_shape=jax.ShapeDtypeStruct((rows, D_MODEL), jnp.float32),
        compiler_params=pltpu.CompilerParams(
            dimension_semantics=("arbitrary", "arbitrary"), vmem_limit_bytes=VMEM_LIMIT_BYTES),
        name="out_proj",
    )(m, wout, x2)


def _t5_bucket(rel):
    nb = REL_BUCKETS // 2
    max_exact = nb // 2
    ret = jnp.where(rel > 0, nb, 0)
    n = jnp.abs(rel)
    nf = jnp.maximum(n, 1).astype(jnp.float32)
    large = max_exact + (jnp.log(nf / max_exact) / math.log(REL_MAX_DIST / max_exact)
                         * (nb - max_exact)).astype(jnp.int32)
    large = jnp.minimum(large, nb - 1)
    return ret + jnp.where(n < max_exact, n, large)


def _bias_tab_kernel(rb_ref, bk_ref, mbk_ref, btab_ref, mtab_ref):
    h = pl.program_id(0)
    bk = bk_ref[...]
    mbk = mbk_ref[...]
    acc = jnp.zeros(bk.shape, jnp.float32)
    macc = jnp.where(mbk < 0, MASK_VALUE, 0.0).astype(jnp.float32)
    for b in range(REL_BUCKETS):
        val = rb_ref[b, h] * LOG2E
        acc = jnp.where(bk == b, val, acc)
        macc = jnp.where(mbk == b, val, macc)
    btab_ref[0] = acc
    mtab_ref[0] = macc


def _bias_tables(rel_bias):
    qq = jnp.arange(TQ, dtype=jnp.int32)[:, None]
    kk = jnp.arange(KC, dtype=jnp.int32)[None, :]
    dd = jnp.arange(-2, 3, dtype=jnp.int32)[:, None, None]
    bk = _t5_bucket(dd * KC + kk[None] - qq[None])
    jm = jnp.arange(META_PAD, dtype=jnp.int32)[None, None, :]
    qpos = N_META + jnp.arange(2, dtype=jnp.int32)[:, None, None] * TQ + qq[None]
    mbk = jnp.where(jm < N_META, _t5_bucket(jm - qpos), -1)
    return pl.pallas_call(
        _bias_tab_kernel,
        grid=(DIFF_HEADS,),
        in_specs=[
            pl.BlockSpec(memory_space=pltpu.SMEM),
            pl.BlockSpec((5, TQ, KC), lambda h: (0, 0, 0)),
            pl.BlockSpec((2, TQ, META_PAD), lambda h: (0, 0, 0)),
        ],
        out_specs=[
            pl.BlockSpec((1, 5, TQ, KC), lambda h: (h, 0, 0, 0)),
            pl.BlockSpec((1, 2, TQ, META_PAD), lambda h: (h, 0, 0, 0)),
        ],
        out_shape=[
            jax.ShapeDtypeStruct((DIFF_HEADS, 5, TQ, KC), jnp.float32),
            jax.ShapeDtypeStruct((DIFF_HEADS, 2, TQ, META_PAD), jnp.float32),
        ],
        compiler_params=pltpu.CompilerParams(dimension_semantics=("arbitrary",)),
        name="bias_tables",
    )(rel_bias.astype(jnp.float32), bk, mbk)


def _rope_tables(n_pos):
    half = MLA_ROPE // 2
    inv = ROPE_THETA ** (-jnp.arange(half, dtype=jnp.float32) / half)
    ang = jnp.arange(n_pos, dtype=jnp.int32).astype(jnp.float32)[:, None] * inv[None, :]
    c, s = jnp.cos(ang), jnp.sin(ang)
    z = jnp.zeros((n_pos, LANES - MLA_ROPE), jnp.float32)
    cos_t = jnp.concatenate([c, c, z], axis=-1)
    sin_t = jnp.concatenate([-s, s, z], axis=-1)
    return cos_t, sin_t


def _pad_rows(a, rows):
    return jnp.pad(a, [(0, rows - a.shape[0])] + [(0, 0)] * (a.ndim - 1))


def kernel(x, meta_tokens, rel_bias, norm_in, w_in, q_a_norm, kv_a_norm, w_uq, w_ukv, mla_q_norm, mla_k_norm,
           diff_q_norm, diff_k_norm, diff_lambda, diff_subln, w_branch_a, w_branch_b, w_out):
    nb, seq, _ = x.shape
    bf = jnp.bfloat16
    n_small = MLA_Q_RANK + MLA_KV_RANK + MLA_ROPE

    w_in0 = w_in[0]
    wsm = jnp.pad(w_in0[:, :n_small], ((0, 0), (0, SMALL_W - n_small))).astype(bf)
    wbig = jnp.pad(w_in0, ((0, 0), (D_MODEL - n_small, 0))).astype(bf)
    wuq = jnp.pad(w_uq[0].reshape(MLA_Q_RANK, MLA_HEADS, MLA_QK),
                  ((0, 0), (0, 0), (0, HEAD_PAD - MLA_QK))).reshape(MLA_Q_RANK, MLA_HEADS * HEAD_PAD).astype(bf)
    wukv = w_ukv[0].astype(bf)
    wa, wb, wo = w_branch_a[0].astype(bf), w_branch_b[0].astype(bf), w_out[0].astype(bf)

    gin = norm_in[0][None]
    gqa, gkva = q_a_norm[0][None], kv_a_norm[0][None]
    gq = jnp.pad(mla_q_norm[0], (0, HEAD_PAD - MLA_QK))[None]
    gk = jnp.pad(mla_k_norm[0], (0, HEAD_PAD - MLA_QK))[None]
    gqd, gkd = diff_q_norm[0][None], diff_k_norm[0][None]
    subln = diff_subln[0][None]
    lam = diff_lambda[0]

    cos_t, sin_t = _rope_tables(N_META + seq)
    btab, mtab = _bias_tables(rel_bias)
    mla_mask = jnp.where(jnp.arange(META_PAD) < N_META, 0.0, MASK_VALUE).astype(jnp.float32)[None]

    x2 = x.reshape(nb * seq, D_MODEL)
    meta = meta_tokens.astype(x.dtype)

    qd_scale = LOG2E * DIFF_QK ** -0.5
    reps = D_MODEL // DIFF_QK
    gain_qk = jnp.concatenate([jnp.tile(gqd * qd_scale, (1, reps)), jnp.tile(gkd, (1, reps))], axis=-1)

    q, k, v, u = _mla_proj(x, gin, wsm, gqa, gkva, wuq, wukv, gq, gk, cos_t[N_META:], sin_t[N_META:], tm=256)
    u2 = u.reshape(nb * seq, D_MODEL)
    zs = _seg_proj(u2, wbig, (SEG_ZA, SEG_ZB), "silu", 1024, D_MODEL)
    qk = _seg_proj(u2, wbig, (SEG_QD, SEG_KD), "norm", 1024, D_MODEL, gain=gain_qk)
    vd = _seg_proj(u2, wbig, (SEG_VD,), "copy", 1024, D_MODEL)
    gs = _seg_proj(u2, wbig, (SEG_GA, SEG_GB), "sigmoid", 1024, D_MODEL)

    _, km, vm, um = _mla_proj(meta[None], gin, wsm, gqa, gkva, wuq, wukv, gq, gk,
                              cos_t[:N_META], sin_t[:N_META], tm=N_META)
    km = jnp.pad(km[0], ((0, 0), (0, META_PAD - N_META), (0, 0)))
    vm = jnp.pad(vm[0], ((0, 0), (0, META_PAD - N_META), (0, 0)))
    k_meta = _pad_rows(_seg_proj(um[0], wbig, (SEG_KD,), "norm", N_META, 1024, gain=gain_qk[:, D_MODEL:]), META_PAD)
    vd_meta = _pad_rows(_seg_proj(um[0], wbig, (SEG_VD,), "copy", N_META, 1024), META_PAD)

    oa, ob = _attention(q, k, v, km, vm, mla_mask, zs, qk, vd, k_meta, vd_meta, btab, mtab, lam, subln, seq)
    m = _merge(oa, ob, wa, wb, gs, tm=1024, tn=1024)
    out = _out_proj(m, wo, x2, tm=512, tn=D_MODEL)
    return out.reshape(nb, seq, D_MODEL)
```

```python
import functools
import math

import jax
import jax.numpy as jnp
from jax import lax
from jax.experimental import pallas as pl
from jax.experimental.pallas import tpu as pltpu

D_MODEL = 2048
N_META = 16
EPS = 1e-6

MLA_HEADS = 16
MLA_Q_RANK = 768
MLA_KV_RANK = 512
MLA_NOPE = 128
MLA_ROPE = 64
MLA_V = 128
MLA_QK = MLA_NOPE + MLA_ROPE
ROPE_THETA = 10000.0

DIFF_HEADS = 8
DIFF_QK = 128
DIFF_V = 2 * DIFF_QK
LAM_INIT = 0.8 - 0.6 * math.exp(-0.3 * 0)

REL_BUCKETS = 32
REL_MAX_DIST = 128

LANES = 128
MXU_DIM = 256
VMEM_LIMIT_BYTES = 56 * 1024 * 1024

HEAD_PAD = MXU_DIM
SMALL_W = MLA_Q_RANK + MLA_KV_RANK + LANES
META_PAD = LANES
MASK_VALUE = -1e30
TQ = 256
KC = 256
LOG2E = math.log2(math.e)

SEG_ZA, SEG_QD, SEG_KD, SEG_VD, SEG_ZB, SEG_GA, SEG_GB = range(7)


def _dot(a, b):
    return jnp.dot(a, b, preferred_element_type=jnp.float32)


def _dot_nt(a, b):
    return lax.dot_general(a, b, (((1,), (1,)), ((), ())), preferred_element_type=jnp.float32)


def _rms_scale(x, width):
    return lax.rsqrt(jnp.sum(x * x, axis=-1, keepdims=True) * (1.0 / width) + EPS)


def _rot_half64(x):
    return pltpu.roll(x, 32, 1) + pltpu.roll(x, 96, 1)


def _mla_proj_kernel(x_ref, gin_ref, wsm_ref, gqa_ref, gkva_ref, wuq_ref, wukv_ref,
                     gq_ref, gk_ref, cos_ref, sin_ref, q_ref, k_ref, v_ref, u_ref, *, q_scale):
    x = x_ref[...]
    u = (x * _rms_scale(x, D_MODEL) * gin_ref[...]).astype(jnp.bfloat16)
    u_ref[...] = u
    p = _dot_nt(u, wsm_ref[...])
    cq = p[:, :MLA_Q_RANK]
    ckv = p[:, MLA_Q_RANK:MLA_Q_RANK + MLA_KV_RANK]
    kr = p[:, MLA_Q_RANK + MLA_KV_RANK:]
    cqn = (cq * _rms_scale(cq, MLA_Q_RANK) * gqa_ref[...]).astype(jnp.bfloat16)
    ckvn = (ckv * _rms_scale(ckv, MLA_KV_RANK) * gkva_ref[...]).astype(jnp.bfloat16)
    qf = _dot(cqn, wuq_ref[...])
    kvf = _dot(ckvn, wukv_ref[...])

    cos = cos_ref[...]
    sin = sin_ref[...]
    gq = gq_ref[...]
    gk = gk_ref[...]
    gq_nope, gq_rope = gq[:, :LANES], gq[:, LANES:]
    gk_nope, gk_rope = gk[:, :LANES], gk[:, LANES:]

    ss_kr = jnp.sum(kr * kr, axis=-1, keepdims=True)
    krg = kr * gk_rope
    kr_roped = krg * cos + _rot_half64(krg) * sin

    for h in range(MLA_HEADS):
        qh = qf[:, h * HEAD_PAD:(h + 1) * HEAD_PAD]
        rq = _rms_scale(qh, MLA_QK)
        q_nope = qh[:, :LANES] * rq * gq_nope
        q_r = qh[:, LANES:] * rq * gq_rope
        q_r = q_r * cos + _rot_half64(q_r) * sin
        q_ref[0, h, :, :LANES] = (q_nope * q_scale).astype(jnp.bfloat16)
        q_ref[0, h, :, LANES:] = (q_r * q_scale).astype(jnp.bfloat16)

        k_nope = kvf[:, h * HEAD_PAD:h * HEAD_PAD + MLA_NOPE]
        rk = lax.rsqrt((jnp.sum(k_nope * k_nope, axis=-1, keepdims=True) + ss_kr) * (1.0 / MLA_QK) + EPS)
        k_ref[0, h, :, :LANES] = (k_nope * rk * gk_nope).astype(jnp.bfloat16)
        k_ref[0, h, :, LANES:] = (kr_roped * rk).astype(jnp.bfloat16)
        v_ref[0, h] = kvf[:, h * HEAD_PAD + MLA_NOPE:(h + 1) * HEAD_PAD].astype(jnp.bfloat16)


def _mla_proj(x3, gin, wsm, gqa, gkva, wuq, wukv, gq, gk, cos, sin, tm):
    nb, rows, _ = x3.shape
    const = lambda b, i: (0, 0)
    kern = functools.partial(_mla_proj_kernel, q_scale=LOG2E * MLA_QK ** -0.5)
    return pl.pallas_call(
        kern,
        grid=(nb, rows // tm),
        in_specs=[
            pl.BlockSpec((None, tm, D_MODEL), lambda b, i: (b, i, 0)),
            pl.BlockSpec((1, D_MODEL), const),
            pl.BlockSpec((SMALL_W, D_MODEL), const),
            pl.BlockSpec((1, MLA_Q_RANK), const),
            pl.BlockSpec((1, MLA_KV_RANK), const),
            pl.BlockSpec((MLA_Q_RANK, MLA_HEADS * HEAD_PAD), const),
            pl.BlockSpec((MLA_KV_RANK, MLA_HEADS * HEAD_PAD), const),
            pl.BlockSpec((1, HEAD_PAD), const),
            pl.BlockSpec((1, HEAD_PAD), const),
            pl.BlockSpec((tm, LANES), lambda b, i: (i, 0)),
            pl.BlockSpec((tm, LANES), lambda b, i: (i, 0)),
        ],
        out_specs=[
            pl.BlockSpec((1, MLA_HEADS, tm, HEAD_PAD), lambda b, i: (b, 0, i, 0)),
            pl.BlockSpec((1, MLA_HEADS, tm, HEAD_PAD), lambda b, i: (b, 0, i, 0)),
            pl.BlockSpec((1, MLA_HEADS, tm, MLA_V), lambda b, i: (b, 0, i, 0)),
            pl.BlockSpec((None, tm, D_MODEL), lambda b, i: (b, i, 0)),
        ],
        out_shape=[
            jax.ShapeDtypeStruct((nb, MLA_HEADS, rows, HEAD_PAD), jnp.bfloat16),
            jax.ShapeDtypeStruct((nb, MLA_HEADS, rows, HEAD_PAD), jnp.bfloat16),
            jax.ShapeDtypeStruct((nb, MLA_HEADS, rows, MLA_V), jnp.bfloat16),
            jax.ShapeDtypeStruct((nb, rows, D_MODEL), jnp.bfloat16),
        ],
        compiler_params=pltpu.CompilerParams(
            dimension_semantics=("arbitrary", "arbitrary"), vmem_limit_bytes=VMEM_LIMIT_BYTES),
        name="mla_proj",
    )(x3, gin, wsm, gqa, gkva, wuq, wukv, gq, gk, cos, sin)


def _seg_proj_kernel(u_ref, w_ref, *rest, kind):
    o_ref = rest[-1]
    acc = _dot_nt(u_ref[...], w_ref[...])
    if kind == "silu":
        o_ref[...] = (acc / (1.0 + jnp.exp(-acc))).astype(o_ref.dtype)
    elif kind == "sigmoid":
        o_ref[...] = (1.0 / (1.0 + jnp.exp(-acc))).astype(o_ref.dtype)
    elif kind == "copy":
        o_ref[...] = acc.astype(o_ref.dtype)
    else:
        gain = rest[0][...]
        for c in range(acc.shape[1] // DIFF_QK):
            cols = slice(c * DIFF_QK, (c + 1) * DIFF_QK)
            g = acc[:, cols]
            o_ref[:, cols] = (g * _rms_scale(g, DIFF_QK) * gain[:, cols]).astype(o_ref.dtype)


def _seg_proj(u2, wbig, segs, kind, tm, tn, gain=None):
    rows = u2.shape[0]
    tps = D_MODEL // tn
    if len(segs) == 1:
        w_map = lambda i, j: (segs[0] * tps + j, 0)
    else:
        w_map = lambda i, j: (jnp.where(j < tps, segs[0] * tps + j, segs[1] * tps + j - tps), 0)
    in_specs = [pl.BlockSpec((tm, D_MODEL), lambda i, j: (i, 0)), pl.BlockSpec((tn, D_MODEL), w_map)]
    args = [u2, wbig]
    if gain is not None:
        in_specs.append(pl.BlockSpec((1, tn), lambda i, j: (0, j)))
        args.append(gain)
    return pl.pallas_call(
        functools.partial(_seg_proj_kernel, kind=kind),
        grid=(rows // tm, len(segs) * tps),
        in_specs=in_specs,
        out_specs=pl.BlockSpec((tm, tn), lambda i, j: (i, j)),
        out_shape=jax.ShapeDtypeStruct((rows, len(segs) * D_MODEL), jnp.bfloat16),
        compiler_params=pltpu.CompilerParams(
            dimension_semantics=("arbitrary", "arbitrary"), vmem_limit_bytes=VMEM_LIMIT_BYTES),
        name="in_proj_" + kind,
    )(*args)


MLA_PER_STEP = MLA_HEADS // DIFF_HEADS


def _lane_fold_max(x):
    blocks = [x[:, i * LANES:(i + 1) * LANES] for i in range(x.shape[1] // LANES)]
    return functools.reduce(jnp.maximum, blocks)


def _mla_tile_fns(q_ref, k_ref, v_ref, km_ref, vm_ref, mask_ref, za_ref, o_ref, s_ref, sm_ref, mx_ref):
    n_t = q_ref.shape[2] // TQ

    def scores(u, slot):
        hh, t = divmod(u, n_t)
        qt = q_ref[0, hh, t * TQ:(t + 1) * TQ, :]
        s = _dot_nt(qt, k_ref[0, hh])
        sm = _dot_nt(qt, km_ref[hh]) + mask_ref[...]
        s_ref[slot] = s
        sm_ref[slot] = sm
        mx_ref[slot] = jnp.maximum(_lane_fold_max(s), sm)

    def finish(u, slot):
        hh, t = divmod(u, n_t)
        rows = slice(t * TQ, (t + 1) * TQ)
        cols = slice(hh * MLA_V, (hh + 1) * MLA_V)
        m = jnp.max(mx_ref[slot], axis=-1, keepdims=True)
        p = jnp.exp2(s_ref[slot] - m).astype(jnp.bfloat16)
        pm = jnp.exp2(sm_ref[slot] - m).astype(jnp.bfloat16)
        ones = jnp.ones((v_ref.shape[2], MLA_V), jnp.bfloat16)
        ones_m = jnp.ones((META_PAD, MLA_V), jnp.bfloat16)
        o = (_dot(p, jnp.concatenate([v_ref[0, hh], ones], axis=-1))
             + _dot(pm, jnp.concatenate([vm_ref[hh], ones_m], axis=-1)))
        l = o[:, MLA_V:MLA_V + 1]
        y = o[:, :MLA_V] * (1.0 / l) * za_ref[rows, cols].astype(jnp.float32)
        o_ref[rows, cols] = y.astype(o_ref.dtype)

    return scores, finish


def _diff_tile_fns(qd_ref, kd_ref, vd_ref, kdm_ref, vdm_ref, btab_ref, mtab_ref, lam_ref, subln_ref, zb_ref,
                   o_ref, s_ref, sm_ref, mx_ref):
    n_chunks = kd_ref.shape[0] // KC
    lv = lam_ref[...]
    lam = (jnp.exp(jnp.sum(lv[0:1] * lv[1:2], axis=-1, keepdims=True))
           - jnp.exp(jnp.sum(lv[2:3] * lv[3:4], axis=-1, keepdims=True)) + LAM_INIT)

    c_neg = btab_ref[0, 0, 0:1, 0:1]
    c_pos = btab_ref[0, 4, 0:1, 0:1]

    def scores(t, slot):
        for half in range(2):
            lo = half * DIFF_QK
            qm = qd_ref[t * TQ:(t + 1) * TQ, lo:lo + DIFF_QK]
            sm = _dot_nt(qm, kdm_ref[:, lo:lo + DIFF_QK]) + mtab_ref[0, min(t, 1)]
            sm_ref[slot, half] = sm
            mx = {"band": sm, "neg": None, "pos": None}
            for c in range(n_chunks):
                sc = _dot_nt(qm, kd_ref[c * KC:(c + 1) * KC, lo:lo + DIFF_QK])
                if abs(c - t) <= 1:
                    sc = sc + btab_ref[0, c - t + 2]
                    grp = "band"
                else:
                    grp = "neg" if c < t else "pos"
                s_ref[slot, half, :, c * KC:(c + 1) * KC] = sc
                fold = _lane_fold_max(sc)
                mx[grp] = fold if mx[grp] is None else jnp.maximum(mx[grp], fold)
            acc = mx["band"]
            if mx["neg"] is not None:
                acc = jnp.maximum(acc, mx["neg"] + c_neg)
            if mx["pos"] is not None:
                acc = jnp.maximum(acc, mx["pos"] + c_pos)
            mx_ref[slot, half] = acc

    def softmax_parts(t, slot, half):
        m = jnp.max(mx_ref[slot, half], axis=-1, keepdims=True)
        m_neg = m - c_neg
        m_pos = m - c_pos
        chunks = []
        for c in range(n_chunks):
            mc = m if abs(c - t) <= 1 else (m_neg if c < t else m_pos)
            chunks.append(jnp.exp2(s_ref[slot, half, :, c * KC:(c + 1) * KC] - mc))
        p = jnp.concatenate(chunks, axis=-1)
        pm = jnp.exp2(sm_ref[slot, half] - m)
        l = jnp.sum(p, axis=-1, keepdims=True) + jnp.sum(pm, axis=-1, keepdims=True)
        return p, pm, l

    def finish(t, slot):
        rows = slice(t * TQ, (t + 1) * TQ)
        p1, pm1, l1 = softmax_parts(t, slot, 0)
        p2, pm2, l2 = softmax_parts(t, slot, 1)
        r = lam * l1 * (1.0 / l2)
        a = (p1 - p2 * r).astype(jnp.bfloat16)
        am = (pm1 - pm2 * r).astype(jnp.bfloat16)
        o = (_dot(a, vd_ref[...]) + _dot(am, vdm_ref[...])) * (1.0 / l1)
        y = o * _rms_scale(o, DIFF_V) * subln_ref[...] * (1.0 - LAM_INIT)
        y = y * zb_ref[rows, :].astype(jnp.float32)
        o_ref[rows, :] = y.astype(o_ref.dtype)

    return scores, finish


def _attn_kernel(q_ref, k_ref, v_ref, km_ref, vm_ref, mask_ref, za_ref,
                 qd_ref, kd_ref, vd_ref, kdm_ref, vdm_ref, btab_ref, mtab_ref, lam_ref, subln_ref, zb_ref,
                 oa_ref, ob_ref, ms_ref, msm_ref, mmx_ref, ds_ref, dsm_ref, dmx_ref):
    m_scores, m_finish = _mla_tile_fns(q_ref, k_ref, v_ref, km_ref, vm_ref, mask_ref, za_ref, oa_ref,
                                       ms_ref, msm_ref, mmx_ref)
    d_scores, d_finish = _diff_tile_fns(qd_ref, kd_ref, vd_ref, kdm_ref, vdm_ref, btab_ref, mtab_ref, lam_ref,
                                        subln_ref, zb_ref, ob_ref, ds_ref, dsm_ref, dmx_ref)
    n_t = qd_ref.shape[0] // TQ
    n_u = n_t * MLA_PER_STEP
    d_scores(0, 0)
    m_scores(0, 0)
    for t in range(n_t):
        if t + 1 < n_t:
            d_scores(t + 1, (t + 1) % 2)
        for u in range(t * MLA_PER_STEP, (t + 1) * MLA_PER_STEP):
            if u + 1 < n_u:
                m_scores(u + 1, (u + 1) % 2)
            m_finish(u, u % 2)
        d_finish(t, t % 2)


def _attention(q, k, v, km, vm, mask, zs, qk, vd, qk_meta, vd_meta, btab, mtab, lam, subln, seq):
    nb = q.shape[0]
    cps = D_MODEL // DIFF_V
    mps = MLA_PER_STEP
    return pl.pallas_call(
        _attn_kernel,
        grid=(nb, DIFF_HEADS),
        in_specs=[
            pl.BlockSpec((1, mps, seq, HEAD_PAD), lambda b, h: (b, h, 0, 0)),
            pl.BlockSpec((1, mps, seq, HEAD_PAD), lambda b, h: (b, h, 0, 0)),
            pl.BlockSpec((1, mps, seq, MLA_V), lambda b, h: (b, h, 0, 0)),
            pl.BlockSpec((mps, META_PAD, HEAD_PAD), lambda b, h: (h, 0, 0)),
            pl.BlockSpec((mps, META_PAD, MLA_V), lambda b, h: (h, 0, 0)),
            pl.BlockSpec((1, META_PAD), lambda b, h: (0, 0)),
            pl.BlockSpec((seq, mps * MLA_V), lambda b, h: (b, h)),
            pl.BlockSpec((seq, DIFF_V), lambda b, h: (b, h)),
            pl.BlockSpec((seq, DIFF_V), lambda b, h: (b, cps + h)),
            pl.BlockSpec((seq, DIFF_V), lambda b, h: (b, h)),
            pl.BlockSpec((META_PAD, DIFF_V), lambda b, h: (0, h)),
            pl.BlockSpec((META_PAD, DIFF_V), lambda b, h: (0, h)),
            pl.BlockSpec((1, 5, TQ, KC), lambda b, h: (h, 0, 0, 0)),
            pl.BlockSpec((1, 2, TQ, META_PAD), lambda b, h: (h, 0, 0, 0)),
            pl.BlockSpec((4, DIFF_QK), lambda b, h: (0, 0)),
            pl.BlockSpec((1, DIFF_V), lambda b, h: (0, 0)),
            pl.BlockSpec((seq, DIFF_V), lambda b, h: (b, cps + h)),
        ],
        out_specs=[
            pl.BlockSpec((seq, mps * MLA_V), lambda b, h: (b, h)),
            pl.BlockSpec((seq, DIFF_V), lambda b, h: (b, h)),
        ],
        out_shape=[
            jax.ShapeDtypeStruct((nb * seq, MLA_HEADS * MLA_V), jnp.bfloat16),
            jax.ShapeDtypeStruct((nb * seq, DIFF_HEADS * DIFF_V), jnp.bfloat16),
        ],
        scratch_shapes=[
            pltpu.VMEM((2, TQ, seq), jnp.float32), pltpu.VMEM((2, TQ, META_PAD), jnp.float32),
            pltpu.VMEM((2, TQ, LANES), jnp.float32),
            pltpu.VMEM((2, 2, TQ, seq), jnp.float32), pltpu.VMEM((2, 2, TQ, META_PAD), jnp.float32),
            pltpu.VMEM((2, 2, TQ, LANES), jnp.float32),
        ],
        compiler_params=pltpu.CompilerParams(
            dimension_semantics=("arbitrary", "arbitrary"), vmem_limit_bytes=VMEM_LIMIT_BYTES),
        name="attention",
    )(q, k, v, km, vm, mask, zs, qk, qk, vd, qk_meta, vd_meta, btab, mtab, lam, subln, zs)


def _merge_kernel(oa_ref, ob_ref, wa_ref, wb_ref, ga_ref, gb_ref, m_ref):
    ya = _dot(oa_ref[...], wa_ref[...])
    yb = _dot(ob_ref[...], wb_ref[...])
    m = ga_ref[...].astype(jnp.float32) * ya + gb_ref[...].astype(jnp.float32) * yb
    m_ref[...] = m.astype(m_ref.dtype)


def _merge(oa, ob, wa, wb, gs, tm, tn):
    rows = oa.shape[0]
    tps = D_MODEL // tn
    return pl.pallas_call(
        _merge_kernel,
        grid=(rows // tm, D_MODEL // tn),
        in_specs=[
            pl.BlockSpec((tm, D_MODEL), lambda i, j: (i, 0)),
            pl.BlockSpec((tm, D_MODEL), lambda i, j: (i, 0)),
            pl.BlockSpec((D_MODEL, tn), lambda i, j: (0, j)),
            pl.BlockSpec((D_MODEL, tn), lambda i, j: (0, j)),
            pl.BlockSpec((tm, tn), lambda i, j: (i, j)),
            pl.BlockSpec((tm, tn), lambda i, j: (i, tps + j)),
        ],
        out_specs=pl.BlockSpec((tm, tn), lambda i, j: (i, j)),
        out_shape=jax.ShapeDtypeStruct((rows, D_MODEL), jnp.bfloat16),
        compiler_params=pltpu.CompilerParams(
            dimension_semantics=("arbitrary", "arbitrary"), vmem_limit_bytes=VMEM_LIMIT_BYTES),
        name="merge",
    )(oa, ob, wa, wb, gs, gs)


def _out_kernel(m_ref, w_ref, x_ref, o_ref):
    o_ref[...] = x_ref[...] + _dot(m_ref[...], w_ref[...])


def _out_proj(m, wout, x2, tm, tn):
    rows = m.shape[0]
    return pl.pallas_call(
        _out_kernel,
        grid=(rows // tm, D_MODEL // tn),
        in_specs=[
            pl.BlockSpec((tm, D_MODEL), lambda i, j: (i, 0)),
            pl.BlockSpec((D_MODEL, tn), lambda i, j: (0, j)),
            pl.BlockSpec((tm, tn), lambda i, j: (i, j)),
        ],
        out_specs=pl.BlockSpec((tm, tn), lambda i, j: (i, j)),
        out_shape=jax.ShapeDtypeStruct((rows, D_MODEL), jnp.float32),
        compiler_params=pltpu.CompilerParams(
            dimension_semantics=("arbitrary", "arbitrary"), vmem_limit_bytes=VMEM_LIMIT_BYTES),
        name="out_proj",
    )(m, wout, x2)


def _t5_bucket(rel):
    nb = REL_BUCKETS // 2
    max_exact = nb // 2
    ret = jnp.where(rel > 0, nb, 0)
    n = jnp.abs(rel)
    nf = jnp.maximum(n, 1).astype(jnp.float32)
    large = max_exact + (jnp.log(nf / max_exact) / math.log(REL_MAX_DIST / max_exact)
                         * (nb - max_exact)).astype(jnp.int32)
    large = jnp.minimum(large, nb - 1)
    return ret + jnp.where(n < max_exact, n, large)


def _bias_tab_kernel(rb_ref, bk_ref, mbk_ref, btab_ref, mtab_ref):
    h = pl.program_id(0)
    bk = bk_ref[...]
    mbk = mbk_ref[...]
    acc = jnp.zeros(bk.shape, jnp.float32)
    macc = jnp.where(mbk < 0, MASK_VALUE, 0.0).astype(jnp.float32)
    for b in range(REL_BUCKETS):
        val = rb_ref[b, h] * LOG2E
        acc = jnp.where(bk == b, val, acc)
        macc = jnp.where(mbk == b, val, macc)
    btab_ref[0] = acc
    mtab_ref[0] = macc


def _bias_tables(rel_bias):
    qq = jnp.arange(TQ, dtype=jnp.int32)[:, None]
    kk = jnp.arange(KC, dtype=jnp.int32)[None, :]
    dd = jnp.arange(-2, 3, dtype=jnp.int32)[:, None, None]
    bk = _t5_bucket(dd * KC + kk[None] - qq[None])
    jm = jnp.arange(META_PAD, dtype=jnp.int32)[None, None, :]
    qpos = N_META + jnp.arange(2, dtype=jnp.int32)[:, None, None] * TQ + qq[None]
    mbk = jnp.where(jm < N_META, _t5_bucket(jm - qpos), -1)
    return pl.pallas_call(
        _bias_tab_kernel,
        grid=(DIFF_HEADS,),
        in_specs=[
            pl.BlockSpec(memory_space=pltpu.SMEM),
            pl.BlockSpec((5, TQ, KC), lambda h: (0, 0, 0)),
            pl.BlockSpec((2, TQ, META_PAD), lambda h: (0, 0, 0)),
        ],
        out_specs=[
            pl.BlockSpec((1, 5, TQ, KC), lambda h: (h, 0, 0, 0)),
            pl.BlockSpec((1, 2, TQ, META_PAD), lambda h: (h, 0, 0, 0)),
        ],
        out_shape=[
            jax.ShapeDtypeStruct((DIFF_HEADS, 5, TQ, KC), jnp.float32),
            jax.ShapeDtypeStruct((DIFF_HEADS, 2, TQ, META_PAD), jnp.float32),
        ],
        compiler_params=pltpu.CompilerParams(dimension_semantics=("arbitrary",)),
        name="bias_tables",
    )(rel_bias.astype(jnp.float32), bk, mbk)


def _rope_tables(n_pos):
    half = MLA_ROPE // 2
    inv = ROPE_THETA ** (-jnp.arange(half, dtype=jnp.float32) / half)
    ang = jnp.arange(n_pos, dtype=jnp.int32).astype(jnp.float32)[:, None] * inv[None, :]
    c, s = jnp.cos(ang), jnp.sin(ang)
    z = jnp.zeros((n_pos, LANES - MLA_ROPE), jnp.float32)
    cos_t = jnp.concatenate([c, c, z], axis=-1)
    sin_t = jnp.concatenate([-s, s, z], axis=-1)
    return cos_t, sin_t


def _pad_rows(a, rows):
    return jnp.pad(a, [(0, rows - a.shape[0])] + [(0, 0)] * (a.ndim - 1))


def kernel(x, meta_tokens, rel_bias, norm_in, w_in, q_a_norm, kv_a_norm, w_uq, w_ukv, mla_q_norm, mla_k_norm,
           diff_q_norm, diff_k_norm, diff_lambda, diff_subln, w_branch_a, w_branch_b, w_out):
    nb, seq, _ = x.shape
    bf = jnp.bfloat16
    n_small = MLA_Q_RANK + MLA_KV_RANK + MLA_ROPE

    w_in_t = jnp.swapaxes(w_in[0], 0, 1)
    wsm = jnp.pad(w_in_t[:n_small], ((0, SMALL_W - n_small), (0, 0))).astype(bf)
    wbig = w_in_t[n_small:].astype(bf)
    wuq = jnp.pad(w_uq[0].reshape(MLA_Q_RANK, MLA_HEADS, MLA_QK),
                  ((0, 0), (0, 0), (0, HEAD_PAD - MLA_QK))).reshape(MLA_Q_RANK, MLA_HEADS * HEAD_PAD).astype(bf)
    wukv = w_ukv[0].astype(bf)
    wa, wb, wo = w_branch_a[0].astype(bf), w_branch_b[0].astype(bf), w_out[0].astype(bf)

    gin = norm_in[0][None]
    gqa, gkva = q_a_norm[0][None], kv_a_norm[0][None]
    gq = jnp.pad(mla_q_norm[0], (0, HEAD_PAD - MLA_QK))[None]
    gk = jnp.pad(mla_k_norm[0], (0, HEAD_PAD - MLA_QK))[None]
    gqd, gkd = diff_q_norm[0][None], diff_k_norm[0][None]
    subln = diff_subln[0][None]
    lam = diff_lambda[0]

    cos_t, sin_t = _rope_tables(N_META + seq)
    btab, mtab = _bias_tables(rel_bias)
    mla_mask = jnp.where(jnp.arange(META_PAD) < N_META, 0.0, MASK_VALUE).astype(jnp.float32)[None]

    x2 = x.reshape(nb * seq, D_MODEL)
    meta = meta_tokens.astype(x.dtype)

    qd_scale = LOG2E * DIFF_QK ** -0.5
    reps = D_MODEL // DIFF_QK
    gain_qk = jnp.concatenate([jnp.tile(gqd * qd_scale, (1, reps)), jnp.tile(gkd, (1, reps))], axis=-1)

    q, k, v, u = _mla_proj(x, gin, wsm, gqa, gkva, wuq, wukv, gq, gk, cos_t[N_META:], sin_t[N_META:], tm=256)
    u2 = u.reshape(nb * seq, D_MODEL)
    zs = _seg_proj(u2, wbig, (SEG_ZA, SEG_ZB), "silu", 1024, D_MODEL)
    qk = _seg_proj(u2, wbig, (SEG_QD, SEG_KD), "norm", 1024, D_MODEL, gain=gain_qk)
    vd = _seg_proj(u2, wbig, (SEG_VD,), "copy", 1024, D_MODEL)
    gs = _seg_proj(u2, wbig, (SEG_GA, SEG_GB), "sigmoid", 1024, D_MODEL)

    _, km, vm, um = _mla_proj(meta[None], gin, wsm, gqa, gkva, wuq, wukv, gq, gk,
                              cos_t[:N_META], sin_t[:N_META], tm=N_META)
    km = jnp.pad(km[0], ((0, 0), (0, META_PAD - N_META), (0, 0)))
    vm = jnp.pad(vm[0], ((0, 0), (0, META_PAD - N_META), (0, 0)))
    k_meta = _pad_rows(_seg_proj(um[0], wbig, (SEG_KD,), "norm", N_META, 1024, gain=gain_qk[:, D_MODEL:]), META_PAD)
    vd_meta = _pad_rows(_seg_proj(um[0], wbig, (SEG_VD,), "copy", N_META, 1024), META_PAD)

    oa, ob = _attention(q, k, v, km, vm, mla_mask, zs, qk, vd, k_meta, vd_meta, btab, mtab, lam, subln, seq)
    m = _merge(oa, ob, wa, wb, gs, tm=1024, tn=1024)
    out = _out_proj(m, wo, x2, tm=512, tn=D_MODEL)
    return out.reshape(nb, seq, D_MODEL)
```

```python
import functools
import math

import jax
import jax.numpy as jnp
from jax import lax
from jax.experimental import pallas as pl
from jax.experimental.pallas import tpu as pltpu

D_MODEL = 2048
N_META = 16
EPS = 1e-6

MLA_HEADS = 16
MLA_Q_RANK = 768
MLA_KV_RANK = 512
MLA_NOPE = 128
MLA_ROPE = 64
MLA_V = 128
MLA_QK = MLA_NOPE + MLA_ROPE
ROPE_THETA = 10000.0

DIFF_HEADS = 8
DIFF_QK = 128
DIFF_V = 2 * DIFF_QK
LAM_INIT = 0.8 - 0.6 * math.exp(-0.3 * 0)

REL_BUCKETS = 32
REL_MAX_DIST = 128

LANES = 128
MXU_DIM = 256
VMEM_LIMIT_BYTES = 62 * 1024 * 1024

HEAD_PAD = MXU_DIM
SMALL_W = MLA_Q_RANK + MLA_KV_RANK + LANES
META_PAD = LANES
MASK_VALUE = -1e30
TQ = 256
KC = 256
LOG2E = math.log2(math.e)

SEG_ZA, SEG_QD, SEG_KD, SEG_VD, SEG_ZB, SEG_GA, SEG_GB = range(7)


def _dot(a, b):
    return jnp.dot(a, b, preferred_element_type=jnp.float32)


def _dot_nt(a, b):
    return lax.dot_general(a, b, (((1,), (1,)), ((), ())), preferred_element_type=jnp.float32)


def _rms_scale(x, width):
    return lax.rsqrt(jnp.sum(x * x, axis=-1, keepdims=True) * (1.0 / width) + EPS)


def _rot_half64(x):
    return pltpu.roll(x, 32, 1) + pltpu.roll(x, 96, 1)


def _mla_proj_kernel(x_ref, gin_ref, wsm_ref, gqa_ref, gkva_ref, wuq_ref, wukv_ref,
                     gq_ref, gk_ref, cos_ref, sin_ref, q_ref, k_ref, v_ref, u_ref, *, q_scale):
    x = x_ref[...]
    u = (x * _rms_scale(x, D_MODEL) * gin_ref[...]).astype(jnp.bfloat16)
    u_ref[...] = u
    p = _dot_nt(u, wsm_ref[...])
    cq = p[:, :MLA_Q_RANK]
    ckv = p[:, MLA_Q_RANK:MLA_Q_RANK + MLA_KV_RANK]
    kr = p[:, MLA_Q_RANK + MLA_KV_RANK:]
    cqn = (cq * _rms_scale(cq, MLA_Q_RANK) * gqa_ref[...]).astype(jnp.bfloat16)
    ckvn = (ckv * _rms_scale(ckv, MLA_KV_RANK) * gkva_ref[...]).astype(jnp.bfloat16)
    qf = _dot(cqn, wuq_ref[...])
    kvf = _dot(ckvn, wukv_ref[...])

    cos = cos_ref[...]
    sin = sin_ref[...]
    gq = gq_ref[...]
    gk = gk_ref[...]
    gq_nope, gq_rope = gq[:, :LANES], gq[:, LANES:]
    gk_nope, gk_rope = gk[:, :LANES], gk[:, LANES:]

    ss_kr = jnp.sum(kr * kr, axis=-1, keepdims=True)
    krg = kr * gk_rope
    kr_roped = krg * cos + _rot_half64(krg) * sin

    for h in range(MLA_HEADS):
        qh = qf[:, h * HEAD_PAD:(h + 1) * HEAD_PAD]
        rq = _rms_scale(qh, MLA_QK)
        q_nope = qh[:, :LANES] * rq * gq_nope
        q_r = qh[:, LANES:] * rq * gq_rope
        q_r = q_r * cos + _rot_half64(q_r) * sin
        q_ref[0, h, :, :LANES] = (q_nope * q_scale).astype(jnp.bfloat16)
        q_ref[0, h, :, LANES:] = (q_r * q_scale).astype(jnp.bfloat16)

        k_nope = kvf[:, h * HEAD_PAD:h * HEAD_PAD + MLA_NOPE]
        rk = lax.rsqrt((jnp.sum(k_nope * k_nope, axis=-1, keepdims=True) + ss_kr) * (1.0 / MLA_QK) + EPS)
        k_ref[0, h, :, :LANES] = (k_nope * rk * gk_nope).astype(jnp.bfloat16)
        k_ref[0, h, :, LANES:] = (kr_roped * rk).astype(jnp.bfloat16)
        v_ref[0, h] = kvf[:, h * HEAD_PAD + MLA_NOPE:(h + 1) * HEAD_PAD].astype(jnp.bfloat16)


def _mla_proj(x3, gin, wsm, gqa, gkva, wuq, wukv, gq, gk, cos, sin, tm):
    nb, rows, _ = x3.shape
    const = lambda b, i: (0, 0)
    kern = functools.partial(_mla_proj_kernel, q_scale=LOG2E * MLA_QK ** -0.5)
    return pl.pallas_call(
        kern,
        grid=(nb, rows // tm),
        in_specs=[
            pl.BlockSpec((None, tm, D_MODEL), lambda b, i: (b, i, 0)),
            pl.BlockSpec((1, D_MODEL), const),
            pl.BlockSpec((SMALL_W, D_MODEL), const),
            pl.BlockSpec((1, MLA_Q_RANK), const),
            pl.BlockSpec((1, MLA_KV_RANK), const),
            pl.BlockSpec((MLA_Q_RANK, MLA_HEADS * HEAD_PAD), const),
            pl.BlockSpec((MLA_KV_RANK, MLA_HEADS * HEAD_PAD), const),
            pl.BlockSpec((1, HEAD_PAD), const),
            pl.BlockSpec((1, HEAD_PAD), const),
            pl.BlockSpec((tm, LANES), lambda b, i: (i, 0)),
            pl.BlockSpec((tm, LANES), lambda b, i: (i, 0)),
        ],
        out_specs=[
            pl.BlockSpec((1, MLA_HEADS, tm, HEAD_PAD), lambda b, i: (b, 0, i, 0)),
            pl.BlockSpec((1, MLA_HEADS, tm, HEAD_PAD), lambda b, i: (b, 0, i, 0)),
            pl.BlockSpec((1, MLA_HEADS, tm, MLA_V), lambda b, i: (b, 0, i, 0)),
            pl.BlockSpec((None, tm, D_MODEL), lambda b, i: (b, i, 0)),
        ],
        out_shape=[
            jax.ShapeDtypeStruct((nb, MLA_HEADS, rows, HEAD_PAD), jnp.bfloat16),
            jax.ShapeDtypeStruct((nb, MLA_HEADS, rows, HEAD_PAD), jnp.bfloat16),
            jax.ShapeDtypeStruct((nb, MLA_HEADS, rows, MLA_V), jnp.bfloat16),
            jax.ShapeDtypeStruct((nb, rows, D_MODEL), jnp.bfloat16),
        ],
        compiler_params=pltpu.CompilerParams(
            dimension_semantics=("arbitrary", "arbitrary"), vmem_limit_bytes=VMEM_LIMIT_BYTES),
        name="mla_proj",
    )(x3, gin, wsm, gqa, gkva, wuq, wukv, gq, gk, cos, sin)


def _seg_proj_kernel(u_ref, w_ref, *rest, kind):
    o_ref = rest[-1]
    acc = _dot_nt(u_ref[...], w_ref[...])
    if kind == "silu":
        o_ref[...] = (acc / (1.0 + jnp.exp(-acc))).astype(o_ref.dtype)
    elif kind == "sigmoid":
        o_ref[...] = (1.0 / (1.0 + jnp.exp(-acc))).astype(o_ref.dtype)
    elif kind == "copy":
        o_ref[...] = acc.astype(o_ref.dtype)
    else:
        gain = rest[0][...]
        for c in range(acc.shape[1] // DIFF_QK):
            cols = slice(c * DIFF_QK, (c + 1) * DIFF_QK)
            g = acc[:, cols]
            o_ref[:, cols] = (g * _rms_scale(g, DIFF_QK) * gain[:, cols]).astype(o_ref.dtype)


def _seg_proj(u2, wbig, segs, kind, tm, tn, gain=None):
    rows = u2.shape[0]
    tps = D_MODEL // tn
    if len(segs) == 1:
        w_map = lambda i, j: (segs[0] * tps + j, 0)
    else:
        w_map = lambda i, j: (jnp.where(j < tps, segs[0] * tps + j, segs[1] * tps + j - tps), 0)
    in_specs = [pl.BlockSpec((tm, D_MODEL), lambda i, j: (i, 0)), pl.BlockSpec((tn, D_MODEL), w_map)]
    args = [u2, wbig]
    if gain is not None:
        in_specs.append(pl.BlockSpec((1, tn), lambda i, j: (0, j)))
        args.append(gain)
    return pl.pallas_call(
        functools.partial(_seg_proj_kernel, kind=kind),
        grid=(rows // tm, len(segs) * tps),
        in_specs=in_specs,
        out_specs=pl.BlockSpec((tm, tn), lambda i, j: (i, j)),
        out_shape=jax.ShapeDtypeStruct((rows, len(segs) * D_MODEL), jnp.bfloat16),
        compiler_params=pltpu.CompilerParams(
            dimension_semantics=("arbitrary", "arbitrary"), vmem_limit_bytes=VMEM_LIMIT_BYTES),
        name="in_proj_" + kind,
    )(*args)


MLA_PER_STEP = MLA_HEADS // DIFF_HEADS


def _lane_fold_max(x):
    blocks = [x[:, i * LANES:(i + 1) * LANES] for i in range(x.shape[1] // LANES)]
    return functools.reduce(jnp.maximum, blocks)


def _mla_tile_fns(q_ref, k_ref, v_ref, km_ref, vm_ref, mask_ref, za_ref, qn_ref, kn_ref, kmn_ref,
                  o_ref, s_ref, sm_ref, mx_ref):
    n_t = q_ref.shape[2] // TQ

    def scores_into(slot, qt, k, km):
        s = _dot_nt(qt, k)
        sm = _dot_nt(qt, km) + mask_ref[...]
        s_ref[slot] = s
        sm_ref[slot] = sm
        mx_ref[slot] = jnp.maximum(_lane_fold_max(s), sm)

    def scores(u, slot):
        hh, t = divmod(u, n_t)
        scores_into(slot, q_ref[0, hh, t * TQ:(t + 1) * TQ, :], k_ref[0, hh], km_ref[hh])

    def scores_next(slot):
        scores_into(slot, qn_ref[0, 0], kn_ref[0, 0], kmn_ref[0])

    def finish(u, slot):
        hh, t = divmod(u, n_t)
        rows = slice(t * TQ, (t + 1) * TQ)
        cols = slice(hh * MLA_V, (hh + 1) * MLA_V)
        m = jnp.max(mx_ref[slot], axis=-1, keepdims=True)
        p = jnp.exp2(s_ref[slot] - m).astype(jnp.bfloat16)
        pm = jnp.exp2(sm_ref[slot] - m).astype(jnp.bfloat16)
        ones = jnp.ones((v_ref.shape[2], MLA_V), jnp.bfloat16)
        ones_m = jnp.ones((META_PAD, MLA_V), jnp.bfloat16)
        o = (_dot(p, jnp.concatenate([v_ref[0, hh], ones], axis=-1))
             + _dot(pm, jnp.concatenate([vm_ref[hh], ones_m], axis=-1)))
        l = o[:, MLA_V:MLA_V + 1]
        y = o[:, :MLA_V] * (1.0 / l) * za_ref[rows, cols].astype(jnp.float32)
        o_ref[rows, cols] = y.astype(o_ref.dtype)

    return scores, scores_next, finish


def _diff_tile_fns(qd_ref, kd_ref, vd_ref, kdm_ref, vdm_ref, btab_ref, mtab_ref, lam_ref, subln_ref, zb_ref,
                   qdn_ref, kdn_ref, kdmn_ref, btabn_ref, mtabn_ref, o_ref, s_ref, sm_ref, mx_ref):
    n_chunks = kd_ref.shape[0] // KC
    lv = lam_ref[...]
    lam = (jnp.exp(jnp.sum(lv[0:1] * lv[1:2], axis=-1, keepdims=True))
           - jnp.exp(jnp.sum(lv[2:3] * lv[3:4], axis=-1, keepdims=True)) + LAM_INIT)

    c_neg = btab_ref[0, 0, 0:1, 0:1]
    c_pos = btab_ref[0, 4, 0:1, 0:1]

    def scores_into(t, slot, q_tile_ref, k_ref_, km_ref_, bt_ref, mt_ref):
        cn, cp = bt_ref[0, 0, 0:1, 0:1], bt_ref[0, 4, 0:1, 0:1]
        for half in range(2):
            lo = half * DIFF_QK
            qm = q_tile_ref[:, lo:lo + DIFF_QK]
            sm = _dot_nt(qm, km_ref_[:, lo:lo + DIFF_QK]) + mt_ref[0, min(t, 1)]
            sm_ref[slot, half] = sm
            mx = {"band": sm, "neg": None, "pos": None}
            for c in range(n_chunks):
                sc = _dot_nt(qm, k_ref_[c * KC:(c + 1) * KC, lo:lo + DIFF_QK])
                if abs(c - t) <= 1:
                    sc = sc + bt_ref[0, c - t + 2]
                    grp = "band"
                else:
                    grp = "neg" if c < t else "pos"
                s_ref[slot, half, :, c * KC:(c + 1) * KC] = sc
                fold = _lane_fold_max(sc)
                mx[grp] = fold if mx[grp] is None else jnp.maximum(mx[grp], fold)
            acc = mx["band"]
            if mx["neg"] is not None:
                acc = jnp.maximum(acc, mx["neg"] + cn)
            if mx["pos"] is not None:
                acc = jnp.maximum(acc, mx["pos"] + cp)
            mx_ref[slot, half] = acc

    def scores(t, slot):
        scores_into(t, slot, qd_ref.at[t * TQ:(t + 1) * TQ], kd_ref, kdm_ref, btab_ref, mtab_ref)

    def scores_next(slot):
        scores_into(0, slot, qdn_ref, kdn_ref, kdmn_ref, btabn_ref, mtabn_ref)

    def softmax_parts(t, slot, half):
        m = jnp.max(mx_ref[slot, half], axis=-1, keepdims=True)
        m_neg = m - c_neg
        m_pos = m - c_pos
        chunks = []
        for c in range(n_chunks):
            mc = m if abs(c - t) <= 1 else (m_neg if c < t else m_pos)
            chunks.append(jnp.exp2(s_ref[slot, half, :, c * KC:(c + 1) * KC] - mc))
        p = jnp.concatenate(chunks, axis=-1)
        pm = jnp.exp2(sm_ref[slot, half] - m)
        l = jnp.sum(p, axis=-1, keepdims=True) + jnp.sum(pm, axis=-1, keepdims=True)
        return p, pm, l

    def finish(t, slot):
        rows = slice(t * TQ, (t + 1) * TQ)
        p1, pm1, l1 = softmax_parts(t, slot, 0)
        p2, pm2, l2 = softmax_parts(t, slot, 1)
        r = lam * l1 * (1.0 / l2)
        a = (p1 - p2 * r).astype(jnp.bfloat16)
        am = (pm1 - pm2 * r).astype(jnp.bfloat16)
        o = (_dot(a, vd_ref[...]) + _dot(am, vdm_ref[...])) * (1.0 / l1)
        y = o * _rms_scale(o, DIFF_V) * subln_ref[...] * (1.0 - LAM_INIT)
        y = y * zb_ref[rows, :].astype(jnp.float32)
        o_ref[rows, :] = y.astype(o_ref.dtype)

    return scores, scores_next, finish


def _attn_kernel(q_ref, k_ref, v_ref, km_ref, vm_ref, mask_ref, za_ref,
                 qd_ref, kd_ref, vd_ref, kdm_ref, vdm_ref, btab_ref, mtab_ref, lam_ref, subln_ref, zb_ref,
                 qn_ref, kn_ref, kmn_ref, qdn_ref, kdn_ref, kdmn_ref, btabn_ref, mtabn_ref,
                 oa_ref, ob_ref, ms_ref, msm_ref, mmx_ref, ds_ref, dsm_ref, dmx_ref):
    m_scores, m_scores_next, m_finish = _mla_tile_fns(
        q_ref, k_ref, v_ref, km_ref, vm_ref, mask_ref, za_ref, qn_ref, kn_ref, kmn_ref, oa_ref,
        ms_ref, msm_ref, mmx_ref)
    d_scores, d_scores_next, d_finish = _diff_tile_fns(
        qd_ref, kd_ref, vd_ref, kdm_ref, vdm_ref, btab_ref, mtab_ref, lam_ref, subln_ref, zb_ref,
        qdn_ref, kdn_ref, kdmn_ref, btabn_ref, mtabn_ref, ob_ref, ds_ref, dsm_ref, dmx_ref)
    n_t = qd_ref.shape[0] // TQ
    n_u = n_t * MLA_PER_STEP

    @pl.when((pl.program_id(0) == 0) & (pl.program_id(1) == 0))
    def _():
        d_scores(0, 0)
        m_scores(0, 0)

    for t in range(n_t):
        if t + 1 < n_t:
            d_scores(t + 1, (t + 1) % 2)
        else:
            d_scores_next(0)
        for u in range(t * MLA_PER_STEP, (t + 1) * MLA_PER_STEP):
            if u + 1 < n_u:
                m_scores(u + 1, (u + 1) % 2)
            else:
                m_scores_next(0)
            m_finish(u, u % 2)
        d_finish(t, t % 2)


def _attention(q, k, v, km, vm, mask, zs, qk, vd, qk_meta, vd_meta, btab, mtab, lam, subln, seq):
    nb = q.shape[0]
    cps = D_MODEL // DIFF_V
    mps = MLA_PER_STEP
    n_steps = nb * DIFF_HEADS

    def nxt(b, h):
        g = jnp.minimum(b * DIFF_HEADS + h + 1, n_steps - 1)
        return g // DIFF_HEADS, g % DIFF_HEADS

    def nb_(b, h):
        return nxt(b, h)[0]

    def nh_(b, h):
        return nxt(b, h)[1]

    return pl.pallas_call(
        _attn_kernel,
        grid=(nb, DIFF_HEADS),
        in_specs=[
            pl.BlockSpec((1, mps, seq, HEAD_PAD), lambda b, h: (b, h, 0, 0)),
            pl.BlockSpec((1, mps, seq, HEAD_PAD), lambda b, h: (b, h, 0, 0)),
            pl.BlockSpec((1, mps, seq, MLA_V), lambda b, h: (b, h, 0, 0)),
            pl.BlockSpec((mps, META_PAD, HEAD_PAD), lambda b, h: (h, 0, 0)),
            pl.BlockSpec((mps, META_PAD, MLA_V), lambda b, h: (h, 0, 0)),
            pl.BlockSpec((1, META_PAD), lambda b, h: (0, 0)),
            pl.BlockSpec((seq, mps * MLA_V), lambda b, h: (b, h)),
            pl.BlockSpec((seq, DIFF_V), lambda b, h: (b, h)),
            pl.BlockSpec((seq, DIFF_V), lambda b, h: (b, cps + h)),
            pl.BlockSpec((seq, DIFF_V), lambda b, h: (b, h)),
            pl.BlockSpec((META_PAD, DIFF_V), lambda b, h: (0, h)),
            pl.BlockSpec((META_PAD, DIFF_V), lambda b, h: (0, h)),
            pl.BlockSpec((1, 5, TQ, KC), lambda b, h: (h, 0, 0, 0)),
            pl.BlockSpec((1, 2, TQ, META_PAD), lambda b, h: (h, 0, 0, 0)),
            pl.BlockSpec((4, DIFF_QK), lambda b, h: (0, 0)),
            pl.BlockSpec((1, DIFF_V), lambda b, h: (0, 0)),
            pl.BlockSpec((seq, DIFF_V), lambda b, h: (b, cps + h)),
            pl.BlockSpec((1, 1, TQ, HEAD_PAD), lambda b, h: (nb_(b, h), mps * nh_(b, h), 0, 0)),
            pl.BlockSpec((1, 1, seq, HEAD_PAD), lambda b, h: (nb_(b, h), mps * nh_(b, h), 0, 0)),
            pl.BlockSpec((1, META_PAD, HEAD_PAD), lambda b, h: (mps * nh_(b, h), 0, 0)),
            pl.BlockSpec((TQ, DIFF_V), lambda b, h: (nb_(b, h) * (seq // TQ), nh_(b, h))),
            pl.BlockSpec((seq, DIFF_V), lambda b, h: (nb_(b, h), cps + nh_(b, h))),
            pl.BlockSpec((META_PAD, DIFF_V), lambda b, h: (0, nh_(b, h))),
            pl.BlockSpec((1, 5, TQ, KC), lambda b, h: (nh_(b, h), 0, 0, 0)),
            pl.BlockSpec((1, 2, TQ, META_PAD), lambda b, h: (nh_(b, h), 0, 0, 0)),
        ],
        out_specs=[
            pl.BlockSpec((seq, mps * MLA_V), lambda b, h: (b, h)),
            pl.BlockSpec((seq, DIFF_V), lambda b, h: (b, h)),
        ],
        out_shape=[
            jax.ShapeDtypeStruct((nb * seq, MLA_HEADS * MLA_V), jnp.bfloat16),
            jax.ShapeDtypeStruct((nb * seq, DIFF_HEADS * DIFF_V), jnp.bfloat16),
        ],
        scratch_shapes=[
            pltpu.VMEM((2, TQ, seq), jnp.float32), pltpu.VMEM((2, TQ, META_PAD), jnp.float32),
            pltpu.VMEM((2, TQ, LANES), jnp.float32),
            pltpu.VMEM((2, 2, TQ, seq), jnp.float32), pltpu.VMEM((2, 2, TQ, META_PAD), jnp.float32),
            pltpu.VMEM((2, 2, TQ, LANES), jnp.float32),
        ],
        compiler_params=pltpu.CompilerParams(
            dimension_semantics=("arbitrary", "arbitrary"), vmem_limit_bytes=VMEM_LIMIT_BYTES),
        name="attention",
    )(q, k, v, km, vm, mask, zs, qk, qk, vd, qk_meta, vd_meta, btab, mtab, lam, subln, zs,
      q, k, km, qk, qk, qk_meta, btab, mtab)


def _merge_kernel(oa_ref, ob_ref, wa_ref, wb_ref, ga_ref, gb_ref, m_ref):
    ya = _dot(oa_ref[...], wa_ref[...])
    yb = _dot(ob_ref[...], wb_ref[...])
    m = ga_ref[...].astype(jnp.float32) * ya + gb_ref[...].astype(jnp.float32) * yb
    m_ref[...] = m.astype(m_ref.dtype)


def _merge(oa, ob, wa, wb, gs, tm, tn):
    rows = oa.shape[0]
    tps = D_MODEL // tn
    return pl.pallas_call(
        _merge_kernel,
        grid=(rows // tm, D_MODEL // tn),
        in_specs=[
            pl.BlockSpec((tm, D_MODEL), lambda i, j: (i, 0)),
            pl.BlockSpec((tm, D_MODEL), lambda i, j: (i, 0)),
            pl.BlockSpec((D_MODEL, tn), lambda i, j: (0, j)),
            pl.BlockSpec((D_MODEL, tn), lambda i, j: (0, j)),
            pl.BlockSpec((tm, tn), lambda i, j: (i, j)),
            pl.BlockSpec((tm, tn), lambda i, j: (i, tps + j)),
        ],
        out_specs=pl.BlockSpec((tm, tn), lambda i, j: (i, j)),
        out_shape=jax.ShapeDtypeStruct((rows, D_MODEL), jnp.bfloat16),
        compiler_params=pltpu.CompilerParams(
            dimension_semantics=("arbitrary", "arbitrary"), vmem_limit_bytes=VMEM_LIMIT_BYTES),
        name="merge",
    )(oa, ob, wa, wb, gs, gs)


def _out_kernel(m_ref, w_ref, x_ref, o_ref):
    o_ref[...] = x_ref[...] + _dot(m_ref[...], w_ref[...])


def _out_proj(m, wout, x2, tm, tn):
    rows = m.shape[0]
    return pl.pallas_call(
        _out_kernel,
        grid=(rows // tm, D_MODEL // tn),
        in_specs=[
            pl.BlockSpec((tm, D_MODEL), lambda i, j: (i, 0)),
            pl.BlockSpec((D_MODEL, tn), lambda i, j: (0, j)),
            pl.BlockSpec((tm, tn), lambda i, j: (i, j)),
        ],
        out_specs=pl.BlockSpec((tm, tn), lambda i, j: (i, j)),
        out_shape=jax.ShapeDtypeStruct((rows, D_MODEL), jnp.float32),
        compiler_params=pltpu.CompilerParams(
            dimension_semantics=("arbitrary", "arbitrary"), vmem_limit_bytes=VMEM_LIMIT_BYTES),
        name="out_proj",
    )(m, wout, x2)


def _t5_bucket(rel):
    nb = REL_BUCKETS // 2
    max_exact = nb // 2
    ret = jnp.where(rel > 0, nb, 0)
    n = jnp.abs(rel)
    nf = jnp.maximum(n, 1).astype(jnp.float32)
    large = max_exact + (jnp.log(nf / max_exact) / math.log(REL_MAX_DIST / max_exact)
                         * (nb - max_exact)).astype(jnp.int32)
    large = jnp.minimum(large, nb - 1)
    return ret + jnp.where(n < max_exact, n, large)


def _bias_tab_kernel(rb_ref, bk_ref, mbk_ref, btab_ref, mtab_ref):
    h = pl.program_id(0)
    bk = bk_ref[...]
    mbk = mbk_ref[...]
    acc = jnp.zeros(bk.shape, jnp.float32)
    macc = jnp.where(mbk < 0, MASK_VALUE, 0.0).astype(jnp.float32)
    for b in range(REL_BUCKETS):
        val = rb_ref[b, h] * LOG2E
        acc = jnp.where(bk == b, val, acc)
        macc = jnp.where(mbk == b, val, macc)
    btab_ref[0] = acc
    mtab_ref[0] = macc


def _bias_tables(rel_bias):
    qq = jnp.arange(TQ, dtype=jnp.int32)[:, None]
    kk = jnp.arange(KC, dtype=jnp.int32)[None, :]
    dd = jnp.arange(-2, 3, dtype=jnp.int32)[:, None, None]
    bk = _t5_bucket(dd * KC + kk[None] - qq[None])
    jm = jnp.arange(META_PAD, dtype=jnp.int32)[None, None, :]
    qpos = N_META + jnp.arange(2, dtype=jnp.int32)[:, None, None] * TQ + qq[None]
    mbk = jnp.where(jm < N_META, _t5_bucket(jm - qpos), -1)
    return pl.pallas_call(
        _bias_tab_kernel,
        grid=(DIFF_HEADS,),
        in_specs=[
            pl.BlockSpec(memory_space=pltpu.SMEM),
            pl.BlockSpec((5, TQ, KC), lambda h: (0, 0, 0)),
            pl.BlockSpec((2, TQ, META_PAD), lambda h: (0, 0, 0)),
        ],
        out_specs=[
            pl.BlockSpec((1, 5, TQ, KC), lambda h: (h, 0, 0, 0)),
            pl.BlockSpec((1, 2, TQ, META_PAD), lambda h: (h, 0, 0, 0)),
        ],
        out_shape=[
            jax.ShapeDtypeStruct((DIFF_HEADS, 5, TQ, KC), jnp.float32),
            jax.ShapeDtypeStruct((DIFF_HEADS, 2, TQ, META_PAD), jnp.float32),
        ],
        compiler_params=pltpu.CompilerParams(dimension_semantics=("arbitrary",)),
        name="bias_tables",
    )(rel_bias.astype(jnp.float32), bk, mbk)


def _rope_tables(n_pos):
    half = MLA_ROPE // 2
    inv = ROPE_THETA ** (-jnp.arange(half, dtype=jnp.float32) / half)
    ang = jnp.arange(n_pos, dtype=jnp.int32).astype(jnp.float32)[:, None] * inv[None, :]
    c, s = jnp.cos(ang), jnp.sin(ang)
    z = jnp.zeros((n_pos, LANES - MLA_ROPE), jnp.float32)
    cos_t = jnp.concatenate([c, c, z], axis=-1)
    sin_t = jnp.concatenate([-s, s, z], axis=-1)
    return cos_t, sin_t


def _pad_rows(a, rows):
    return jnp.pad(a, [(0, rows - a.shape[0])] + [(0, 0)] * (a.ndim - 1))


def kernel(x, meta_tokens, rel_bias, norm_in, w_in, q_a_norm, kv_a_norm, w_uq, w_ukv, mla_q_norm, mla_k_norm,
           diff_q_norm, diff_k_norm, diff_lambda, diff_subln, w_branch_a, w_branch_b, w_out):
    nb, seq, _ = x.shape
    bf = jnp.bfloat16
    n_small = MLA_Q_RANK + MLA_KV_RANK + MLA_ROPE

    w_in_t = jnp.swapaxes(w_in[0], 0, 1)
    wsm = jnp.pad(w_in_t[:n_small], ((0, SMALL_W - n_small), (0, 0))).astype(bf)
    wbig = w_in_t[n_small:].astype(bf)
    wuq = jnp.pad(w_uq[0].reshape(MLA_Q_RANK, MLA_HEADS, MLA_QK),
                  ((0, 0), (0, 0), (0, HEAD_PAD - MLA_QK))).reshape(MLA_Q_RANK, MLA_HEADS * HEAD_PAD).astype(bf)
    wukv = w_ukv[0].astype(bf)
    wa, wb, wo = w_branch_a[0].astype(bf), w_branch_b[0].astype(bf), w_out[0].astype(bf)

    gin = norm_in[0][None]
    gqa, gkva = q_a_norm[0][None], kv_a_norm[0][None]
    gq = jnp.pad(mla_q_norm[0], (0, HEAD_PAD - MLA_QK))[None]
    gk = jnp.pad(mla_k_norm[0], (0, HEAD_PAD - MLA_QK))[None]
    gqd, gkd = diff_q_norm[0][None], diff_k_norm[0][None]
    subln = diff_subln[0][None]
    lam = diff_lambda[0]

    cos_t, sin_t = _rope_tables(N_META + seq)
    btab, mtab = _bias_tables(rel_bias)
    mla_mask = jnp.where(jnp.arange(META_PAD) < N_META, 0.0, MASK_VALUE).astype(jnp.float32)[None]

    x2 = x.reshape(nb * seq, D_MODEL)
    meta = meta_tokens.astype(x.dtype)

    qd_scale = LOG2E * DIFF_QK ** -0.5
    reps = D_MODEL // DIFF_QK
    gain_qk = jnp.concatenate([jnp.tile(gqd * qd_scale, (1, reps)), jnp.tile(gkd, (1, reps))], axis=-1)

    q, k, v, u = _mla_proj(x, gin, wsm, gqa, gkva, wuq, wukv, gq, gk, cos_t[N_META:], sin_t[N_META:], tm=256)
    u2 = u.reshape(nb * seq, D_MODEL)
    zs = _seg_proj(u2, wbig, (SEG_ZA, SEG_ZB), "silu", 1024, D_MODEL)
    qk = _seg_proj(u2, wbig, (SEG_QD, SEG_KD), "norm", 1024, D_MODEL, gain=gain_qk)
    vd = _seg_proj(u2, wbig, (SEG_VD,), "copy", 1024, D_MODEL)
    gs = _seg_proj(u2, wbig, (SEG_GA, SEG_GB), "sigmoid", 1024, D_MODEL)

    _, km, vm, um = _mla_proj(meta[None], gin, wsm, gqa, gkva, wuq, wukv, gq, gk,
                              cos_t[:N_META], sin_t[:N_META], tm=N_META)
    km = jnp.pad(km[0], ((0, 0), (0, META_PAD - N_META), (0, 0)))
    vm = jnp.pad(vm[0], ((0, 0), (0, META_PAD - N_META), (0, 0)))
    k_meta = _pad_rows(_seg_proj(um[0], wbig, (SEG_KD,), "norm", N_META, 1024, gain=gain_qk[:, D_MODEL:]), META_PAD)
    vd_meta = _pad_rows(_seg_proj(um[0], wbig, (SEG_VD,), "copy", N_META, 1024), META_PAD)

    oa, ob = _attention(q, k, v, km, vm, mla_mask, zs, qk, vd, k_meta, vd_meta, btab, mtab, lam, subln, seq)
    m = _merge(oa, ob, wa, wb, gs, tm=1024, tn=1024)
    out = _out_proj(m, wo, x2, tm=512, tn=D_MODEL)
    return out.reshape(nb, seq, D_MODEL)
```

```python
import functools
import math

import jax
import jax.numpy as jnp
from jax import lax
from jax.experimental import pallas as pl
from jax.experimental.pallas import tpu as pltpu

D_MODEL = 2048
N_META = 16
EPS = 1e-6

MLA_HEADS = 16
MLA_Q_RANK = 768
MLA_KV_RANK = 512
MLA_NOPE = 128
MLA_ROPE = 64
MLA_V = 128
MLA_QK = MLA_NOPE + MLA_ROPE
ROPE_THETA = 10000.0

DIFF_HEADS = 8
DIFF_QK = 128
DIFF_V = 2 * DIFF_QK
LAM_INIT = 0.8 - 0.6 * math.exp(-0.3 * 0)

REL_BUCKETS = 32
REL_MAX_DIST = 128

LANES = 128
MXU_DIM = 256
VMEM_LIMIT_BYTES = 62 * 1024 * 1024

HEAD_PAD = MXU_DIM
SMALL_W = MLA_Q_RANK + MLA_KV_RANK + LANES
META_PAD = LANES
MASK_VALUE = -1e30
TQ = 256
KC = 256
LOG2E = math.log2(math.e)

SEG_ZA, SEG_QD, SEG_KD, SEG_VD, SEG_ZB, SEG_GA, SEG_GB = range(7)


def _dot(a, b):
    return jnp.dot(a, b, preferred_element_type=jnp.float32)


def _dot_nt(a, b):
    return lax.dot_general(a, b, (((1,), (1,)), ((), ())), preferred_element_type=jnp.float32)


def _rms_scale(x, width):
    return lax.rsqrt(jnp.sum(x * x, axis=-1, keepdims=True) * (1.0 / width) + EPS)


def _rot_half64(x):
    return pltpu.roll(x, 32, 1) + pltpu.roll(x, 96, 1)


def _mla_proj_kernel(x_ref, gin_ref, wsm_ref, gqa_ref, gkva_ref, wuq_ref, wukv_ref,
                     gq_ref, gk_ref, cos_ref, sin_ref, q_ref, k_ref, v_ref, u_ref, *, q_scale):
    x = x_ref[...]
    u = (x * _rms_scale(x, D_MODEL) * gin_ref[...]).astype(jnp.bfloat16)
    u_ref[...] = u
    p = _dot_nt(u, wsm_ref[...])
    cq = p[:, :MLA_Q_RANK]
    ckv = p[:, MLA_Q_RANK:MLA_Q_RANK + MLA_KV_RANK]
    kr = p[:, MLA_Q_RANK + MLA_KV_RANK:]
    cqn = (cq * _rms_scale(cq, MLA_Q_RANK) * gqa_ref[...]).astype(jnp.bfloat16)
    ckvn = (ckv * _rms_scale(ckv, MLA_KV_RANK) * gkva_ref[...]).astype(jnp.bfloat16)
    qf = _dot(cqn, wuq_ref[...])
    kvf = _dot(ckvn, wukv_ref[...])

    cos = cos_ref[...]
    sin = sin_ref[...]
    gq = gq_ref[...]
    gk = gk_ref[...]
    gq_nope, gq_rope = gq[:, :LANES], gq[:, LANES:]
    gk_nope, gk_rope = gk[:, :LANES], gk[:, LANES:]

    ss_kr = jnp.sum(kr * kr, axis=-1, keepdims=True)
    krg = kr * gk_rope
    kr_roped = krg * cos + _rot_half64(krg) * sin

    for h in range(MLA_HEADS):
        qh = qf[:, h * HEAD_PAD:(h + 1) * HEAD_PAD]
        rq = _rms_scale(qh, MLA_QK)
        q_nope = qh[:, :LANES] * rq * gq_nope
        q_r = qh[:, LANES:] * rq * gq_rope
        q_r = q_r * cos + _rot_half64(q_r) * sin
        q_ref[0, h, :, :LANES] = (q_nope * q_scale).astype(jnp.bfloat16)
        q_ref[0, h, :, LANES:] = (q_r * q_scale).astype(jnp.bfloat16)

        k_nope = kvf[:, h * HEAD_PAD:h * HEAD_PAD + MLA_NOPE]
        rk = lax.rsqrt((jnp.sum(k_nope * k_nope, axis=-1, keepdims=True) + ss_kr) * (1.0 / MLA_QK) + EPS)
        k_ref[0, h, :, :LANES] = (k_nope * rk * gk_nope).astype(jnp.bfloat16)
        k_ref[0, h, :, LANES:] = (kr_roped * rk).astype(jnp.bfloat16)
        v_ref[0, h] = kvf[:, h * HEAD_PAD + MLA_NOPE:(h + 1) * HEAD_PAD].T.astype(jnp.bfloat16)


def _mla_proj(x3, gin, wsm, gqa, gkva, wuq, wukv, gq, gk, cos, sin, tm):
    nb, rows, _ = x3.shape
    const = lambda b, i: (0, 0)
    kern = functools.partial(_mla_proj_kernel, q_scale=LOG2E * MLA_QK ** -0.5)
    return pl.pallas_call(
        kern,
        grid=(nb, rows // tm),
        in_specs=[
            pl.BlockSpec((None, tm, D_MODEL), lambda b, i: (b, i, 0)),
            pl.BlockSpec((1, D_MODEL), const),
            pl.BlockSpec((SMALL_W, D_MODEL), const),
            pl.BlockSpec((1, MLA_Q_RANK), const),
            pl.BlockSpec((1, MLA_KV_RANK), const),
            pl.BlockSpec((MLA_Q_RANK, MLA_HEADS * HEAD_PAD), const),
            pl.BlockSpec((MLA_KV_RANK, MLA_HEADS * HEAD_PAD), const),
            pl.BlockSpec((1, HEAD_PAD), const),
            pl.BlockSpec((1, HEAD_PAD), const),
            pl.BlockSpec((tm, LANES), lambda b, i: (i, 0)),
            pl.BlockSpec((tm, LANES), lambda b, i: (i, 0)),
        ],
        out_specs=[
            pl.BlockSpec((1, MLA_HEADS, tm, HEAD_PAD), lambda b, i: (b, 0, i, 0)),
            pl.BlockSpec((1, MLA_HEADS, tm, HEAD_PAD), lambda b, i: (b, 0, i, 0)),
            pl.BlockSpec((1, MLA_HEADS, MLA_V, tm), lambda b, i: (b, 0, 0, i)),
            pl.BlockSpec((None, tm, D_MODEL), lambda b, i: (b, i, 0)),
        ],
        out_shape=[
            jax.ShapeDtypeStruct((nb, MLA_HEADS, rows, HEAD_PAD), jnp.bfloat16),
            jax.ShapeDtypeStruct((nb, MLA_HEADS, rows, HEAD_PAD), jnp.bfloat16),
            jax.ShapeDtypeStruct((nb, MLA_HEADS, MLA_V, rows), jnp.bfloat16),
            jax.ShapeDtypeStruct((nb, rows, D_MODEL), jnp.bfloat16),
        ],
        compiler_params=pltpu.CompilerParams(
            dimension_semantics=("arbitrary", "arbitrary"), vmem_limit_bytes=VMEM_LIMIT_BYTES),
        name="mla_proj",
    )(x3, gin, wsm, gqa, gkva, wuq, wukv, gq, gk, cos, sin)


def _seg_proj_kernel(u_ref, w_ref, *rest, kind):
    o_ref = rest[-1]
    acc = _dot_nt(u_ref[...], w_ref[...])
    if kind == "silu":
        o_ref[...] = (acc / (1.0 + jnp.exp(-acc))).astype(o_ref.dtype)
    elif kind == "sigmoid":
        o_ref[...] = (1.0 / (1.0 + jnp.exp(-acc))).astype(o_ref.dtype)
    elif kind == "copy":
        o_ref[...] = acc.astype(o_ref.dtype)
    else:
        gain = rest[0][...]
        for c in range(acc.shape[1] // DIFF_QK):
            cols = slice(c * DIFF_QK, (c + 1) * DIFF_QK)
            g = acc[:, cols]
            o_ref[:, cols] = (g * _rms_scale(g, DIFF_QK) * gain[:, cols]).astype(o_ref.dtype)


def _seg_proj(u2, wbig, segs, kind, tm, tn, gain=None):
    rows = u2.shape[0]
    tps = D_MODEL // tn
    if len(segs) == 1:
        w_map = lambda i, j: (segs[0] * tps + j, 0)
    else:
        w_map = lambda i, j: (jnp.where(j < tps, segs[0] * tps + j, segs[1] * tps + j - tps), 0)
    in_specs = [pl.BlockSpec((tm, D_MODEL), lambda i, j: (i, 0)), pl.BlockSpec((tn, D_MODEL), w_map)]
    args = [u2, wbig]
    if gain is not None:
        in_specs.append(pl.BlockSpec((1, tn), lambda i, j: (0, j)))
        args.append(gain)
    return pl.pallas_call(
        functools.partial(_seg_proj_kernel, kind=kind),
        grid=(rows // tm, len(segs) * tps),
        in_specs=in_specs,
        out_specs=pl.BlockSpec((tm, tn), lambda i, j: (i, j)),
        out_shape=jax.ShapeDtypeStruct((rows, len(segs) * D_MODEL), jnp.bfloat16),
        compiler_params=pltpu.CompilerParams(
            dimension_semantics=("arbitrary", "arbitrary"), vmem_limit_bytes=VMEM_LIMIT_BYTES),
        name="in_proj_" + kind,
    )(*args)


MLA_PER_STEP = MLA_HEADS // DIFF_HEADS


def _lane_fold_max(x):
    blocks = [x[:, i * LANES:(i + 1) * LANES] for i in range(x.shape[1] // LANES)]
    return functools.reduce(jnp.maximum, blocks)


ONES_ROWS = 16


def _mla_tile_fns(q_ref, k_ref, vt_ref, km_ref, vmt_ref, za_ref, qn_ref, kn_ref, kmn_ref,
                  o_ref, s_ref, sm_ref, mx_ref):
    n_t = q_ref.shape[2] // TQ

    def scores_into(slot, qt, k, km):
        s = _dot_nt(k, qt)
        sm = _dot_nt(km, qt)
        key = lax.broadcasted_iota(jnp.int32, sm.shape, 0)
        sm = jnp.where(key < N_META, sm, MASK_VALUE)
        s_ref[slot] = s
        sm_ref[slot] = sm
        mx_ref[slot] = jnp.maximum(jnp.max(s, axis=0, keepdims=True), jnp.max(sm, axis=0, keepdims=True))

    def scores(u, slot):
        hh, t = divmod(u, n_t)
        scores_into(slot, q_ref[0, hh, t * TQ:(t + 1) * TQ, :], k_ref[0, hh], km_ref[hh])

    def scores_next(slot):
        scores_into(slot, qn_ref[0, 0], kn_ref[0, 0], kmn_ref[0])

    def finish(u, slot):
        hh, t = divmod(u, n_t)
        rows = slice(t * TQ, (t + 1) * TQ)
        cols = slice(hh * MLA_V, (hh + 1) * MLA_V)
        m = mx_ref[slot]
        p = jnp.exp2(s_ref[slot] - m).astype(jnp.bfloat16)
        pm = jnp.exp2(sm_ref[slot] - m).astype(jnp.bfloat16)
        ones = jnp.ones((ONES_ROWS, vt_ref.shape[3]), jnp.bfloat16)
        ones_m = jnp.ones((ONES_ROWS, META_PAD), jnp.bfloat16)
        o = (_dot(jnp.concatenate([vt_ref[0, hh], ones], axis=0), p)
             + _dot(jnp.concatenate([vmt_ref[hh], ones_m], axis=0), pm))
        l = o[MLA_V:MLA_V + 1, :]
        y = (o[:MLA_V, :] * (1.0 / l)).T
        o_ref[rows, cols] = (y * za_ref[rows, cols].astype(jnp.float32)).astype(o_ref.dtype)

    return scores, scores_next, finish


def _diff_tile_fns(qd_ref, kd_ref, vd_ref, kdm_ref, vdm_ref, btab_ref, mtab_ref, lam_ref, subln_ref, zb_ref,
                   qdn_ref, kdn_ref, kdmn_ref, btabn_ref, mtabn_ref, o_ref, s_ref, sm_ref, mx_ref):
    n_chunks = kd_ref.shape[0] // KC
    lv = lam_ref[...]
    lam = (jnp.exp(jnp.sum(lv[0:1] * lv[1:2], axis=-1, keepdims=True))
           - jnp.exp(jnp.sum(lv[2:3] * lv[3:4], axis=-1, keepdims=True)) + LAM_INIT)

    c_neg = btab_ref[0, 0, 0:1, 0:1]
    c_pos = btab_ref[0, 4, 0:1, 0:1]

    def scores_into(t, slot, q_tile_ref, k_ref_, km_ref_, bt_ref, mt_ref):
        cn, cp = bt_ref[0, 0, 0:1, 0:1], bt_ref[0, 4, 0:1, 0:1]
        for half in range(2):
            lo = half * DIFF_QK
            qm = q_tile_ref[:, lo:lo + DIFF_QK]
            sm = _dot_nt(qm, km_ref_[:, lo:lo + DIFF_QK]) + mt_ref[0, min(t, 1)]
            sm_ref[slot, half] = sm
            mx = {"band": sm, "neg": None, "pos": None}
            for c in range(n_chunks):
                sc = _dot_nt(qm, k_ref_[c * KC:(c + 1) * KC, lo:lo + DIFF_QK])
                if abs(c - t) <= 1:
                    sc = sc + bt_ref[0, c - t + 2]
                    grp = "band"
                else:
                    grp = "neg" if c < t else "pos"
                s_ref[slot, half, :, c * KC:(c + 1) * KC] = sc
                fold = _lane_fold_max(sc)
                mx[grp] = fold if mx[grp] is None else jnp.maximum(mx[grp], fold)
            acc = mx["band"]
            if mx["neg"] is not None:
                acc = jnp.maximum(acc, mx["neg"] + cn)
            if mx["pos"] is not None:
                acc = jnp.maximum(acc, mx["pos"] + cp)
            mx_ref[slot, half] = acc

    def scores(t, slot):
        scores_into(t, slot, qd_ref.at[t * TQ:(t + 1) * TQ], kd_ref, kdm_ref, btab_ref, mtab_ref)

    def scores_next(slot):
        scores_into(0, slot, qdn_ref, kdn_ref, kdmn_ref, btabn_ref, mtabn_ref)

    def softmax_parts(t, slot, half):
        m = jnp.max(mx_ref[slot, half], axis=-1, keepdims=True)
        m_neg = m - c_neg
        m_pos = m - c_pos
        chunks = []
        for c in range(n_chunks):
            mc = m if abs(c - t) <= 1 else (m_neg if c < t else m_pos)
            chunks.append(jnp.exp2(s_ref[slot, half, :, c * KC:(c + 1) * KC] - mc))
        p = jnp.concatenate(chunks, axis=-1)
        pm = jnp.exp2(sm_ref[slot, half] - m)
        l = jnp.sum(p, axis=-1, keepdims=True) + jnp.sum(pm, axis=-1, keepdims=True)
        return p, pm, l

    def finish(t, slot):
        rows = slice(t * TQ, (t + 1) * TQ)
        p1, pm1, l1 = softmax_parts(t, slot, 0)
        p2, pm2, l2 = softmax_parts(t, slot, 1)
        r = lam * l1 * (1.0 / l2)
        a = (p1 - p2 * r).astype(jnp.bfloat16)
        am = (pm1 - pm2 * r).astype(jnp.bfloat16)
        o = (_dot(a, vd_ref[...]) + _dot(am, vdm_ref[...])) * (1.0 / l1)
        y = o * _rms_scale(o, DIFF_V) * subln_ref[...] * (1.0 - LAM_INIT)
        y = y * zb_ref[rows, :].astype(jnp.float32)
        o_ref[rows, :] = y.astype(o_ref.dtype)

    return scores, scores_next, finish


def _attn_kernel(q_ref, k_ref, vt_ref, km_ref, vmt_ref, za_ref,
                 qd_ref, kd_ref, vd_ref, kdm_ref, vdm_ref, btab_ref, mtab_ref, lam_ref, subln_ref, zb_ref,
                 qn_ref, kn_ref, kmn_ref, qdn_ref, kdn_ref, kdmn_ref, btabn_ref, mtabn_ref,
                 oa_ref, ob_ref, ms_ref, msm_ref, mmx_ref, ds_ref, dsm_ref, dmx_ref):
    m_scores, m_scores_next, m_finish = _mla_tile_fns(
        q_ref, k_ref, vt_ref, km_ref, vmt_ref, za_ref, qn_ref, kn_ref, kmn_ref, oa_ref,
        ms_ref, msm_ref, mmx_ref)
    d_scores, d_scores_next, d_finish = _diff_tile_fns(
        qd_ref, kd_ref, vd_ref, kdm_ref, vdm_ref, btab_ref, mtab_ref, lam_ref, subln_ref, zb_ref,
        qdn_ref, kdn_ref, kdmn_ref, btabn_ref, mtabn_ref, ob_ref, ds_ref, dsm_ref, dmx_ref)
    n_t = qd_ref.shape[0] // TQ
    n_u = n_t * MLA_PER_STEP

    @pl.when((pl.program_id(0) == 0) & (pl.program_id(1) == 0))
    def _():
        d_scores(0, 0)
        m_scores(0, 0)

    for t in range(n_t):
        if t + 1 < n_t:
            d_scores(t + 1, (t + 1) % 2)
        else:
            d_scores_next(0)
        for u in range(t * MLA_PER_STEP, (t + 1) * MLA_PER_STEP):
            if u + 1 < n_u:
                m_scores(u + 1, (u + 1) % 2)
            else:
                m_scores_next(0)
            m_finish(u, u % 2)
        d_finish(t, t % 2)


def _attention(q, k, vt, km, vmt, zs, qk, vd, qk_meta, vd_meta, btab, mtab, lam, subln, seq):
    nb = q.shape[0]
    cps = D_MODEL // DIFF_V
    mps = MLA_PER_STEP
    n_steps = nb * DIFF_HEADS

    def nxt(b, h):
        g = jnp.minimum(b * DIFF_HEADS + h + 1, n_steps - 1)
        return g // DIFF_HEADS, g % DIFF_HEADS

    def nb_(b, h):
        return nxt(b, h)[0]

    def nh_(b, h):
        return nxt(b, h)[1]

    return pl.pallas_call(
        _attn_kernel,
        grid=(nb, DIFF_HEADS),
        in_specs=[
            pl.BlockSpec((1, mps, seq, HEAD_PAD), lambda b, h: (b, h, 0, 0)),
            pl.BlockSpec((1, mps, seq, HEAD_PAD), lambda b, h: (b, h, 0, 0)),
            pl.BlockSpec((1, mps, MLA_V, seq), lambda b, h: (b, h, 0, 0)),
            pl.BlockSpec((mps, META_PAD, HEAD_PAD), lambda b, h: (h, 0, 0)),
            pl.BlockSpec((mps, MLA_V, META_PAD), lambda b, h: (h, 0, 0)),
            pl.BlockSpec((seq, mps * MLA_V), lambda b, h: (b, h)),
            pl.BlockSpec((seq, DIFF_V), lambda b, h: (b, h)),
            pl.BlockSpec((seq, DIFF_V), lambda b, h: (b, cps + h)),
            pl.BlockSpec((seq, DIFF_V), lambda b, h: (b, h)),
            pl.BlockSpec((META_PAD, DIFF_V), lambda b, h: (0, h)),
            pl.BlockSpec((META_PAD, DIFF_V), lambda b, h: (0, h)),
            pl.BlockSpec((1, 5, TQ, KC), lambda b, h: (h, 0, 0, 0)),
            pl.BlockSpec((1, 2, TQ, META_PAD), lambda b, h: (h, 0, 0, 0)),
            pl.BlockSpec((4, DIFF_QK), lambda b, h: (0, 0)),
            pl.BlockSpec((1, DIFF_V), lambda b, h: (0, 0)),
            pl.BlockSpec((seq, DIFF_V), lambda b, h: (b, cps + h)),
            pl.BlockSpec((1, 1, TQ, HEAD_PAD), lambda b, h: (nb_(b, h), mps * nh_(b, h), 0, 0)),
            pl.BlockSpec((1, 1, seq, HEAD_PAD), lambda b, h: (nb_(b, h), mps * nh_(b, h), 0, 0)),
            pl.BlockSpec((1, META_PAD, HEAD_PAD), lambda b, h: (mps * nh_(b, h), 0, 0)),
            pl.BlockSpec((TQ, DIFF_V), lambda b, h: (nb_(b, h) * (seq // TQ), nh_(b, h))),
            pl.BlockSpec((seq, DIFF_V), lambda b, h: (nb_(b, h), cps + nh_(b, h))),
            pl.BlockSpec((META_PAD, DIFF_V), lambda b, h: (0, nh_(b, h))),
            pl.BlockSpec((1, 5, TQ, KC), lambda b, h: (nh_(b, h), 0, 0, 0)),
            pl.BlockSpec((1, 2, TQ, META_PAD), lambda b, h: (nh_(b, h), 0, 0, 0)),
        ],
        out_specs=[
            pl.BlockSpec((seq, mps * MLA_V), lambda b, h: (b, h)),
            pl.BlockSpec((seq, DIFF_V), lambda b, h: (b, h)),
        ],
        out_shape=[
            jax.ShapeDtypeStruct((nb * seq, MLA_HEADS * MLA_V), jnp.bfloat16),
            jax.ShapeDtypeStruct((nb * seq, DIFF_HEADS * DIFF_V), jnp.bfloat16),
        ],
        scratch_shapes=[
            pltpu.VMEM((2, seq, TQ), jnp.float32), pltpu.VMEM((2, META_PAD, TQ), jnp.float32),
            pltpu.VMEM((2, 1, TQ), jnp.float32),
            pltpu.VMEM((2, 2, TQ, seq), jnp.float32), pltpu.VMEM((2, 2, TQ, META_PAD), jnp.float32),
            pltpu.VMEM((2, 2, TQ, LANES), jnp.float32),
        ],
        compiler_params=pltpu.CompilerParams(
            dimension_semantics=("arbitrary", "arbitrary"), vmem_limit_bytes=VMEM_LIMIT_BYTES),
        name="attention",
    )(q, k, vt, km, vmt, zs, qk, qk, vd, qk_meta, vd_meta, btab, mtab, lam, subln, zs,
      q, k, km, qk, qk, qk_meta, btab, mtab)


def _merge_kernel(oa_ref, ob_ref, wa_ref, wb_ref, ga_ref, gb_ref, m_ref):
    ya = _dot(oa_ref[...], wa_ref[...])
    yb = _dot(ob_ref[...], wb_ref[...])
    m = ga_ref[...].astype(jnp.float32) * ya + gb_ref[...].astype(jnp.float32) * yb
    m_ref[...] = m.astype(m_ref.dtype)


def _merge(oa, ob, wa, wb, gs, tm, tn):
    rows = oa.shape[0]
    tps = D_MODEL // tn
    return pl.pallas_call(
        _merge_kernel,
        grid=(rows // tm, D_MODEL // tn),
        in_specs=[
            pl.BlockSpec((tm, D_MODEL), lambda i, j: (i, 0)),
            pl.BlockSpec((tm, D_MODEL), lambda i, j: (i, 0)),
            pl.BlockSpec((D_MODEL, tn), lambda i, j: (0, j)),
            pl.BlockSpec((D_MODEL, tn), lambda i, j: (0, j)),
            pl.BlockSpec((tm, tn), lambda i, j: (i, j)),
            pl.BlockSpec((tm, tn), lambda i, j: (i, tps + j)),
        ],
        out_specs=pl.BlockSpec((tm, tn), lambda i, j: (i, j)),
        out_shape=jax.ShapeDtypeStruct((rows, D_MODEL), jnp.bfloat16),
        compiler_params=pltpu.CompilerParams(
            dimension_semantics=("arbitrary", "arbitrary"), vmem_limit_bytes=VMEM_LIMIT_BYTES),
        name="merge",
    )(oa, ob, wa, wb, gs, gs)


def _out_kernel(m_ref, w_ref, x_ref, o_ref):
    o_ref[...] = x_ref[...] + _dot(m_ref[...], w_ref[...])


def _out_proj(m, wout, x2, tm, tn):
    rows = m.shape[0]
    return pl.pallas_call(
        _out_kernel,
        grid=(rows // tm, D_MODEL // tn),
        in_specs=[
            pl.BlockSpec((tm, D_MODEL), lambda i, j: (i, 0)),
            pl.BlockSpec((D_MODEL, tn), lambda i, j: (0, j)),
            pl.BlockSpec((tm, tn), lambda i, j: (i, j)),
        ],
        out_specs=pl.BlockSpec((tm, tn), lambda i, j: (i, j)),
        out_shape=jax.ShapeDtypeStruct((rows, D_MODEL), jnp.float32),
        compiler_params=pltpu.CompilerParams(
            dimension_semantics=("arbitrary", "arbitrary"), vmem_limit_bytes=VMEM_LIMIT_BYTES),
        name="out_proj",
    )(m, wout, x2)


def _t5_bucket(rel):
    nb = REL_BUCKETS // 2
    max_exact = nb // 2
    ret = jnp.where(rel > 0, nb, 0)
    n = jnp.abs(rel)
    nf = jnp.maximum(n, 1).astype(jnp.float32)
    large = max_exact + (jnp.log(nf / max_exact) / math.log(REL_MAX_DIST / max_exact)
                         * (nb - max_exact)).astype(jnp.int32)
    large = jnp.minimum(large, nb - 1)
    return ret + jnp.where(n < max_exact, n, large)


def _bias_tab_kernel(rb_ref, bk_ref, mbk_ref, btab_ref, mtab_ref):
    h = pl.program_id(0)
    bk = bk_ref[...]
    mbk = mbk_ref[...]
    acc = jnp.zeros(bk.shape, jnp.float32)
    macc = jnp.where(mbk < 0, MASK_VALUE, 0.0).astype(jnp.float32)
    for b in range(REL_BUCKETS):
        val = rb_ref[b, h] * LOG2E
        acc = jnp.where(bk == b, val, acc)
        macc = jnp.where(mbk == b, val, macc)
    btab_ref[0] = acc
    mtab_ref[0] = macc


def _bias_tables(rel_bias):
    qq = jnp.arange(TQ, dtype=jnp.int32)[:, None]
    kk = jnp.arange(KC, dtype=jnp.int32)[None, :]
    dd = jnp.arange(-2, 3, dtype=jnp.int32)[:, None, None]
    bk = _t5_bucket(dd * KC + kk[None] - qq[None])
    jm = jnp.arange(META_PAD, dtype=jnp.int32)[None, None, :]
    qpos = N_META + jnp.arange(2, dtype=jnp.int32)[:, None, None] * TQ + qq[None]
    mbk = jnp.where(jm < N_META, _t5_bucket(jm - qpos), -1)
    return pl.pallas_call(
        _bias_tab_kernel,
        grid=(DIFF_HEADS,),
        in_specs=[
            pl.BlockSpec(memory_space=pltpu.SMEM),
            pl.BlockSpec((5, TQ, KC), lambda h: (0, 0, 0)),
            pl.BlockSpec((2, TQ, META_PAD), lambda h: (0, 0, 0)),
        ],
        out_specs=[
            pl.BlockSpec((1, 5, TQ, KC), lambda h: (h, 0, 0, 0)),
            pl.BlockSpec((1, 2, TQ, META_PAD), lambda h: (h, 0, 0, 0)),
        ],
        out_shape=[
            jax.ShapeDtypeStruct((DIFF_HEADS, 5, TQ, KC), jnp.float32),
            jax.ShapeDtypeStruct((DIFF_HEADS, 2, TQ, META_PAD), jnp.float32),
        ],
        compiler_params=pltpu.CompilerParams(dimension_semantics=("arbitrary",)),
        name="bias_tables",
    )(rel_bias.astype(jnp.float32), bk, mbk)


def _rope_tables(n_pos):
    half = MLA_ROPE // 2
    inv = ROPE_THETA ** (-jnp.arange(half, dtype=jnp.float32) / half)
    ang = jnp.arange(n_pos, dtype=jnp.int32).astype(jnp.float32)[:, None] * inv[None, :]
    c, s = jnp.cos(ang), jnp.sin(ang)
    z = jnp.zeros((n_pos, LANES - MLA_ROPE), jnp.float32)
    cos_t = jnp.concatenate([c, c, z], axis=-1)
    sin_t = jnp.concatenate([-s, s, z], axis=-1)
    return cos_t, sin_t


def _pad_rows(a, rows):
    return jnp.pad(a, [(0, rows - a.shape[0])] + [(0, 0)] * (a.ndim - 1))


def kernel(x, meta_tokens, rel_bias, norm_in, w_in, q_a_norm, kv_a_norm, w_uq, w_ukv, mla_q_norm, mla_k_norm,
           diff_q_norm, diff_k_norm, diff_lambda, diff_subln, w_branch_a, w_branch_b, w_out):
    nb, seq, _ = x.shape
    bf = jnp.bfloat16
    n_small = MLA_Q_RANK + MLA_KV_RANK + MLA_ROPE

    w_in_t = jnp.swapaxes(w_in[0], 0, 1)
    wsm = jnp.pad(w_in_t[:n_small], ((0, SMALL_W - n_small), (0, 0))).astype(bf)
    wbig = w_in_t[n_small:].astype(bf)
    wuq = jnp.pad(w_uq[0].reshape(MLA_Q_RANK, MLA_HEADS, MLA_QK),
                  ((0, 0), (0, 0), (0, HEAD_PAD - MLA_QK))).reshape(MLA_Q_RANK, MLA_HEADS * HEAD_PAD).astype(bf)
    wukv = w_ukv[0].astype(bf)
    wa, wb, wo = w_branch_a[0].astype(bf), w_branch_b[0].astype(bf), w_out[0].astype(bf)

    gin = norm_in[0][None]
    gqa, gkva = q_a_norm[0][None], kv_a_norm[0][None]
    gq = jnp.pad(mla_q_norm[0], (0, HEAD_PAD - MLA_QK))[None]
    gk = jnp.pad(mla_k_norm[0], (0, HEAD_PAD - MLA_QK))[None]
    gqd, gkd = diff_q_norm[0][None], diff_k_norm[0][None]
    subln = diff_subln[0][None]
    lam = diff_lambda[0]

    cos_t, sin_t = _rope_tables(N_META + seq)
    btab, mtab = _bias_tables(rel_bias)

    x2 = x.reshape(nb * seq, D_MODEL)
    meta = meta_tokens.astype(x.dtype)

    qd_scale = LOG2E * DIFF_QK ** -0.5
    reps = D_MODEL // DIFF_QK
    gain_qk = jnp.concatenate([jnp.tile(gqd * qd_scale, (1, reps)), jnp.tile(gkd, (1, reps))], axis=-1)

    q, k, vt, u = _mla_proj(x, gin, wsm, gqa, gkva, wuq, wukv, gq, gk, cos_t[N_META:], sin_t[N_META:], tm=256)
    u2 = u.reshape(nb * seq, D_MODEL)
    zs = _seg_proj(u2, wbig, (SEG_ZA, SEG_ZB), "silu", 1024, D_MODEL)
    qk = _seg_proj(u2, wbig, (SEG_QD, SEG_KD), "norm", 1024, D_MODEL, gain=gain_qk)
    vd = _seg_proj(u2, wbig, (SEG_VD,), "copy", 1024, D_MODEL)
    gs = _seg_proj(u2, wbig, (SEG_GA, SEG_GB), "sigmoid", 1024, D_MODEL)

    _, km, vmt, um = _mla_proj(meta[None], gin, wsm, gqa, gkva, wuq, wukv, gq, gk,
                              cos_t[:N_META], sin_t[:N_META], tm=N_META)
    km = jnp.pad(km[0], ((0, 0), (0, META_PAD - N_META), (0, 0)))
    vmt = jnp.pad(vmt[0], ((0, 0), (0, 0), (0, META_PAD - N_META)))
    k_meta = _pad_rows(_seg_proj(um[0], wbig, (SEG_KD,), "norm", N_META, 1024, gain=gain_qk[:, D_MODEL:]), META_PAD)
    vd_meta = _pad_rows(_seg_proj(um[0], wbig, (SEG_VD,), "copy", N_META, 1024), META_PAD)

    oa, ob = _attention(q, k, vt, km, vmt, zs, qk, vd, k_meta, vd_meta, btab, mtab, lam, subln, seq)
    m = _merge(oa, ob, wa, wb, gs, tm=1024, tn=1024)
    out = _out_proj(m, wo, x2, tm=512, tn=D_MODEL)
    return out.reshape(nb, seq, D_MODEL)
```

```python
import functools
import math

import jax
import jax.numpy as jnp
from jax import lax
from jax.experimental import pallas as pl
from jax.experimental.pallas import tpu as pltpu

D_MODEL = 2048
N_META = 16
EPS = 1e-6

MLA_HEADS = 16
MLA_Q_RANK = 768
MLA_KV_RANK = 512
MLA_NOPE = 128
MLA_ROPE = 64
MLA_V = 128
MLA_QK = MLA_NOPE + MLA_ROPE
ROPE_THETA = 10000.0

DIFF_HEADS = 8
DIFF_QK = 128
DIFF_V = 2 * DIFF_QK
LAM_INIT = 0.8 - 0.6 * math.exp(-0.3 * 0)

REL_BUCKETS = 32
REL_MAX_DIST = 128

LANES = 128
MXU_DIM = 256
VMEM_LIMIT_BYTES = 62 * 1024 * 1024

HEAD_PAD = MXU_DIM
SMALL_W = MLA_Q_RANK + MLA_KV_RANK + LANES
META_PAD = LANES
MASK_VALUE = -1e30
TQ = 256
KC = 256
LOG2E = math.log2(math.e)

SEG_ZA, SEG_QD, SEG_KD, SEG_VD, SEG_ZB, SEG_GA, SEG_GB = range(7)


def _dot(a, b):
    return jnp.dot(a, b, preferred_element_type=jnp.float32)


def _dot_nt(a, b):
    return lax.dot_general(a, b, (((1,), (1,)), ((), ())), preferred_element_type=jnp.float32)


def _rms_scale(x, width):
    return lax.rsqrt(jnp.sum(x * x, axis=-1, keepdims=True) * (1.0 / width) + EPS)


def _rot_half64(x):
    return pltpu.roll(x, 32, 1) + pltpu.roll(x, 96, 1)


def _mla_proj_kernel(x_ref, gin_ref, wsm_ref, gqa_ref, gkva_ref, wuq_ref, wuk_ref, wvt_ref,
                     gq_ref, gk_ref, cos_ref, sin_ref, q_ref, k_ref, v_ref, u_ref, *, q_scale):
    x = x_ref[...]
    u = (x * _rms_scale(x, D_MODEL) * gin_ref[...]).astype(jnp.bfloat16)
    u_ref[...] = u
    p = _dot_nt(u, wsm_ref[...])
    cq = p[:, :MLA_Q_RANK]
    ckv = p[:, MLA_Q_RANK:MLA_Q_RANK + MLA_KV_RANK]
    kr = p[:, MLA_Q_RANK + MLA_KV_RANK:]
    cqn = (cq * _rms_scale(cq, MLA_Q_RANK) * gqa_ref[...]).astype(jnp.bfloat16)
    ckvn = (ckv * _rms_scale(ckv, MLA_KV_RANK) * gkva_ref[...]).astype(jnp.bfloat16)
    qf = _dot(cqn, wuq_ref[...])
    kf = _dot(ckvn, wuk_ref[...])
    vt = _dot_nt(wvt_ref[...], ckvn).astype(jnp.bfloat16)

    cos = cos_ref[...]
    sin = sin_ref[...]
    gq = gq_ref[...]
    gk = gk_ref[...]
    gq_nope, gq_rope = gq[:, :LANES], gq[:, LANES:]
    gk_nope, gk_rope = gk[:, :LANES], gk[:, LANES:]

    ss_kr = jnp.sum(kr * kr, axis=-1, keepdims=True)
    krg = kr * gk_rope
    kr_roped = krg * cos + _rot_half64(krg) * sin

    for h in range(MLA_HEADS):
        qh = qf[:, h * HEAD_PAD:(h + 1) * HEAD_PAD]
        rq = _rms_scale(qh, MLA_QK)
        q_nope = qh[:, :LANES] * rq * gq_nope
        q_r = qh[:, LANES:] * rq * gq_rope
        q_r = q_r * cos + _rot_half64(q_r) * sin
        q_ref[0, h, :, :LANES] = (q_nope * q_scale).astype(jnp.bfloat16)
        q_ref[0, h, :, LANES:] = (q_r * q_scale).astype(jnp.bfloat16)

        k_nope = kf[:, h * MLA_NOPE:(h + 1) * MLA_NOPE]
        rk = lax.rsqrt((jnp.sum(k_nope * k_nope, axis=-1, keepdims=True) + ss_kr) * (1.0 / MLA_QK) + EPS)
        k_ref[0, h, :, :LANES] = (k_nope * rk * gk_nope).astype(jnp.bfloat16)
        k_ref[0, h, :, LANES:] = (kr_roped * rk).astype(jnp.bfloat16)
        v_ref[0, h] = vt[h * MLA_V:(h + 1) * MLA_V, :]


def _mla_proj(x3, gin, wsm, gqa, gkva, wuq, wuk, wvt, gq, gk, cos, sin, tm):
    nb, rows, _ = x3.shape
    const = lambda b, i: (0, 0)
    kern = functools.partial(_mla_proj_kernel, q_scale=LOG2E * MLA_QK ** -0.5)
    return pl.pallas_call(
        kern,
        grid=(nb, rows // tm),
        in_specs=[
            pl.BlockSpec((None, tm, D_MODEL), lambda b, i: (b, i, 0)),
            pl.BlockSpec((1, D_MODEL), const),
            pl.BlockSpec((SMALL_W, D_MODEL), const),
            pl.BlockSpec((1, MLA_Q_RANK), const),
            pl.BlockSpec((1, MLA_KV_RANK), const),
            pl.BlockSpec((MLA_Q_RANK, MLA_HEADS * HEAD_PAD), const),
            pl.BlockSpec((MLA_KV_RANK, MLA_HEADS * MLA_NOPE), const),
            pl.BlockSpec((MLA_HEADS * MLA_V, MLA_KV_RANK), const),
            pl.BlockSpec((1, HEAD_PAD), const),
            pl.BlockSpec((1, HEAD_PAD), const),
            pl.BlockSpec((tm, LANES), lambda b, i: (i, 0)),
            pl.BlockSpec((tm, LANES), lambda b, i: (i, 0)),
        ],
        out_specs=[
            pl.BlockSpec((1, MLA_HEADS, tm, HEAD_PAD), lambda b, i: (b, 0, i, 0)),
            pl.BlockSpec((1, MLA_HEADS, tm, HEAD_PAD), lambda b, i: (b, 0, i, 0)),
            pl.BlockSpec((1, MLA_HEADS, MLA_V, tm), lambda b, i: (b, 0, 0, i)),
            pl.BlockSpec((None, tm, D_MODEL), lambda b, i: (b, i, 0)),
        ],
        out_shape=[
            jax.ShapeDtypeStruct((nb, MLA_HEADS, rows, HEAD_PAD), jnp.bfloat16),
            jax.ShapeDtypeStruct((nb, MLA_HEADS, rows, HEAD_PAD), jnp.bfloat16),
            jax.ShapeDtypeStruct((nb, MLA_HEADS, MLA_V, rows), jnp.bfloat16),
            jax.ShapeDtypeStruct((nb, rows, D_MODEL), jnp.bfloat16),
        ],
        compiler_params=pltpu.CompilerParams(
            dimension_semantics=("arbitrary", "arbitrary"), vmem_limit_bytes=VMEM_LIMIT_BYTES),
        name="mla_proj",
    )(x3, gin, wsm, gqa, gkva, wuq, wuk, wvt, gq, gk, cos, sin)


def _seg_proj_kernel(u_ref, w_ref, *rest, kind):
    o_ref = rest[-1]
    acc = _dot_nt(u_ref[...], w_ref[...])
    if kind == "silu":
        o_ref[...] = (acc / (1.0 + jnp.exp(-acc))).astype(o_ref.dtype)
    elif kind == "sigmoid":
        o_ref[...] = (1.0 / (1.0 + jnp.exp(-acc))).astype(o_ref.dtype)
    elif kind == "copy":
        o_ref[...] = acc.astype(o_ref.dtype)
    else:
        gain = rest[0][...]
        for c in range(acc.shape[1] // DIFF_QK):
            cols = slice(c * DIFF_QK, (c + 1) * DIFF_QK)
            g = acc[:, cols]
            o_ref[:, cols] = (g * _rms_scale(g, DIFF_QK) * gain[:, cols]).astype(o_ref.dtype)


def _seg_proj(u2, wbig, segs, kind, tm, tn, gain=None):
    rows = u2.shape[0]
    tps = D_MODEL // tn
    if len(segs) == 1:
        w_map = lambda i, j: (segs[0] * tps + j, 0)
    else:
        w_map = lambda i, j: (jnp.where(j < tps, segs[0] * tps + j, segs[1] * tps + j - tps), 0)
    in_specs = [pl.BlockSpec((tm, D_MODEL), lambda i, j: (i, 0)), pl.BlockSpec((tn, D_MODEL), w_map)]
    args = [u2, wbig]
    if gain is not None:
        in_specs.append(pl.BlockSpec((1, tn), lambda i, j: (0, j)))
        args.append(gain)
    return pl.pallas_call(
        functools.partial(_seg_proj_kernel, kind=kind),
        grid=(rows // tm, len(segs) * tps),
        in_specs=in_specs,
        out_specs=pl.BlockSpec((tm, tn), lambda i, j: (i, j)),
        out_shape=jax.ShapeDtypeStruct((rows, len(segs) * D_MODEL), jnp.bfloat16),
        compiler_params=pltpu.CompilerParams(
            dimension_semantics=("arbitrary", "arbitrary"), vmem_limit_bytes=VMEM_LIMIT_BYTES),
        name="in_proj_" + kind,
    )(*args)


MLA_PER_STEP = MLA_HEADS // DIFF_HEADS


def _lane_fold_max(x):
    blocks = [x[:, i * LANES:(i + 1) * LANES] for i in range(x.shape[1] // LANES)]
    return functools.reduce(jnp.maximum, blocks)


ONES_ROWS = 16


def _mla_tile_fns(q_ref, k_ref, vt_ref, km_ref, vmt_ref, za_ref, qn_ref, kn_ref, kmn_ref,
                  o_ref, s_ref, sm_ref, mx_ref):
    n_t = q_ref.shape[2] // TQ

    def scores_into(slot, qt, k, km):
        s = _dot_nt(k, qt)
        sm = _dot_nt(km, qt)
        key = lax.broadcasted_iota(jnp.int32, sm.shape, 0)
        sm = jnp.where(key < N_META, sm, MASK_VALUE)
        s_ref[slot] = s
        sm_ref[slot] = sm
        mx_ref[slot] = jnp.maximum(jnp.max(s, axis=0, keepdims=True), jnp.max(sm, axis=0, keepdims=True))

    def scores(u, slot):
        hh, t = divmod(u, n_t)
        scores_into(slot, q_ref[0, hh, t * TQ:(t + 1) * TQ, :], k_ref[0, hh], km_ref[hh])

    def scores_next(slot):
        scores_into(slot, qn_ref[0, 0], kn_ref[0, 0], kmn_ref[0])

    def finish(u, slot):
        hh, t = divmod(u, n_t)
        rows = slice(t * TQ, (t + 1) * TQ)
        cols = slice(hh * MLA_V, (hh + 1) * MLA_V)
        m = mx_ref[slot]
        p = jnp.exp2(s_ref[slot] - m).astype(jnp.bfloat16)
        pm = jnp.exp2(sm_ref[slot] - m).astype(jnp.bfloat16)
        ones = jnp.ones((ONES_ROWS, vt_ref.shape[3]), jnp.bfloat16)
        ones_m = jnp.ones((ONES_ROWS, META_PAD), jnp.bfloat16)
        o = (_dot(jnp.concatenate([vt_ref[0, hh], ones], axis=0), p)
             + _dot(jnp.concatenate([vmt_ref[hh], ones_m], axis=0), pm))
        l = o[MLA_V:MLA_V + 1, :]
        y = (o[:MLA_V, :] * (1.0 / l)).T
        o_ref[rows, cols] = (y * za_ref[rows, cols].astype(jnp.float32)).astype(o_ref.dtype)

    return scores, scores_next, finish


def _diff_tile_fns(qd_ref, kd_ref, vd_ref, kdm_ref, vdm_ref, btab_ref, mtab_ref, lam_ref, subln_ref, zb_ref,
                   qdn_ref, kdn_ref, kdmn_ref, btabn_ref, mtabn_ref, o_ref, s_ref, sm_ref, mx_ref):
    n_chunks = kd_ref.shape[0] // KC
    lv = lam_ref[...]
    lam = (jnp.exp(jnp.sum(lv[0:1] * lv[1:2], axis=-1, keepdims=True))
           - jnp.exp(jnp.sum(lv[2:3] * lv[3:4], axis=-1, keepdims=True)) + LAM_INIT)

    c_neg = btab_ref[0, 0, 0:1, 0:1]
    c_pos = btab_ref[0, 4, 0:1, 0:1]

    def scores_into(t, slot, q_tile_ref, k_ref_, km_ref_, bt_ref, mt_ref):
        cn, cp = bt_ref[0, 0, 0:1, 0:1], bt_ref[0, 4, 0:1, 0:1]
        for half in range(2):
            lo = half * DIFF_QK
            qm = q_tile_ref[:, lo:lo + DIFF_QK]
            sm = _dot_nt(qm, km_ref_[:, lo:lo + DIFF_QK]) + mt_ref[0, min(t, 1)]
            sm_ref[slot, half] = sm
            mx = {"band": sm, "neg": None, "pos": None}
            for c in range(n_chunks):
                sc = _dot_nt(qm, k_ref_[c * KC:(c + 1) * KC, lo:lo + DIFF_QK])
                if abs(c - t) <= 1:
                    sc = sc + bt_ref[0, c - t + 2]
                    grp = "band"
                else:
                    grp = "neg" if c < t else "pos"
                s_ref[slot, half, :, c * KC:(c + 1) * KC] = sc
                fold = _lane_fold_max(sc)
                mx[grp] = fold if mx[grp] is None else jnp.maximum(mx[grp], fold)
            acc = mx["band"]
            if mx["neg"] is not None:
                acc = jnp.maximum(acc, mx["neg"] + cn)
            if mx["pos"] is not None:
                acc = jnp.maximum(acc, mx["pos"] + cp)
            mx_ref[slot, half] = acc

    def scores(t, slot):
        scores_into(t, slot, qd_ref.at[t * TQ:(t + 1) * TQ], kd_ref, kdm_ref, btab_ref, mtab_ref)

    def scores_next(slot):
        scores_into(0, slot, qdn_ref, kdn_ref, kdmn_ref, btabn_ref, mtabn_ref)

    def softmax_parts(t, slot, half):
        m = jnp.max(mx_ref[slot, half], axis=-1, keepdims=True)
        m_neg = m - c_neg
        m_pos = m - c_pos
        chunks = []
        for c in range(n_chunks):
            mc = m if abs(c - t) <= 1 else (m_neg if c < t else m_pos)
            chunks.append(jnp.exp2(s_ref[slot, half, :, c * KC:(c + 1) * KC] - mc))
        p = jnp.concatenate(chunks, axis=-1)
        pm = jnp.exp2(sm_ref[slot, half] - m)
        l = jnp.sum(p, axis=-1, keepdims=True) + jnp.sum(pm, axis=-1, keepdims=True)
        return p, pm, l

    def finish(t, slot):
        rows = slice(t * TQ, (t + 1) * TQ)
        p1, pm1, l1 = softmax_parts(t, slot, 0)
        p2, pm2, l2 = softmax_parts(t, slot, 1)
        r = lam * l1 * (1.0 / l2)
        a = (p1 - p2 * r).astype(jnp.bfloat16)
        am = (pm1 - pm2 * r).astype(jnp.bfloat16)
        o = (_dot(a, vd_ref[...]) + _dot(am, vdm_ref[...])) * (1.0 / l1)
        y = o * _rms_scale(o, DIFF_V) * subln_ref[...] * (1.0 - LAM_INIT)
        y = y * zb_ref[rows, :].astype(jnp.float32)
        o_ref[rows, :] = y.astype(o_ref.dtype)

    return scores, scores_next, finish


def _attn_kernel(q_ref, k_ref, vt_ref, km_ref, vmt_ref, za_ref,
                 qd_ref, kd_ref, vd_ref, kdm_ref, vdm_ref, btab_ref, mtab_ref, lam_ref, subln_ref, zb_ref,
                 qn_ref, kn_ref, kmn_ref, qdn_ref, kdn_ref, kdmn_ref, btabn_ref, mtabn_ref,
                 oa_ref, ob_ref, ms_ref, msm_ref, mmx_ref, ds_ref, dsm_ref, dmx_ref):
    m_scores, m_scores_next, m_finish = _mla_tile_fns(
        q_ref, k_ref, vt_ref, km_ref, vmt_ref, za_ref, qn_ref, kn_ref, kmn_ref, oa_ref,
        ms_ref, msm_ref, mmx_ref)
    d_scores, d_scores_next, d_finish = _diff_tile_fns(
        qd_ref, kd_ref, vd_ref, kdm_ref, vdm_ref, btab_ref, mtab_ref, lam_ref, subln_ref, zb_ref,
        qdn_ref, kdn_ref, kdmn_ref, btabn_ref, mtabn_ref, ob_ref, ds_ref, dsm_ref, dmx_ref)
    n_t = qd_ref.shape[0] // TQ
    n_u = n_t * MLA_PER_STEP

    @pl.when((pl.program_id(0) == 0) & (pl.program_id(1) == 0))
    def _():
        d_scores(0, 0)
        m_scores(0, 0)

    for t in range(n_t):
        if t + 1 < n_t:
            d_scores(t + 1, (t + 1) % 2)
        else:
            d_scores_next(0)
        for u in range(t * MLA_PER_STEP, (t + 1) * MLA_PER_STEP):
            if u + 1 < n_u:
                m_scores(u + 1, (u + 1) % 2)
            else:
                m_scores_next(0)
            m_finish(u, u % 2)
        d_finish(t, t % 2)


def _attention(q, k, vt, km, vmt, zs, qk, vd, qk_meta, vd_meta, btab, mtab, lam, subln, seq):
    nb = q.shape[0]
    cps = D_MODEL // DIFF_V
    mps = MLA_PER_STEP
    n_steps = nb * DIFF_HEADS

    def nxt(b, h):
        g = jnp.minimum(b * DIFF_HEADS + h + 1, n_steps - 1)
        return g // DIFF_HEADS, g % DIFF_HEADS

    def nb_(b, h):
        return nxt(b, h)[0]

    def nh_(b, h):
        return nxt(b, h)[1]

    return pl.pallas_call(
        _attn_kernel,
        grid=(nb, DIFF_HEADS),
        in_specs=[
            pl.BlockSpec((1, mps, seq, HEAD_PAD), lambda b, h: (b, h, 0, 0)),
            pl.BlockSpec((1, mps, seq, HEAD_PAD), lambda b, h: (b, h, 0, 0)),
            pl.BlockSpec((1, mps, MLA_V, seq), lambda b, h: (b, h, 0, 0)),
            pl.BlockSpec((mps, META_PAD, HEAD_PAD), lambda b, h: (h, 0, 0)),
            pl.BlockSpec((mps, MLA_V, META_PAD), lambda b, h: (h, 0, 0)),
            pl.BlockSpec((seq, mps * MLA_V), lambda b, h: (b, h)),
            pl.BlockSpec((seq, DIFF_V), lambda b, h: (b, h)),
            pl.BlockSpec((seq, DIFF_V), lambda b, h: (b, cps + h)),
            pl.BlockSpec((seq, DIFF_V), lambda b, h: (b, h)),
            pl.BlockSpec((META_PAD, DIFF_V), lambda b, h: (0, h)),
            pl.BlockSpec((META_PAD, DIFF_V), lambda b, h: (0, h)),
            pl.BlockSpec((1, 5, TQ, KC), lambda b, h: (h, 0, 0, 0)),
            pl.BlockSpec((1, 2, TQ, META_PAD), lambda b, h: (h, 0, 0, 0)),
            pl.BlockSpec((4, DIFF_QK), lambda b, h: (0, 0)),
            pl.BlockSpec((1, DIFF_V), lambda b, h: (0, 0)),
            pl.BlockSpec((seq, DIFF_V), lambda b, h: (b, cps + h)),
            pl.BlockSpec((1, 1, TQ, HEAD_PAD), lambda b, h: (nb_(b, h), mps * nh_(b, h), 0, 0)),
            pl.BlockSpec((1, 1, seq, HEAD_PAD), lambda b, h: (nb_(b, h), mps * nh_(b, h), 0, 0)),
            pl.BlockSpec((1, META_PAD, HEAD_PAD), lambda b, h: (mps * nh_(b, h), 0, 0)),
            pl.BlockSpec((TQ, DIFF_V), lambda b, h: (nb_(b, h) * (seq // TQ), nh_(b, h))),
            pl.BlockSpec((seq, DIFF_V), lambda b, h: (nb_(b, h), cps + nh_(b, h))),
            pl.BlockSpec((META_PAD, DIFF_V), lambda b, h: (0, nh_(b, h))),
            pl.BlockSpec((1, 5, TQ, KC), lambda b, h: (nh_(b, h), 0, 0, 0)),
            pl.BlockSpec((1, 2, TQ, META_PAD), lambda b, h: (nh_(b, h), 0, 0, 0)),
        ],
        out_specs=[
            pl.BlockSpec((seq, mps * MLA_V), lambda b, h: (b, h)),
            pl.BlockSpec((seq, DIFF_V), lambda b, h: (b, h)),
        ],
        out_shape=[
            jax.ShapeDtypeStruct((nb * seq, MLA_HEADS * MLA_V), jnp.bfloat16),
            jax.ShapeDtypeStruct((nb * seq, DIFF_HEADS * DIFF_V), jnp.bfloat16),
        ],
        scratch_shapes=[
            pltpu.VMEM((2, seq, TQ), jnp.float32), pltpu.VMEM((2, META_PAD, TQ), jnp.float32),
            pltpu.VMEM((2, 1, TQ), jnp.float32),
            pltpu.VMEM((2, 2, TQ, seq), jnp.float32), pltpu.VMEM((2, 2, TQ, META_PAD), jnp.float32),
            pltpu.VMEM((2, 2, TQ, LANES), jnp.float32),
        ],
        compiler_params=pltpu.CompilerParams(
            dimension_semantics=("arbitrary", "arbitrary"), vmem_limit_bytes=VMEM_LIMIT_BYTES),
        name="attention",
    )(q, k, vt, km, vmt, zs, qk, qk, vd, qk_meta, vd_meta, btab, mtab, lam, subln, zs,
      q, k, km, qk, qk, qk_meta, btab, mtab)


def _merge_kernel(oa_ref, ob_ref, wa_ref, wb_ref, ga_ref, gb_ref, m_ref):
    ya = _dot(oa_ref[...], wa_ref[...])
    yb = _dot(ob_ref[...], wb_ref[...])
    m = ga_ref[...].astype(jnp.float32) * ya + gb_ref[...].astype(jnp.float32) * yb
    m_ref[...] = m.astype(m_ref.dtype)


def _merge(oa, ob, wa, wb, gs, tm, tn):
    rows = oa.shape[0]
    tps = D_MODEL // tn
    return pl.pallas_call(
        _merge_kernel,
        grid=(rows // tm, D_MODEL // tn),
        in_specs=[
            pl.BlockSpec((tm, D_MODEL), lambda i, j: (i, 0)),
            pl.BlockSpec((tm, D_MODEL), lambda i, j: (i, 0)),
            pl.BlockSpec((D_MODEL, tn), lambda i, j: (0, j)),
            pl.BlockSpec((D_MODEL, tn), lambda i, j: (0, j)),
            pl.BlockSpec((tm, tn), lambda i, j: (i, j)),
            pl.BlockSpec((tm, tn), lambda i, j: (i, tps + j)),
        ],
        out_specs=pl.BlockSpec((tm, tn), lambda i, j: (i, j)),
        out_shape=jax.ShapeDtypeStruct((rows, D_MODEL), jnp.bfloat16),
        compiler_params=pltpu.CompilerParams(
            dimension_semantics=("arbitrary", "arbitrary"), vmem_limit_bytes=VMEM_LIMIT_BYTES),
        name="merge",
    )(oa, ob, wa, wb, gs, gs)


def _out_kernel(m_ref, w_ref, x_ref, o_ref):
    o_ref[...] = x_ref[...] + _dot(m_ref[...], w_ref[...])


def _out_proj(m, wout, x2, tm, tn):
    rows = m.shape[0]
    return pl.pallas_call(
        _out_kernel,
        grid=(rows // tm, D_MODEL // tn),
        in_specs=[
            pl.BlockSpec((tm, D_MODEL), lambda i, j: (i, 0)),
            pl.BlockSpec((D_MODEL, tn), lambda i, j: (0, j)),
            pl.BlockSpec((tm, tn), lambda i, j: (i, j)),
        ],
        out_specs=pl.BlockSpec((tm, tn), lambda i, j: (i, j)),
        out_shape=jax.ShapeDtypeStruct((rows, D_MODEL), jnp.float32),
        compiler_params=pltpu.CompilerParams(
            dimension_semantics=("arbitrary", "arbitrary"), vmem_limit_bytes=VMEM_LIMIT_BYTES),
        name="out_proj",
    )(m, wout, x2)


def _t5_bucket(rel):
    nb = REL_BUCKETS // 2
    max_exact = nb // 2
    ret = jnp.where(rel > 0, nb, 0)
    n = jnp.abs(rel)
    nf = jnp.maximum(n, 1).astype(jnp.float32)
    large = max_exact + (jnp.log(nf / max_exact) / math.log(REL_MAX_DIST / max_exact)
                         * (nb - max_exact)).astype(jnp.int32)
    large = jnp.minimum(large, nb - 1)
    return ret + jnp.where(n < max_exact, n, large)


def _bias_tab_kernel(rb_ref, bk_ref, mbk_ref, btab_ref, mtab_ref):
    h = pl.program_id(0)
    bk = bk_ref[...]
    mbk = mbk_ref[...]
    acc = jnp.zeros(bk.shape, jnp.float32)
    macc = jnp.where(mbk < 0, MASK_VALUE, 0.0).astype(jnp.float32)
    for b in range(REL_BUCKETS):
        val = rb_ref[b, h] * LOG2E
        acc = jnp.where(bk == b, val, acc)
        macc = jnp.where(mbk == b, val, macc)
    btab_ref[0] = acc
    mtab_ref[0] = macc


def _bias_tables(rel_bias):
    qq = jnp.arange(TQ, dtype=jnp.int32)[:, None]
    kk = jnp.arange(KC, dtype=jnp.int32)[None, :]
    dd = jnp.arange(-2, 3, dtype=jnp.int32)[:, None, None]
    bk = _t5_bucket(dd * KC + kk[None] - qq[None])
    jm = jnp.arange(META_PAD, dtype=jnp.int32)[None, None, :]
    qpos = N_META + jnp.arange(2, dtype=jnp.int32)[:, None, None] * TQ + qq[None]
    mbk = jnp.where(jm < N_META, _t5_bucket(jm - qpos), -1)
    return pl.pallas_call(
        _bias_tab_kernel,
        grid=(DIFF_HEADS,),
        in_specs=[
            pl.BlockSpec(memory_space=pltpu.SMEM),
            pl.BlockSpec((5, TQ, KC), lambda h: (0, 0, 0)),
            pl.BlockSpec((2, TQ, META_PAD), lambda h: (0, 0, 0)),
        ],
        out_specs=[
            pl.BlockSpec((1, 5, TQ, KC), lambda h: (h, 0, 0, 0)),
            pl.BlockSpec((1, 2, TQ, META_PAD), lambda h: (h, 0, 0, 0)),
        ],
        out_shape=[
            jax.ShapeDtypeStruct((DIFF_HEADS, 5, TQ, KC), jnp.float32),
            jax.ShapeDtypeStruct((DIFF_HEADS, 2, TQ, META_PAD), jnp.float32),
        ],
        compiler_params=pltpu.CompilerParams(dimension_semantics=("arbitrary",)),
        name="bias_tables",
    )(rel_bias.astype(jnp.float32), bk, mbk)


def _rope_tables(n_pos):
    half = MLA_ROPE // 2
    inv = ROPE_THETA ** (-jnp.arange(half, dtype=jnp.float32) / half)
    ang = jnp.arange(n_pos, dtype=jnp.int32).astype(jnp.float32)[:, None] * inv[None, :]
    c, s = jnp.cos(ang), jnp.sin(ang)
    z = jnp.zeros((n_pos, LANES - MLA_ROPE), jnp.float32)
    cos_t = jnp.concatenate([c, c, z], axis=-1)
    sin_t = jnp.concatenate([-s, s, z], axis=-1)
    return cos_t, sin_t


def _pad_rows(a, rows):
    return jnp.pad(a, [(0, rows - a.shape[0])] + [(0, 0)] * (a.ndim - 1))


def kernel(x, meta_tokens, rel_bias, norm_in, w_in, q_a_norm, kv_a_norm, w_uq, w_ukv, mla_q_norm, mla_k_norm,
           diff_q_norm, diff_k_norm, diff_lambda, diff_subln, w_branch_a, w_branch_b, w_out):
    nb, seq, _ = x.shape
    bf = jnp.bfloat16
    n_small = MLA_Q_RANK + MLA_KV_RANK + MLA_ROPE

    w_in_t = jnp.swapaxes(w_in[0], 0, 1)
    wsm = jnp.pad(w_in_t[:n_small], ((0, SMALL_W - n_small), (0, 0))).astype(bf)
    wbig = w_in_t[n_small:].astype(bf)
    wuq = jnp.pad(w_uq[0].reshape(MLA_Q_RANK, MLA_HEADS, MLA_QK),
                  ((0, 0), (0, 0), (0, HEAD_PAD - MLA_QK))).reshape(MLA_Q_RANK, MLA_HEADS * HEAD_PAD).astype(bf)
    wukv = w_ukv[0].reshape(MLA_KV_RANK, MLA_HEADS, MLA_NOPE + MLA_V)
    wuk = wukv[:, :, :MLA_NOPE].reshape(MLA_KV_RANK, MLA_HEADS * MLA_NOPE).astype(bf)
    wvt = jnp.swapaxes(wukv[:, :, MLA_NOPE:].reshape(MLA_KV_RANK, MLA_HEADS * MLA_V), 0, 1).astype(bf)
    wa, wb, wo = w_branch_a[0].astype(bf), w_branch_b[0].astype(bf), w_out[0].astype(bf)

    gin = norm_in[0][None]
    gqa, gkva = q_a_norm[0][None], kv_a_norm[0][None]
    gq = jnp.pad(mla_q_norm[0], (0, HEAD_PAD - MLA_QK))[None]
    gk = jnp.pad(mla_k_norm[0], (0, HEAD_PAD - MLA_QK))[None]
    gqd, gkd = diff_q_norm[0][None], diff_k_norm[0][None]
    subln = diff_subln[0][None]
    lam = diff_lambda[0]

    cos_t, sin_t = _rope_tables(N_META + seq)
    btab, mtab = _bias_tables(rel_bias)

    x2 = x.reshape(nb * seq, D_MODEL)
    meta = meta_tokens.astype(x.dtype)

    qd_scale = LOG2E * DIFF_QK ** -0.5
    reps = D_MODEL // DIFF_QK
    gain_qk = jnp.concatenate([jnp.tile(gqd * qd_scale, (1, reps)), jnp.tile(gkd, (1, reps))], axis=-1)

    q, k, vt, u = _mla_proj(x, gin, wsm, gqa, gkva, wuq, wuk, wvt, gq, gk, cos_t[N_META:], sin_t[N_META:], tm=256)
    u2 = u.reshape(nb * seq, D_MODEL)
    zs = _seg_proj(u2, wbig, (SEG_ZA, SEG_ZB), "silu", 1024, D_MODEL)
    qk = _seg_proj(u2, wbig, (SEG_QD, SEG_KD), "norm", 1024, D_MODEL, gain=gain_qk)
    vd = _seg_proj(u2, wbig, (SEG_VD,), "copy", 1024, D_MODEL)
    gs = _seg_proj(u2, wbig, (SEG_GA, SEG_GB), "sigmoid", 1024, D_MODEL)

    _, km, vmt, um = _mla_proj(meta[None], gin, wsm, gqa, gkva, wuq, wuk, wvt, gq, gk,
                              cos_t[:N_META], sin_t[:N_META], tm=N_META)
    km = jnp.pad(km[0], ((0, 0), (0, META_PAD - N_META), (0, 0)))
    vmt = jnp.pad(vmt[0], ((0, 0), (0, 0), (0, META_PAD - N_META)))
    k_meta = _pad_rows(_seg_proj(um[0], wbig, (SEG_KD,), "norm", N_META, 1024, gain=gain_qk[:, D_MODEL:]), META_PAD)
    vd_meta = _pad_rows(_seg_proj(um[0], wbig, (SEG_VD,), "copy", N_META, 1024), META_PAD)

    oa, ob = _attention(q, k, vt, km, vmt, zs, qk, vd, k_meta, vd_meta, btab, mtab, lam, subln, seq)
    m = _merge(oa, ob, wa, wb, gs, tm=1024, tn=1024)
    out = _out_proj(m, wo, x2, tm=512, tn=D_MODEL)
    return out.reshape(nb, seq, D_MODEL)
```

```python
import functools
import math

import jax
import jax.numpy as jnp
from jax import lax
from jax.experimental import pallas as pl
from jax.experimental.pallas import tpu as pltpu

D_MODEL = 2048
N_META = 16
EPS = 1e-6

MLA_HEADS = 16
MLA_Q_RANK = 768
MLA_KV_RANK = 512
MLA_NOPE = 128
MLA_ROPE = 64
MLA_V = 128
MLA_QK = MLA_NOPE + MLA_ROPE
ROPE_THETA = 10000.0

DIFF_HEADS = 8
DIFF_QK = 128
DIFF_V = 2 * DIFF_QK
LAM_INIT = 0.8 - 0.6 * math.exp(-0.3 * 0)

REL_BUCKETS = 32
REL_MAX_DIST = 128

LANES = 128
MXU_DIM = 256
VMEM_LIMIT_BYTES = 62 * 1024 * 1024

HEAD_PAD = MXU_DIM
SMALL_W = MLA_Q_RANK + MLA_KV_RANK + LANES
META_PAD = LANES
MASK_VALUE = -1e30
TQ = 256
KC = 256
LOG2E = math.log2(math.e)

SEG_ZA, SEG_QD, SEG_KD, SEG_VD, SEG_ZB, SEG_GA, SEG_GB = range(7)


def _dot(a, b):
    return jnp.dot(a, b, preferred_element_type=jnp.float32)


def _dot_nt(a, b):
    return lax.dot_general(a, b, (((1,), (1,)), ((), ())), preferred_element_type=jnp.float32)


def _rms_scale(x, width):
    return lax.rsqrt(jnp.sum(x * x, axis=-1, keepdims=True) * (1.0 / width) + EPS)


def _rot_half64(x):
    return pltpu.roll(x, 32, 1) + pltpu.roll(x, 96, 1)


def _mla_proj_kernel(x_ref, gin_ref, wsm_ref, gqa_ref, gkva_ref, wuq_ref, wuk_ref, wvt_ref,
                     gq_ref, gk_ref, cos_ref, sin_ref, *rest, q_scale):
    if len(rest) == 6:
        rest[5][...] = rest[0][...].astype(jnp.bfloat16)
        rest = rest[1:5]
    q_ref, k_ref, v_ref, u_ref = rest
    x = x_ref[...]
    u = (x * _rms_scale(x, D_MODEL) * gin_ref[...]).astype(jnp.bfloat16)
    u_ref[...] = u
    p = _dot_nt(u, wsm_ref[...])
    cq = p[:, :MLA_Q_RANK]
    ckv = p[:, MLA_Q_RANK:MLA_Q_RANK + MLA_KV_RANK]
    kr = p[:, MLA_Q_RANK + MLA_KV_RANK:]
    cqn = (cq * _rms_scale(cq, MLA_Q_RANK) * gqa_ref[...]).astype(jnp.bfloat16)
    ckvn = (ckv * _rms_scale(ckv, MLA_KV_RANK) * gkva_ref[...]).astype(jnp.bfloat16)
    qf = _dot(cqn, wuq_ref[...])
    kf = _dot(ckvn, wuk_ref[...])
    vt = _dot_nt(wvt_ref[...], ckvn).astype(jnp.bfloat16)

    cos = cos_ref[...]
    sin = sin_ref[...]
    gq = gq_ref[...]
    gk = gk_ref[...]
    gq_nope, gq_rope = gq[:, :LANES], gq[:, LANES:]
    gk_nope, gk_rope = gk[:, :LANES], gk[:, LANES:]

    ss_kr = jnp.sum(kr * kr, axis=-1, keepdims=True)
    krg = kr * gk_rope
    kr_roped = krg * cos + _rot_half64(krg) * sin

    for h in range(MLA_HEADS):
        qh = qf[:, h * HEAD_PAD:(h + 1) * HEAD_PAD]
        rq = _rms_scale(qh, MLA_QK)
        q_nope = qh[:, :LANES] * rq * gq_nope
        q_r = qh[:, LANES:] * rq * gq_rope
        q_r = q_r * cos + _rot_half64(q_r) * sin
        q_ref[0, h, :, :LANES] = (q_nope * q_scale).astype(jnp.bfloat16)
        q_ref[0, h, :, LANES:] = (q_r * q_scale).astype(jnp.bfloat16)

        k_nope = kf[:, h * MLA_NOPE:(h + 1) * MLA_NOPE]
        rk = lax.rsqrt((jnp.sum(k_nope * k_nope, axis=-1, keepdims=True) + ss_kr) * (1.0 / MLA_QK) + EPS)
        k_ref[0, h, :, :LANES] = (k_nope * rk * gk_nope).astype(jnp.bfloat16)
        k_ref[0, h, :, LANES:] = (kr_roped * rk).astype(jnp.bfloat16)
        v_ref[0, h] = vt[h * MLA_V:(h + 1) * MLA_V, :]


def _mla_proj(x3, gin, wsm, gqa, gkva, wuq, wuk, wvt, gq, gk, cos, sin, tm, w_wide=None, wide_row0=0):
    nb, rows, _ = x3.shape
    const = lambda b, i: (0, 0)
    tpb = rows // tm
    kern = functools.partial(_mla_proj_kernel, q_scale=LOG2E * MLA_QK ** -0.5)
    extra_in, extra_out_specs, extra_out_shape, extra_args = [], [], [], []
    if w_wide is not None:
        n_wide = w_wide.shape[0] - wide_row0
        slab = n_wide // (nb * tpb)
        sub = 8
        assert wide_row0 % sub == 0 and slab % sub == 0
        extra_in = [pl.BlockSpec((pl.Element(slab), pl.Element(D_MODEL)),
                                 lambda b, i: ((wide_row0 // sub + (b * tpb + i) * (slab // sub)) * sub, 0))]
        extra_out_specs = [pl.BlockSpec((slab, D_MODEL), lambda b, i: (b * tpb + i, 0))]
        extra_out_shape = [jax.ShapeDtypeStruct((n_wide, D_MODEL), jnp.bfloat16)]
        extra_args = [w_wide]
    return pl.pallas_call(
        kern,
        grid=(nb, rows // tm),
        in_specs=[
            pl.BlockSpec((None, tm, D_MODEL), lambda b, i: (b, i, 0)),
            pl.BlockSpec((1, D_MODEL), const),
            pl.BlockSpec((SMALL_W, D_MODEL), const),
            pl.BlockSpec((1, MLA_Q_RANK), const),
            pl.BlockSpec((1, MLA_KV_RANK), const),
            pl.BlockSpec((MLA_Q_RANK, MLA_HEADS * HEAD_PAD), const),
            pl.BlockSpec((MLA_KV_RANK, MLA_HEADS * MLA_NOPE), const),
            pl.BlockSpec((MLA_HEADS * MLA_V, MLA_KV_RANK), const),
            pl.BlockSpec((1, HEAD_PAD), const),
            pl.BlockSpec((1, HEAD_PAD), const),
            pl.BlockSpec((tm, LANES), lambda b, i: (i, 0)),
            pl.BlockSpec((tm, LANES), lambda b, i: (i, 0)),
        ] + extra_in,
        out_specs=[
            pl.BlockSpec((1, MLA_HEADS, tm, HEAD_PAD), lambda b, i: (b, 0, i, 0)),
            pl.BlockSpec((1, MLA_HEADS, tm, HEAD_PAD), lambda b, i: (b, 0, i, 0)),
            pl.BlockSpec((1, MLA_HEADS, MLA_V, tm), lambda b, i: (b, 0, 0, i)),
            pl.BlockSpec((None, tm, D_MODEL), lambda b, i: (b, i, 0)),
        ] + extra_out_specs,
        out_shape=[
            jax.ShapeDtypeStruct((nb, MLA_HEADS, rows, HEAD_PAD), jnp.bfloat16),
            jax.ShapeDtypeStruct((nb, MLA_HEADS, rows, HEAD_PAD), jnp.bfloat16),
            jax.ShapeDtypeStruct((nb, MLA_HEADS, MLA_V, rows), jnp.bfloat16),
            jax.ShapeDtypeStruct((nb, rows, D_MODEL), jnp.bfloat16),
        ] + extra_out_shape,
        compiler_params=pltpu.CompilerParams(
            dimension_semantics=("arbitrary", "arbitrary"), vmem_limit_bytes=VMEM_LIMIT_BYTES),
        name="mla_proj",
    )(x3, gin, wsm, gqa, gkva, wuq, wuk, wvt, gq, gk, cos, sin, *extra_args)


def _seg_proj_kernel(u_ref, w_ref, *rest, kind):
    o_ref = rest[-1]
    acc = _dot_nt(u_ref[...], w_ref[...])
    if kind == "silu":
        o_ref[...] = (acc / (1.0 + jnp.exp(-acc))).astype(o_ref.dtype)
    elif kind == "sigmoid":
        o_ref[...] = (1.0 / (1.0 + jnp.exp(-acc))).astype(o_ref.dtype)
    elif kind == "copy":
        o_ref[...] = acc.astype(o_ref.dtype)
    else:
        gain = rest[0][...]
        for c in range(acc.shape[1] // DIFF_QK):
            cols = slice(c * DIFF_QK, (c + 1) * DIFF_QK)
            g = acc[:, cols]
            o_ref[:, cols] = (g * _rms_scale(g, DIFF_QK) * gain[:, cols]).astype(o_ref.dtype)


def _seg_proj(u2, wbig, segs, kind, tm, tn, gain=None):
    rows = u2.shape[0]
    tps = D_MODEL // tn
    if len(segs) == 1:
        w_map = lambda i, j: (segs[0] * tps + j, 0)
    else:
        w_map = lambda i, j: (jnp.where(j < tps, segs[0] * tps + j, segs[1] * tps + j - tps), 0)
    in_specs = [pl.BlockSpec((tm, D_MODEL), lambda i, j: (i, 0)), pl.BlockSpec((tn, D_MODEL), w_map)]
    args = [u2, wbig]
    if gain is not None:
        in_specs.append(pl.BlockSpec((1, tn), lambda i, j: (0, j)))
        args.append(gain)
    return pl.pallas_call(
        functools.partial(_seg_proj_kernel, kind=kind),
        grid=(rows // tm, len(segs) * tps),
        in_specs=in_specs,
        out_specs=pl.BlockSpec((tm, tn), lambda i, j: (i, j)),
        out_shape=jax.ShapeDtypeStruct((rows, len(segs) * D_MODEL), jnp.bfloat16),
        compiler_params=pltpu.CompilerParams(
            dimension_semantics=("arbitrary", "arbitrary"), vmem_limit_bytes=VMEM_LIMIT_BYTES),
        name="in_proj_" + kind,
    )(*args)


MLA_PER_STEP = MLA_HEADS // DIFF_HEADS


def _lane_fold_max(x):
    blocks = [x[:, i * LANES:(i + 1) * LANES] for i in range(x.shape[1] // LANES)]
    return functools.reduce(jnp.maximum, blocks)


ONES_ROWS = 16


def _mla_tile_fns(q_ref, k_ref, vt_ref, km_ref, vmt_ref, za_ref, qn_ref, kn_ref, kmn_ref,
                  o_ref, s_ref, sm_ref, mx_ref):
    n_t = q_ref.shape[2] // TQ

    def scores_into(slot, qt, k, km):
        s = _dot_nt(k, qt)
        sm = _dot_nt(km, qt)
        key = lax.broadcasted_iota(jnp.int32, sm.shape, 0)
        sm = jnp.where(key < N_META, sm, MASK_VALUE)
        s_ref[slot] = s
        sm_ref[slot] = sm
        mx_ref[slot] = jnp.maximum(jnp.max(s, axis=0, keepdims=True), jnp.max(sm, axis=0, keepdims=True))

    def scores(u, slot):
        hh, t = divmod(u, n_t)
        scores_into(slot, q_ref[0, hh, t * TQ:(t + 1) * TQ, :], k_ref[0, hh], km_ref[hh])

    def scores_next(slot):
        scores_into(slot, qn_ref[0, 0], kn_ref[0, 0], kmn_ref[0])

    def finish(u, slot):
        hh, t = divmod(u, n_t)
        rows = slice(t * TQ, (t + 1) * TQ)
        cols = slice(hh * MLA_V, (hh + 1) * MLA_V)
        m = mx_ref[slot]
        p = jnp.exp2(s_ref[slot] - m).astype(jnp.bfloat16)
        pm = jnp.exp2(sm_ref[slot] - m).astype(jnp.bfloat16)
        ones = jnp.ones((ONES_ROWS, vt_ref.shape[3]), jnp.bfloat16)
        ones_m = jnp.ones((ONES_ROWS, META_PAD), jnp.bfloat16)
        o = (_dot(jnp.concatenate([vt_ref[0, hh], ones], axis=0), p)
             + _dot(jnp.concatenate([vmt_ref[hh], ones_m], axis=0), pm))
        l = o[MLA_V:MLA_V + 1, :]
        y = (o[:MLA_V, :] * (1.0 / l)).T
        o_ref[rows, cols] = (y * za_ref[rows, cols].astype(jnp.float32)).astype(o_ref.dtype)

    return scores, scores_next, finish


def _diff_tile_fns(qd_ref, kd_ref, vd_ref, kdm_ref, vdm_ref, btab_ref, mtab_ref, lam_ref, subln_ref, zb_ref,
                   qdn_ref, kdn_ref, kdmn_ref, btabn_ref, mtabn_ref, o_ref, s_ref, sm_ref, mx_ref):
    n_chunks = kd_ref.shape[0] // KC
    lv = lam_ref[...]
    lam = (jnp.exp(jnp.sum(lv[0:1] * lv[1:2], axis=-1, keepdims=True))
           - jnp.exp(jnp.sum(lv[2:3] * lv[3:4], axis=-1, keepdims=True)) + LAM_INIT)

    c_neg = btab_ref[0, 0, 0:1, 0:1]
    c_pos = btab_ref[0, 4, 0:1, 0:1]

    def scores_into(t, slot, q_tile_ref, k_ref_, km_ref_, bt_ref, mt_ref):
        cn, cp = bt_ref[0, 0, 0:1, 0:1], bt_ref[0, 4, 0:1, 0:1]
        for half in range(2):
            lo = half * DIFF_QK
            qm = q_tile_ref[:, lo:lo + DIFF_QK]
            sm = _dot_nt(qm, km_ref_[:, lo:lo + DIFF_QK]) + mt_ref[0, min(t, 1)]
            sm_ref[slot, half] = sm
            mx = {"band": sm, "neg": None, "pos": None}
            for c in range(n_chunks):
                sc = _dot_nt(qm, k_ref_[c * KC:(c + 1) * KC, lo:lo + DIFF_QK])
                if abs(c - t) <= 1:
                    sc = sc + bt_ref[0, c - t + 2]
                    grp = "band"
                else:
                    grp = "neg" if c < t else "pos"
                s_ref[slot, half, :, c * KC:(c + 1) * KC] = sc
                fold = _lane_fold_max(sc)
                mx[grp] = fold if mx[grp] is None else jnp.maximum(mx[grp], fold)
            acc = mx["band"]
            if mx["neg"] is not None:
                acc = jnp.maximum(acc, mx["neg"] + cn)
            if mx["pos"] is not None:
                acc = jnp.maximum(acc, mx["pos"] + cp)
            mx_ref[slot, half] = acc

    def scores(t, slot):
        scores_into(t, slot, qd_ref.at[t * TQ:(t + 1) * TQ], kd_ref, kdm_ref, btab_ref, mtab_ref)

    def scores_next(slot):
        scores_into(0, slot, qdn_ref, kdn_ref, kdmn_ref, btabn_ref, mtabn_ref)

    def softmax_parts(t, slot, half):
        m = jnp.max(mx_ref[slot, half], axis=-1, keepdims=True)
        m_neg = m - c_neg
        m_pos = m - c_pos
        chunks = []
        for c in range(n_chunks):
            mc = m if abs(c - t) <= 1 else (m_neg if c < t else m_pos)
            chunks.append(jnp.exp2(s_ref[slot, half, :, c * KC:(c + 1) * KC] - mc))
        p = jnp.concatenate(chunks, axis=-1)
        pm = jnp.exp2(sm_ref[slot, half] - m)
        l = jnp.sum(p, axis=-1, keepdims=True) + jnp.sum(pm, axis=-1, keepdims=True)
        return p, pm, l

    def finish(t, slot):
        rows = slice(t * TQ, (t + 1) * TQ)
        p1, pm1, l1 = softmax_parts(t, slot, 0)
        p2, pm2, l2 = softmax_parts(t, slot, 1)
        r = lam * l1 * (1.0 / l2)
        a = (p1 - p2 * r).astype(jnp.bfloat16)
        am = (pm1 - pm2 * r).astype(jnp.bfloat16)
        o = (_dot(a, vd_ref[...]) + _dot(am, vdm_ref[...])) * (1.0 / l1)
        y = o * _rms_scale(o, DIFF_V) * subln_ref[...] * (1.0 - LAM_INIT)
        y = y * zb_ref[rows, :].astype(jnp.float32)
        o_ref[rows, :] = y.astype(o_ref.dtype)

    return scores, scores_next, finish


def _attn_kernel(q_ref, k_ref, vt_ref, km_ref, vmt_ref, za_ref,
                 qd_ref, kd_ref, vd_ref, kdm_ref, vdm_ref, btab_ref, mtab_ref, lam_ref, subln_ref, zb_ref,
                 qn_ref, kn_ref, kmn_ref, qdn_ref, kdn_ref, kdmn_ref, btabn_ref, mtabn_ref,
                 oa_ref, ob_ref, ms_ref, msm_ref, mmx_ref, ds_ref, dsm_ref, dmx_ref):
    m_scores, m_scores_next, m_finish = _mla_tile_fns(
        q_ref, k_ref, vt_ref, km_ref, vmt_ref, za_ref, qn_ref, kn_ref, kmn_ref, oa_ref,
        ms_ref, msm_ref, mmx_ref)
    d_scores, d_scores_next, d_finish = _diff_tile_fns(
        qd_ref, kd_ref, vd_ref, kdm_ref, vdm_ref, btab_ref, mtab_ref, lam_ref, subln_ref, zb_ref,
        qdn_ref, kdn_ref, kdmn_ref, btabn_ref, mtabn_ref, ob_ref, ds_ref, dsm_ref, dmx_ref)
    n_t = qd_ref.shape[0] // TQ
    n_u = n_t * MLA_PER_STEP

    @pl.when((pl.program_id(0) == 0) & (pl.program_id(1) == 0))
    def _():
        d_scores(0, 0)
        m_scores(0, 0)

    for t in range(n_t):
        if t + 1 < n_t:
            d_scores(t + 1, (t + 1) % 2)
        else:
            d_scores_next(0)
        for u in range(t * MLA_PER_STEP, (t + 1) * MLA_PER_STEP):
            if u + 1 < n_u:
                m_scores(u + 1, (u + 1) % 2)
            else:
                m_scores_next(0)
            m_finish(u, u % 2)
        d_finish(t, t % 2)


def _attention(q, k, vt, km, vmt, zs, qk, vd, qk_meta, vd_meta, btab, mtab, lam, subln, seq):
    nb = q.shape[0]
    cps = D_MODEL // DIFF_V
    mps = MLA_PER_STEP
    n_steps = nb * DIFF_HEADS

    def nxt(b, h):
        g = jnp.minimum(b * DIFF_HEADS + h + 1, n_steps - 1)
        return g // DIFF_HEADS, g % DIFF_HEADS

    def nb_(b, h):
        return nxt(b, h)[0]

    def nh_(b, h):
        return nxt(b, h)[1]

    return pl.pallas_call(
        _attn_kernel,
        grid=(nb, DIFF_HEADS),
        in_specs=[
            pl.BlockSpec((1, mps, seq, HEAD_PAD), lambda b, h: (b, h, 0, 0)),
            pl.BlockSpec((1, mps, seq, HEAD_PAD), lambda b, h: (b, h, 0, 0)),
            pl.BlockSpec((1, mps, MLA_V, seq), lambda b, h: (b, h, 0, 0)),
            pl.BlockSpec((mps, META_PAD, HEAD_PAD), lambda b, h: (h, 0, 0)),
            pl.BlockSpec((mps, MLA_V, META_PAD), lambda b, h: (h, 0, 0)),
            pl.BlockSpec((seq, mps * MLA_V), lambda b, h: (b, h)),
            pl.BlockSpec((seq, DIFF_V), lambda b, h: (b, h)),
            pl.BlockSpec((seq, DIFF_V), lambda b, h: (b, cps + h)),
            pl.BlockSpec((seq, DIFF_V), lambda b, h: (b, h)),
            pl.BlockSpec((META_PAD, DIFF_V), lambda b, h: (0, h)),
            pl.BlockSpec((META_PAD, DIFF_V), lambda b, h: (0, h)),
            pl.BlockSpec((1, 5, TQ, KC), lambda b, h: (h, 0, 0, 0)),
            pl.BlockSpec((1, 2, TQ, META_PAD), lambda b, h: (h, 0, 0, 0)),
            pl.BlockSpec((4, DIFF_QK), lambda b, h: (0, 0)),
            pl.BlockSpec((1, DIFF_V), lambda b, h: (0, 0)),
            pl.BlockSpec((seq, DIFF_V), lambda b, h: (b, cps + h)),
            pl.BlockSpec((1, 1, TQ, HEAD_PAD), lambda b, h: (nb_(b, h), mps * nh_(b, h), 0, 0)),
            pl.BlockSpec((1, 1, seq, HEAD_PAD), lambda b, h: (nb_(b, h), mps * nh_(b, h), 0, 0)),
            pl.BlockSpec((1, META_PAD, HEAD_PAD), lambda b, h: (mps * nh_(b, h), 0, 0)),
            pl.BlockSpec((TQ, DIFF_V), lambda b, h: (nb_(b, h) * (seq // TQ), nh_(b, h))),
            pl.BlockSpec((seq, DIFF_V), lambda b, h: (nb_(b, h), cps + nh_(b, h))),
            pl.BlockSpec((META_PAD, DIFF_V), lambda b, h: (0, nh_(b, h))),
            pl.BlockSpec((1, 5, TQ, KC), lambda b, h: (nh_(b, h), 0, 0, 0)),
            pl.BlockSpec((1, 2, TQ, META_PAD), lambda b, h: (nh_(b, h), 0, 0, 0)),
        ],
        out_specs=[
            pl.BlockSpec((seq, mps * MLA_V), lambda b, h: (b, h)),
            pl.BlockSpec((seq, DIFF_V), lambda b, h: (b, h)),
        ],
        out_shape=[
            jax.ShapeDtypeStruct((nb * seq, MLA_HEADS * MLA_V), jnp.bfloat16),
            jax.ShapeDtypeStruct((nb * seq, DIFF_HEADS * DIFF_V), jnp.bfloat16),
        ],
        scratch_shapes=[
            pltpu.VMEM((2, seq, TQ), jnp.float32), pltpu.VMEM((2, META_PAD, TQ), jnp.float32),
            pltpu.VMEM((2, 1, TQ), jnp.float32),
            pltpu.VMEM((2, 2, TQ, seq), jnp.float32), pltpu.VMEM((2, 2, TQ, META_PAD), jnp.float32),
            pltpu.VMEM((2, 2, TQ, LANES), jnp.float32),
        ],
        compiler_params=pltpu.CompilerParams(
            dimension_semantics=("arbitrary", "arbitrary"), vmem_limit_bytes=VMEM_LIMIT_BYTES),
        name="attention",
    )(q, k, vt, km, vmt, zs, qk, qk, vd, qk_meta, vd_meta, btab, mtab, lam, subln, zs,
      q, k, km, qk, qk, qk_meta, btab, mtab)


def _merge_kernel(oa_ref, ob_ref, wa_ref, wb_ref, ga_ref, gb_ref, m_ref):
    ya = _dot(oa_ref[...], wa_ref[...])
    yb = _dot(ob_ref[...], wb_ref[...])
    m = ga_ref[...].astype(jnp.float32) * ya + gb_ref[...].astype(jnp.float32) * yb
    m_ref[...] = m.astype(m_ref.dtype)


def _merge(oa, ob, wa, wb, gs, tm, tn):
    rows = oa.shape[0]
    tps = D_MODEL // tn
    return pl.pallas_call(
        _merge_kernel,
        grid=(rows // tm, D_MODEL // tn),
        in_specs=[
            pl.BlockSpec((tm, D_MODEL), lambda i, j: (i, 0)),
            pl.BlockSpec((tm, D_MODEL), lambda i, j: (i, 0)),
            pl.BlockSpec((D_MODEL, tn), lambda i, j: (0, j)),
            pl.BlockSpec((D_MODEL, tn), lambda i, j: (0, j)),
            pl.BlockSpec((tm, tn), lambda i, j: (i, j)),
            pl.BlockSpec((tm, tn), lambda i, j: (i, tps + j)),
        ],
        out_specs=pl.BlockSpec((tm, tn), lambda i, j: (i, j)),
        out_shape=jax.ShapeDtypeStruct((rows, D_MODEL), jnp.bfloat16),
        compiler_params=pltpu.CompilerParams(
            dimension_semantics=("arbitrary", "arbitrary"), vmem_limit_bytes=VMEM_LIMIT_BYTES),
        name="merge",
    )(oa, ob, wa, wb, gs, gs)


def _out_kernel(m_ref, w_ref, x_ref, o_ref):
    o_ref[...] = x_ref[...] + _dot(m_ref[...], w_ref[...])


def _out_proj(m, wout, x2, tm, tn):
    rows = m.shape[0]
    return pl.pallas_call(
        _out_kernel,
        grid=(rows // tm, D_MODEL // tn),
        in_specs=[
            pl.BlockSpec((tm, D_MODEL), lambda i, j: (i, 0)),
            pl.BlockSpec((D_MODEL, tn), lambda i, j: (0, j)),
            pl.BlockSpec((tm, tn), lambda i, j: (i, j)),
        ],
        out_specs=pl.BlockSpec((tm, tn), lambda i, j: (i, j)),
        out_shape=jax.ShapeDtypeStruct((rows, D_MODEL), jnp.float32),
        compiler_params=pltpu.CompilerParams(
            dimension_semantics=("arbitrary", "arbitrary"), vmem_limit_bytes=VMEM_LIMIT_BYTES),
        name="out_proj",
    )(m, wout, x2)


def _t5_bucket(rel):
    nb = REL_BUCKETS // 2
    max_exact = nb // 2
    ret = jnp.where(rel > 0, nb, 0)
    n = jnp.abs(rel)
    nf = jnp.maximum(n, 1).astype(jnp.float32)
    large = max_exact + (jnp.log(nf / max_exact) / math.log(REL_MAX_DIST / max_exact)
                         * (nb - max_exact)).astype(jnp.int32)
    large = jnp.minimum(large, nb - 1)
    return ret + jnp.where(n < max_exact, n, large)


def _bias_tab_kernel(rb_ref, bk_ref, mbk_ref, btab_ref, mtab_ref):
    h = pl.program_id(0)
    bk = bk_ref[...]
    mbk = mbk_ref[...]
    acc = jnp.zeros(bk.shape, jnp.float32)
    macc = jnp.where(mbk < 0, MASK_VALUE, 0.0).astype(jnp.float32)
    for b in range(REL_BUCKETS):
        val = rb_ref[b, h] * LOG2E
        acc = jnp.where(bk == b, val, acc)
        macc = jnp.where(mbk == b, val, macc)
    btab_ref[0] = acc
    mtab_ref[0] = macc


def _bias_tables(rel_bias):
    qq = jnp.arange(TQ, dtype=jnp.int32)[:, None]
    kk = jnp.arange(KC, dtype=jnp.int32)[None, :]
    dd = jnp.arange(-2, 3, dtype=jnp.int32)[:, None, None]
    bk = _t5_bucket(dd * KC + kk[None] - qq[None])
    jm = jnp.arange(META_PAD, dtype=jnp.int32)[None, None, :]
    qpos = N_META + jnp.arange(2, dtype=jnp.int32)[:, None, None] * TQ + qq[None]
    mbk = jnp.where(jm < N_META, _t5_bucket(jm - qpos), -1)
    return pl.pallas_call(
        _bias_tab_kernel,
        grid=(DIFF_HEADS,),
        in_specs=[
            pl.BlockSpec(memory_space=pltpu.SMEM),
            pl.BlockSpec((5, TQ, KC), lambda h: (0, 0, 0)),
            pl.BlockSpec((2, TQ, META_PAD), lambda h: (0, 0, 0)),
        ],
        out_specs=[
            pl.BlockSpec((1, 5, TQ, KC), lambda h: (h, 0, 0, 0)),
            pl.BlockSpec((1, 2, TQ, META_PAD), lambda h: (h, 0, 0, 0)),
        ],
        out_shape=[
            jax.ShapeDtypeStruct((DIFF_HEADS, 5, TQ, KC), jnp.float32),
            jax.ShapeDtypeStruct((DIFF_HEADS, 2, TQ, META_PAD), jnp.float32),
        ],
        compiler_params=pltpu.CompilerParams(dimension_semantics=("arbitrary",)),
        name="bias_tables",
    )(rel_bias.astype(jnp.float32), bk, mbk)


def _rope_tables(n_pos):
    half = MLA_ROPE // 2
    inv = ROPE_THETA ** (-jnp.arange(half, dtype=jnp.float32) / half)
    ang = jnp.arange(n_pos, dtype=jnp.int32).astype(jnp.float32)[:, None] * inv[None, :]
    c, s = jnp.cos(ang), jnp.sin(ang)
    z = jnp.zeros((n_pos, LANES - MLA_ROPE), jnp.float32)
    cos_t = jnp.concatenate([c, c, z], axis=-1)
    sin_t = jnp.concatenate([-s, s, z], axis=-1)
    return cos_t, sin_t


def _pad_rows(a, rows):
    return jnp.pad(a, [(0, rows - a.shape[0])] + [(0, 0)] * (a.ndim - 1))


def kernel(x, meta_tokens, rel_bias, norm_in, w_in, q_a_norm, kv_a_norm, w_uq, w_ukv, mla_q_norm, mla_k_norm,
           diff_q_norm, diff_k_norm, diff_lambda, diff_subln, w_branch_a, w_branch_b, w_out):
    nb, seq, _ = x.shape
    bf = jnp.bfloat16
    n_small = MLA_Q_RANK + MLA_KV_RANK + MLA_ROPE

    w_in_t = jnp.swapaxes(w_in[0], 0, 1)
    wsm = jnp.pad(w_in_t[:n_small], ((0, SMALL_W - n_small), (0, 0))).astype(bf)
    wuq = jnp.pad(w_uq[0].reshape(MLA_Q_RANK, MLA_HEADS, MLA_QK),
                  ((0, 0), (0, 0), (0, HEAD_PAD - MLA_QK))).reshape(MLA_Q_RANK, MLA_HEADS * HEAD_PAD).astype(bf)
    wukv = w_ukv[0].reshape(MLA_KV_RANK, MLA_HEADS, MLA_NOPE + MLA_V)
    wuk = wukv[:, :, :MLA_NOPE].reshape(MLA_KV_RANK, MLA_HEADS * MLA_NOPE).astype(bf)
    wvt = jnp.swapaxes(wukv[:, :, MLA_NOPE:].reshape(MLA_KV_RANK, MLA_HEADS * MLA_V), 0, 1).astype(bf)
    wa, wb, wo = w_branch_a[0].astype(bf), w_branch_b[0].astype(bf), w_out[0].astype(bf)

    gin = norm_in[0][None]
    gqa, gkva = q_a_norm[0][None], kv_a_norm[0][None]
    gq = jnp.pad(mla_q_norm[0], (0, HEAD_PAD - MLA_QK))[None]
    gk = jnp.pad(mla_k_norm[0], (0, HEAD_PAD - MLA_QK))[None]
    gqd, gkd = diff_q_norm[0][None], diff_k_norm[0][None]
    subln = diff_subln[0][None]
    lam = diff_lambda[0]

    cos_t, sin_t = _rope_tables(N_META + seq)
    btab, mtab = _bias_tables(rel_bias)

    x2 = x.reshape(nb * seq, D_MODEL)
    meta = meta_tokens.astype(x.dtype)

    qd_scale = LOG2E * DIFF_QK ** -0.5
    reps = D_MODEL // DIFF_QK
    gain_qk = jnp.concatenate([jnp.tile(gqd * qd_scale, (1, reps)), jnp.tile(gkd, (1, reps))], axis=-1)

    q, k, vt, u, wbig = _mla_proj(x, gin, wsm, gqa, gkva, wuq, wuk, wvt, gq, gk, cos_t[N_META:], sin_t[N_META:],
                                  tm=256, w_wide=w_in_t, wide_row0=n_small)
    u2 = u.reshape(nb * seq, D_MODEL)
    zs = _seg_proj(u2, wbig, (SEG_ZA, SEG_ZB), "silu", 1024, D_MODEL)
    qk = _seg_proj(u2, wbig, (SEG_QD, SEG_KD), "norm", 1024, D_MODEL, gain=gain_qk)
    vd = _seg_proj(u2, wbig, (SEG_VD,), "copy", 1024, D_MODEL)
    gs = _seg_proj(u2, wbig, (SEG_GA, SEG_GB), "sigmoid", 1024, D_MODEL)

    _, km, vmt, um = _mla_proj(meta[None], gin, wsm, gqa, gkva, wuq, wuk, wvt, gq, gk,
                              cos_t[:N_META], sin_t[:N_META], tm=N_META)
    km = jnp.pad(km[0], ((0, 0), (0, META_PAD - N_META), (0, 0)))
    vmt = jnp.pad(vmt[0], ((0, 0), (0, 0), (0, META_PAD - N_META)))
    k_meta = _pad_rows(_seg_proj(um[0], wbig, (SEG_KD,), "norm", N_META, 1024, gain=gain_qk[:, D_MODEL:]), META_PAD)
    vd_meta = _pad_rows(_seg_proj(um[0], wbig, (SEG_VD,), "copy", N_META, 1024), META_PAD)

    oa, ob = _attention(q, k, vt, km, vmt, zs, qk, vd, k_meta, vd_meta, btab, mtab, lam, subln, seq)
    m = _merge(oa, ob, wa, wb, gs, tm=1024, tn=1024)
    out = _out_proj(m, wo, x2, tm=512, tn=D_MODEL)
    return out.reshape(nb, seq, D_MODEL)
```

```python
import functools
import math

import jax
import jax.numpy as jnp
from jax import lax
from jax.experimental import pallas as pl
from jax.experimental.pallas import tpu as pltpu

D_MODEL = 2048
N_META = 16
EPS = 1e-6

MLA_HEADS = 16
MLA_Q_RANK = 768
MLA_KV_RANK = 512
MLA_NOPE = 128
MLA_ROPE = 64
MLA_V = 128
MLA_QK = MLA_NOPE + MLA_ROPE
ROPE_THETA = 10000.0

DIFF_HEADS = 8
DIFF_QK = 128
DIFF_V = 2 * DIFF_QK
LAM_INIT = 0.8 - 0.6 * math.exp(-0.3 * 0)

REL_BUCKETS = 32
REL_MAX_DIST = 128

LANES = 128
MXU_DIM = 256
VMEM_LIMIT_BYTES = 62 * 1024 * 1024

HEAD_PAD = MXU_DIM
SMALL_W = MLA_Q_RANK + MLA_KV_RANK + LANES
META_PAD = LANES
MASK_VALUE = -1e30
TQ = 256
KC = 256
LOG2E = math.log2(math.e)

SEG_ZA, SEG_QD, SEG_KD, SEG_VD, SEG_ZB, SEG_GA, SEG_GB = range(7)


def _dot(a, b):
    return jnp.dot(a, b, preferred_element_type=jnp.float32)


def _dot_nt(a, b):
    return lax.dot_general(a, b, (((1,), (1,)), ((), ())), preferred_element_type=jnp.float32)


def _rms_scale(x, width):
    return lax.rsqrt(jnp.sum(x * x, axis=-1, keepdims=True) * (1.0 / width) + EPS)


def _rot_half64(x):
    return pltpu.roll(x, 32, 1) + pltpu.roll(x, 96, 1)


def _mla_proj_kernel(x_ref, gin_ref, wsm_ref, gqa_ref, gkva_ref, wuq_ref, wuk_ref, wvt_ref,
                     gq_ref, gk_ref, cos_ref, sin_ref, *rest, q_scale):
    if len(rest) == 6:
        rest[5][...] = rest[0][...].astype(jnp.bfloat16)
        rest = rest[1:5]
    q_ref, k_ref, v_ref, u_ref = rest
    x = x_ref[...]
    u = (x * _rms_scale(x, D_MODEL) * gin_ref[...]).astype(jnp.bfloat16)
    u_ref[...] = u
    p = _dot_nt(u, wsm_ref[...])
    cq = p[:, :MLA_Q_RANK]
    ckv = p[:, MLA_Q_RANK:MLA_Q_RANK + MLA_KV_RANK]
    kr = p[:, MLA_Q_RANK + MLA_KV_RANK:]
    cqn = (cq * _rms_scale(cq, MLA_Q_RANK) * gqa_ref[...]).astype(jnp.bfloat16)
    ckvn = (ckv * _rms_scale(ckv, MLA_KV_RANK) * gkva_ref[...]).astype(jnp.bfloat16)
    qf = _dot(cqn, wuq_ref[...])
    kf = _dot(ckvn, wuk_ref[...])
    vt = _dot_nt(wvt_ref[...], ckvn).astype(jnp.bfloat16)

    cos = cos_ref[...]
    sin = sin_ref[...]
    gq = gq_ref[...]
    gk = gk_ref[...]
    gq_nope, gq_rope = gq[:, :LANES], gq[:, LANES:]
    gk_nope, gk_rope = gk[:, :LANES], gk[:, LANES:]

    ss_kr = jnp.sum(kr * kr, axis=-1, keepdims=True)
    krg = kr * gk_rope
    kr_roped = krg * cos + _rot_half64(krg) * sin

    for h in range(MLA_HEADS):
        qh = qf[:, h * HEAD_PAD:(h + 1) * HEAD_PAD]
        rq = _rms_scale(qh, MLA_QK)
        q_nope = qh[:, :LANES] * rq * gq_nope
        q_r = qh[:, LANES:] * rq * gq_rope
        q_r = q_r * cos + _rot_half64(q_r) * sin
        q_ref[0, h, :, :LANES] = (q_nope * q_scale).astype(jnp.bfloat16)
        q_ref[0, h, :, LANES:] = (q_r * q_scale).astype(jnp.bfloat16)

        k_nope = kf[:, h * MLA_NOPE:(h + 1) * MLA_NOPE]
        rk = lax.rsqrt((jnp.sum(k_nope * k_nope, axis=-1, keepdims=True) + ss_kr) * (1.0 / MLA_QK) + EPS)
        k_ref[0, h, :, :LANES] = (k_nope * rk * gk_nope).astype(jnp.bfloat16)
        k_ref[0, h, :, LANES:] = (kr_roped * rk).astype(jnp.bfloat16)
        v_ref[0, h] = vt[h * MLA_V:(h + 1) * MLA_V, :]


def _mla_proj(x3, gin, wsm, gqa, gkva, wuq, wuk, wvt, gq, gk, cos, sin, tm, w_wide=None, wide_row0=0):
    nb, rows, _ = x3.shape
    const = lambda b, i: (0, 0)
    tpb = rows // tm
    kern = functools.partial(_mla_proj_kernel, q_scale=LOG2E * MLA_QK ** -0.5)
    extra_in, extra_out_specs, extra_out_shape, extra_args = [], [], [], []
    if w_wide is not None:
        n_wide = w_wide.shape[0] - wide_row0
        slab = n_wide // (nb * tpb)
        sub = 8
        assert wide_row0 % sub == 0 and slab % sub == 0
        extra_in = [pl.BlockSpec((pl.Element(slab), pl.Element(D_MODEL)),
                                 lambda b, i: ((wide_row0 // sub + (b * tpb + i) * (slab // sub)) * sub, 0))]
        extra_out_specs = [pl.BlockSpec((slab, D_MODEL), lambda b, i: (b * tpb + i, 0))]
        extra_out_shape = [jax.ShapeDtypeStruct((n_wide, D_MODEL), jnp.bfloat16)]
        extra_args = [w_wide]
    return pl.pallas_call(
        kern,
        grid=(nb, rows // tm),
        in_specs=[
            pl.BlockSpec((None, tm, D_MODEL), lambda b, i: (b, i, 0)),
            pl.BlockSpec((1, D_MODEL), const),
            pl.BlockSpec((SMALL_W, D_MODEL), const),
            pl.BlockSpec((1, MLA_Q_RANK), const),
            pl.BlockSpec((1, MLA_KV_RANK), const),
            pl.BlockSpec((MLA_Q_RANK, MLA_HEADS * HEAD_PAD), const),
            pl.BlockSpec((MLA_KV_RANK, MLA_HEADS * MLA_NOPE), const),
            pl.BlockSpec((MLA_HEADS * MLA_V, MLA_KV_RANK), const),
            pl.BlockSpec((1, HEAD_PAD), const),
            pl.BlockSpec((1, HEAD_PAD), const),
            pl.BlockSpec((tm, LANES), lambda b, i: (i, 0)),
            pl.BlockSpec((tm, LANES), lambda b, i: (i, 0)),
        ] + extra_in,
        out_specs=[
            pl.BlockSpec((1, MLA_HEADS, tm, HEAD_PAD), lambda b, i: (b, 0, i, 0)),
            pl.BlockSpec((1, MLA_HEADS, tm, HEAD_PAD), lambda b, i: (b, 0, i, 0)),
            pl.BlockSpec((1, MLA_HEADS, MLA_V, tm), lambda b, i: (b, 0, 0, i)),
            pl.BlockSpec((None, tm, D_MODEL), lambda b, i: (b, i, 0)),
        ] + extra_out_specs,
        out_shape=[
            jax.ShapeDtypeStruct((nb, MLA_HEADS, rows, HEAD_PAD), jnp.bfloat16),
            jax.ShapeDtypeStruct((nb, MLA_HEADS, rows, HEAD_PAD), jnp.bfloat16),
            jax.ShapeDtypeStruct((nb, MLA_HEADS, MLA_V, rows), jnp.bfloat16),
            jax.ShapeDtypeStruct((nb, rows, D_MODEL), jnp.bfloat16),
        ] + extra_out_shape,
        compiler_params=pltpu.CompilerParams(
            dimension_semantics=("arbitrary", "arbitrary"), vmem_limit_bytes=VMEM_LIMIT_BYTES),
        name="mla_proj",
    )(x3, gin, wsm, gqa, gkva, wuq, wuk, wvt, gq, gk, cos, sin, *extra_args)


def _seg_proj_kernel(u_ref, w_ref, *rest, kind, has_gain, has_cast):
    rest = list(rest)
    gain_ref = rest.pop(0) if has_gain else None
    cast_src_ref = rest.pop(0) if has_cast else None
    o_ref = rest.pop(0)
    if has_cast:
        rest.pop(0)[...] = cast_src_ref[...].astype(jnp.bfloat16)
    acc = _dot_nt(u_ref[...], w_ref[...])
    if kind == "silu":
        o_ref[...] = (acc / (1.0 + jnp.exp(-acc))).astype(o_ref.dtype)
    elif kind == "sigmoid":
        o_ref[...] = (1.0 / (1.0 + jnp.exp(-acc))).astype(o_ref.dtype)
    elif kind == "copy":
        o_ref[...] = acc.astype(o_ref.dtype)
    else:
        gain = gain_ref[...]
        for c in range(acc.shape[1] // DIFF_QK):
            cols = slice(c * DIFF_QK, (c + 1) * DIFF_QK)
            g = acc[:, cols]
            o_ref[:, cols] = (g * _rms_scale(g, DIFF_QK) * gain[:, cols]).astype(o_ref.dtype)


def _seg_proj(u2, wbig, segs, kind, tm, tn, gain=None, cast_src=None):
    rows = u2.shape[0]
    tps = D_MODEL // tn
    nj = len(segs) * tps
    if len(segs) == 1:
        w_map = lambda i, j: (segs[0] * tps + j, 0)
    else:
        w_map = lambda i, j: (jnp.where(j < tps, segs[0] * tps + j, segs[1] * tps + j - tps), 0)
    in_specs = [pl.BlockSpec((tm, D_MODEL), lambda i, j: (i, 0)), pl.BlockSpec((tn, D_MODEL), w_map)]
    args = [u2, wbig]
    out_specs = [pl.BlockSpec((tm, tn), lambda i, j: (i, j))]
    out_shape = [jax.ShapeDtypeStruct((rows, len(segs) * D_MODEL), jnp.bfloat16)]
    if gain is not None:
        in_specs.append(pl.BlockSpec((1, tn), lambda i, j: (0, j)))
        args.append(gain)
    if cast_src is not None:
        slab = cast_src.shape[0] // ((rows // tm) * nj)
        slab_spec = pl.BlockSpec((slab, cast_src.shape[1]), lambda i, j: (i * nj + j, 0))
        in_specs.append(slab_spec)
        args.append(cast_src)
        out_specs.append(slab_spec)
        out_shape.append(jax.ShapeDtypeStruct(cast_src.shape, jnp.bfloat16))
    res = pl.pallas_call(
        functools.partial(_seg_proj_kernel, kind=kind, has_gain=gain is not None, has_cast=cast_src is not None),
        grid=(rows // tm, nj),
        in_specs=in_specs,
        out_specs=out_specs,
        out_shape=out_shape,
        compiler_params=pltpu.CompilerParams(
            dimension_semantics=("arbitrary", "arbitrary"), vmem_limit_bytes=VMEM_LIMIT_BYTES),
        name="in_proj_" + kind,
    )(*args)
    return res if cast_src is not None else res[0]


MLA_PER_STEP = MLA_HEADS // DIFF_HEADS


def _lane_fold_max(x):
    blocks = [x[:, i * LANES:(i + 1) * LANES] for i in range(x.shape[1] // LANES)]
    return functools.reduce(jnp.maximum, blocks)


ONES_ROWS = 16


def _mla_tile_fns(q_ref, k_ref, vt_ref, km_ref, vmt_ref, za_ref, qn_ref, kn_ref, kmn_ref,
                  o_ref, s_ref, sm_ref, mx_ref):
    n_t = q_ref.shape[2] // TQ

    def scores_into(slot, qt, k, km):
        s = _dot_nt(k, qt)
        sm = _dot_nt(km, qt)
        key = lax.broadcasted_iota(jnp.int32, sm.shape, 0)
        sm = jnp.where(key < N_META, sm, MASK_VALUE)
        s_ref[slot] = s
        sm_ref[slot] = sm
        mx_ref[slot] = jnp.maximum(jnp.max(s, axis=0, keepdims=True), jnp.max(sm, axis=0, keepdims=True))

    def scores(u, slot):
        hh, t = divmod(u, n_t)
        scores_into(slot, q_ref[0, hh, t * TQ:(t + 1) * TQ, :], k_ref[0, hh], km_ref[hh])

    def scores_next(slot):
        scores_into(slot, qn_ref[0, 0], kn_ref[0, 0], kmn_ref[0])

    def finish(u, slot):
        hh, t = divmod(u, n_t)
        rows = slice(t * TQ, (t + 1) * TQ)
        cols = slice(hh * MLA_V, (hh + 1) * MLA_V)
        m = mx_ref[slot]
        p = jnp.exp2(s_ref[slot] - m).astype(jnp.bfloat16)
        pm = jnp.exp2(sm_ref[slot] - m).astype(jnp.bfloat16)
        ones = jnp.ones((ONES_ROWS, vt_ref.shape[3]), jnp.bfloat16)
        ones_m = jnp.ones((ONES_ROWS, META_PAD), jnp.bfloat16)
        o = (_dot(jnp.concatenate([vt_ref[0, hh], ones], axis=0), p)
             + _dot(jnp.concatenate([vmt_ref[hh], ones_m], axis=0), pm))
        l = o[MLA_V:MLA_V + 1, :]
        y = (o[:MLA_V, :] * (1.0 / l)).T
        o_ref[rows, cols] = (y * za_ref[rows, cols].astype(jnp.float32)).astype(o_ref.dtype)

    return scores, scores_next, finish


def _diff_tile_fns(qd_ref, kd_ref, vd_ref, kdm_ref, vdm_ref, btab_ref, mtab_ref, lam_ref, subln_ref, zb_ref,
                   qdn_ref, kdn_ref, kdmn_ref, btabn_ref, mtabn_ref, o_ref, s_ref, sm_ref, mx_ref):
    n_chunks = kd_ref.shape[0] // KC
    lv = lam_ref[...]
    lam = (jnp.exp(jnp.sum(lv[0:1] * lv[1:2], axis=-1, keepdims=True))
           - jnp.exp(jnp.sum(lv[2:3] * lv[3:4], axis=-1, keepdims=True)) + LAM_INIT)

    c_neg = btab_ref[0, 0, 0:1, 0:1]
    c_pos = btab_ref[0, 4, 0:1, 0:1]

    def scores_into(t, slot, q_tile_ref, k_ref_, km_ref_, bt_ref, mt_ref):
        cn, cp = bt_ref[0, 0, 0:1, 0:1], bt_ref[0, 4, 0:1, 0:1]
        for half in range(2):
            lo = half * DIFF_QK
            qm = q_tile_ref[:, lo:lo + DIFF_QK]
            sm = _dot_nt(qm, km_ref_[:, lo:lo + DIFF_QK]) + mt_ref[0, min(t, 1)]
            sm_ref[slot, half] = sm
            mx = {"band": sm, "neg": None, "pos": None}
            for c in range(n_chunks):
                sc = _dot_nt(qm, k_ref_[c * KC:(c + 1) * KC, lo:lo + DIFF_QK])
                if abs(c - t) <= 1:
                    sc = sc + bt_ref[0, c - t + 2]
                    grp = "band"
                else:
                    grp = "neg" if c < t else "pos"
                s_ref[slot, half, :, c * KC:(c + 1) * KC] = sc
                fold = _lane_fold_max(sc)
                mx[grp] = fold if mx[grp] is None else jnp.maximum(mx[grp], fold)
            acc = mx["band"]
            if mx["neg"] is not None:
                acc = jnp.maximum(acc, mx["neg"] + cn)
            if mx["pos"] is not None:
                acc = jnp.maximum(acc, mx["pos"] + cp)
            mx_ref[slot, half] = acc

    def scores(t, slot):
        scores_into(t, slot, qd_ref.at[t * TQ:(t + 1) * TQ], kd_ref, kdm_ref, btab_ref, mtab_ref)

    def scores_next(slot):
        scores_into(0, slot, qdn_ref, kdn_ref, kdmn_ref, btabn_ref, mtabn_ref)

    def softmax_parts(t, slot, half):
        m = jnp.max(mx_ref[slot, half], axis=-1, keepdims=True)
        m_neg = m - c_neg
        m_pos = m - c_pos
        chunks = []
        for c in range(n_chunks):
            mc = m if abs(c - t) <= 1 else (m_neg if c < t else m_pos)
            chunks.append(jnp.exp2(s_ref[slot, half, :, c * KC:(c + 1) * KC] - mc))
        p = jnp.concatenate(chunks, axis=-1)
        pm = jnp.exp2(sm_ref[slot, half] - m)
        l = jnp.sum(p, axis=-1, keepdims=True) + jnp.sum(pm, axis=-1, keepdims=True)
        return p, pm, l

    def finish(t, slot):
        rows = slice(t * TQ, (t + 1) * TQ)
        p1, pm1, l1 = softmax_parts(t, slot, 0)
        p2, pm2, l2 = softmax_parts(t, slot, 1)
        r = lam * l1 * (1.0 / l2)
        a = (p1 - p2 * r).astype(jnp.bfloat16)
        am = (pm1 - pm2 * r).astype(jnp.bfloat16)
        o = (_dot(a, vd_ref[...]) + _dot(am, vdm_ref[...])) * (1.0 / l1)
        y = o * _rms_scale(o, DIFF_V) * subln_ref[...] * (1.0 - LAM_INIT)
        y = y * zb_ref[rows, :].astype(jnp.float32)
        o_ref[rows, :] = y.astype(o_ref.dtype)

    return scores, scores_next, finish


def _attn_kernel(q_ref, k_ref, vt_ref, km_ref, vmt_ref, za_ref,
                 qd_ref, kd_ref, vd_ref, kdm_ref, vdm_ref, btab_ref, mtab_ref, lam_ref, subln_ref, zb_ref,
                 qn_ref, kn_ref, kmn_ref, qdn_ref, kdn_ref, kdmn_ref, btabn_ref, mtabn_ref,
                 oa_ref, ob_ref, ms_ref, msm_ref, mmx_ref, ds_ref, dsm_ref, dmx_ref):
    m_scores, m_scores_next, m_finish = _mla_tile_fns(
        q_ref, k_ref, vt_ref, km_ref, vmt_ref, za_ref, qn_ref, kn_ref, kmn_ref, oa_ref,
        ms_ref, msm_ref, mmx_ref)
    d_scores, d_scores_next, d_finish = _diff_tile_fns(
        qd_ref, kd_ref, vd_ref, kdm_ref, vdm_ref, btab_ref, mtab_ref, lam_ref, subln_ref, zb_ref,
        qdn_ref, kdn_ref, kdmn_ref, btabn_ref, mtabn_ref, ob_ref, ds_ref, dsm_ref, dmx_ref)
    n_t = qd_ref.shape[0] // TQ
    n_u = n_t * MLA_PER_STEP

    @pl.when((pl.program_id(0) == 0) & (pl.program_id(1) == 0))
    def _():
        d_scores(0, 0)
        m_scores(0, 0)

    for t in range(n_t):
        if t + 1 < n_t:
            d_scores(t + 1, (t + 1) % 2)
        else:
            d_scores_next(0)
        for u in range(t * MLA_PER_STEP, (t + 1) * MLA_PER_STEP):
            if u + 1 < n_u:
                m_scores(u + 1, (u + 1) % 2)
            else:
                m_scores_next(0)
            m_finish(u, u % 2)
        d_finish(t, t % 2)


def _attention(q, k, vt, km, vmt, zs, qk, vd, qk_meta, vd_meta, btab, mtab, lam, subln, seq):
    nb = q.shape[0]
    cps = D_MODEL // DIFF_V
    mps = MLA_PER_STEP
    n_steps = nb * DIFF_HEADS

    def nxt(b, h):
        g = jnp.minimum(b * DIFF_HEADS + h + 1, n_steps - 1)
        return g // DIFF_HEADS, g % DIFF_HEADS

    def nb_(b, h):
        return nxt(b, h)[0]

    def nh_(b, h):
        return nxt(b, h)[1]

    return pl.pallas_call(
        _attn_kernel,
        grid=(nb, DIFF_HEADS),
        in_specs=[
            pl.BlockSpec((1, mps, seq, HEAD_PAD), lambda b, h: (b, h, 0, 0)),
            pl.BlockSpec((1, mps, seq, HEAD_PAD), lambda b, h: (b, h, 0, 0)),
            pl.BlockSpec((1, mps, MLA_V, seq), lambda b, h: (b, h, 0, 0)),
            pl.BlockSpec((mps, META_PAD, HEAD_PAD), lambda b, h: (h, 0, 0)),
            pl.BlockSpec((mps, MLA_V, META_PAD), lambda b, h: (h, 0, 0)),
            pl.BlockSpec((seq, mps * MLA_V), lambda b, h: (b, h)),
            pl.BlockSpec((seq, DIFF_V), lambda b, h: (b, h)),
            pl.BlockSpec((seq, DIFF_V), lambda b, h: (b, cps + h)),
            pl.BlockSpec((seq, DIFF_V), lambda b, h: (b, h)),
            pl.BlockSpec((META_PAD, DIFF_V), lambda b, h: (0, h)),
            pl.BlockSpec((META_PAD, DIFF_V), lambda b, h: (0, h)),
            pl.BlockSpec((1, 5, TQ, KC), lambda b, h: (h, 0, 0, 0)),
            pl.BlockSpec((1, 2, TQ, META_PAD), lambda b, h: (h, 0, 0, 0)),
            pl.BlockSpec((4, DIFF_QK), lambda b, h: (0, 0)),
            pl.BlockSpec((1, DIFF_V), lambda b, h: (0, 0)),
            pl.BlockSpec((seq, DIFF_V), lambda b, h: (b, cps + h)),
            pl.BlockSpec((1, 1, TQ, HEAD_PAD), lambda b, h: (nb_(b, h), mps * nh_(b, h), 0, 0)),
            pl.BlockSpec((1, 1, seq, HEAD_PAD), lambda b, h: (nb_(b, h), mps * nh_(b, h), 0, 0)),
            pl.BlockSpec((1, META_PAD, HEAD_PAD), lambda b, h: (mps * nh_(b, h), 0, 0)),
            pl.BlockSpec((TQ, DIFF_V), lambda b, h: (nb_(b, h) * (seq // TQ), nh_(b, h))),
            pl.BlockSpec((seq, DIFF_V), lambda b, h: (nb_(b, h), cps + nh_(b, h))),
            pl.BlockSpec((META_PAD, DIFF_V), lambda b, h: (0, nh_(b, h))),
            pl.BlockSpec((1, 5, TQ, KC), lambda b, h: (nh_(b, h), 0, 0, 0)),
            pl.BlockSpec((1, 2, TQ, META_PAD), lambda b, h: (nh_(b, h), 0, 0, 0)),
        ],
        out_specs=[
            pl.BlockSpec((seq, mps * MLA_V), lambda b, h: (b, h)),
            pl.BlockSpec((seq, DIFF_V), lambda b, h: (b, h)),
        ],
        out_shape=[
            jax.ShapeDtypeStruct((nb * seq, MLA_HEADS * MLA_V), jnp.bfloat16),
            jax.ShapeDtypeStruct((nb * seq, DIFF_HEADS * DIFF_V), jnp.bfloat16),
        ],
        scratch_shapes=[
            pltpu.VMEM((2, seq, TQ), jnp.float32), pltpu.VMEM((2, META_PAD, TQ), jnp.float32),
            pltpu.VMEM((2, 1, TQ), jnp.float32),
            pltpu.VMEM((2, 2, TQ, seq), jnp.float32), pltpu.VMEM((2, 2, TQ, META_PAD), jnp.float32),
            pltpu.VMEM((2, 2, TQ, LANES), jnp.float32),
        ],
        compiler_params=pltpu.CompilerParams(
            dimension_semantics=("arbitrary", "arbitrary"), vmem_limit_bytes=VMEM_LIMIT_BYTES),
        name="attention",
    )(q, k, vt, km, vmt, zs, qk, qk, vd, qk_meta, vd_meta, btab, mtab, lam, subln, zs,
      q, k, km, qk, qk, qk_meta, btab, mtab)


def _merge_kernel(oa_ref, ob_ref, wa_ref, wb_ref, ga_ref, gb_ref, m_ref):
    ya = _dot(oa_ref[...], wa_ref[...])
    yb = _dot(ob_ref[...], wb_ref[...])
    m = ga_ref[...].astype(jnp.float32) * ya + gb_ref[...].astype(jnp.float32) * yb
    m_ref[...] = m.astype(m_ref.dtype)


def _merge(oa, ob, wa, wb, gs, tm, tn):
    rows = oa.shape[0]
    tps = D_MODEL // tn
    return pl.pallas_call(
        _merge_kernel,
        grid=(rows // tm, D_MODEL // tn),
        in_specs=[
            pl.BlockSpec((tm, D_MODEL), lambda i, j: (i, 0)),
            pl.BlockSpec((tm, D_MODEL), lambda i, j: (i, 0)),
            pl.BlockSpec((D_MODEL, tn), lambda i, j: (0, j)),
            pl.BlockSpec((D_MODEL, tn), lambda i, j: (0, j)),
            pl.BlockSpec((tm, tn), lambda i, j: (i, j)),
            pl.BlockSpec((tm, tn), lambda i, j: (i, tps + j)),
        ],
        out_specs=pl.BlockSpec((tm, tn), lambda i, j: (i, j)),
        out_shape=jax.ShapeDtypeStruct((rows, D_MODEL), jnp.bfloat16),
        compiler_params=pltpu.CompilerParams(
            dimension_semantics=("arbitrary", "arbitrary"), vmem_limit_bytes=VMEM_LIMIT_BYTES),
        name="merge",
    )(oa, ob, wa, wb, gs, gs)


def _out_kernel(m_ref, w_ref, x_ref, o_ref):
    o_ref[...] = x_ref[...] + _dot(m_ref[...], w_ref[...])


def _out_proj(m, wout, x2, tm, tn):
    rows = m.shape[0]
    return pl.pallas_call(
        _out_kernel,
        grid=(rows // tm, D_MODEL // tn),
        in_specs=[
            pl.BlockSpec((tm, D_MODEL), lambda i, j: (i, 0)),
            pl.BlockSpec((D_MODEL, tn), lambda i, j: (0, j)),
            pl.BlockSpec((tm, tn), lambda i, j: (i, j)),
        ],
        out_specs=pl.BlockSpec((tm, tn), lambda i, j: (i, j)),
        out_shape=jax.ShapeDtypeStruct((rows, D_MODEL), jnp.float32),
        compiler_params=pltpu.CompilerParams(
            dimension_semantics=("arbitrary", "arbitrary"), vmem_limit_bytes=VMEM_LIMIT_BYTES),
        name="out_proj",
    )(m, wout, x2)


def _t5_bucket(rel):
    nb = REL_BUCKETS // 2
    max_exact = nb // 2
    ret = jnp.where(rel > 0, nb, 0)
    n = jnp.abs(rel)
    nf = jnp.maximum(n, 1).astype(jnp.float32)
    large = max_exact + (jnp.log(nf / max_exact) / math.log(REL_MAX_DIST / max_exact)
                         * (nb - max_exact)).astype(jnp.int32)
    large = jnp.minimum(large, nb - 1)
    return ret + jnp.where(n < max_exact, n, large)


SEG_W = 2 * KC


def _bias_tab_kernel(seg_ref, btab_ref, mtab_ref):
    def expand(row):
        x = jnp.broadcast_to(seg_ref[0, row:row + 1, :], (TQ, SEG_W))
        return pltpu.roll(x, 0, 1, stride=1, stride_axis=0) * LOG2E

    for t in range(5):
        btab_ref[0, t] = expand(t)[:, KC:]
    lane = lax.broadcasted_iota(jnp.int32, (TQ, META_PAD), 1)
    for t in range(2):
        mtab_ref[0, t] = jnp.where(lane < N_META, expand(5 + t)[:, KC:KC + META_PAD], MASK_VALUE)


def _bias_tables(rel_bias):
    i = jnp.arange(SEG_W, dtype=jnp.int32)[None, :]
    dd = jnp.arange(-2, 3, dtype=jnp.int32)[:, None]
    rel_seq = dd * KC - KC + i
    rel_meta = i - KC - N_META - jnp.arange(2, dtype=jnp.int32)[:, None] * TQ
    rel = jnp.concatenate([rel_seq, rel_meta, rel_meta[:1]], axis=0)
    seg = jnp.transpose(rel_bias[_t5_bucket(rel)], (2, 0, 1)).astype(jnp.float32)
    return pl.pallas_call(
        _bias_tab_kernel,
        grid=(DIFF_HEADS,),
        in_specs=[pl.BlockSpec((1, 8, SEG_W), lambda h: (h, 0, 0))],
        out_specs=[
            pl.BlockSpec((1, 5, TQ, KC), lambda h: (h, 0, 0, 0)),
            pl.BlockSpec((1, 2, TQ, META_PAD), lambda h: (h, 0, 0, 0)),
        ],
        out_shape=[
            jax.ShapeDtypeStruct((DIFF_HEADS, 5, TQ, KC), jnp.float32),
            jax.ShapeDtypeStruct((DIFF_HEADS, 2, TQ, META_PAD), jnp.float32),
        ],
        compiler_params=pltpu.CompilerParams(dimension_semantics=("arbitrary",)),
        name="bias_tables",
    )(seg)


def _rope_tables(n_pos):
    half = MLA_ROPE // 2
    inv = ROPE_THETA ** (-jnp.arange(half, dtype=jnp.float32) / half)
    ang = jnp.arange(n_pos, dtype=jnp.int32).astype(jnp.float32)[:, None] * inv[None, :]
    c, s = jnp.cos(ang), jnp.sin(ang)
    z = jnp.zeros((n_pos, LANES - MLA_ROPE), jnp.float32)
    cos_t = jnp.concatenate([c, c, z], axis=-1)
    sin_t = jnp.concatenate([-s, s, z], axis=-1)
    return cos_t, sin_t


def _pad_rows(a, rows):
    return jnp.pad(a, [(0, rows - a.shape[0])] + [(0, 0)] * (a.ndim - 1))


def kernel(x, meta_tokens, rel_bias, norm_in, w_in, q_a_norm, kv_a_norm, w_uq, w_ukv, mla_q_norm, mla_k_norm,
           diff_q_norm, diff_k_norm, diff_lambda, diff_subln, w_branch_a, w_branch_b, w_out):
    nb, seq, _ = x.shape
    bf = jnp.bfloat16
    n_small = MLA_Q_RANK + MLA_KV_RANK + MLA_ROPE

    w_in_t = jnp.swapaxes(w_in[0], 0, 1)
    wsm = jnp.pad(w_in_t[:n_small], ((0, SMALL_W - n_small), (0, 0))).astype(bf)
    wuq = jnp.pad(w_uq[0].reshape(MLA_Q_RANK, MLA_HEADS, MLA_QK),
                  ((0, 0), (0, 0), (0, HEAD_PAD - MLA_QK))).reshape(MLA_Q_RANK, MLA_HEADS * HEAD_PAD).astype(bf)
    wukv = w_ukv[0].reshape(MLA_KV_RANK, MLA_HEADS, MLA_NOPE + MLA_V)
    wuk = wukv[:, :, :MLA_NOPE].reshape(MLA_KV_RANK, MLA_HEADS * MLA_NOPE).astype(bf)
    wvt = jnp.swapaxes(wukv[:, :, MLA_NOPE:].reshape(MLA_KV_RANK, MLA_HEADS * MLA_V), 0, 1).astype(bf)

    gin = norm_in[0][None]
    gqa, gkva = q_a_norm[0][None], kv_a_norm[0][None]
    gq = jnp.pad(mla_q_norm[0], (0, HEAD_PAD - MLA_QK))[None]
    gk = jnp.pad(mla_k_norm[0], (0, HEAD_PAD - MLA_QK))[None]
    gqd, gkd = diff_q_norm[0][None], diff_k_norm[0][None]
    subln = diff_subln[0][None]
    lam = diff_lambda[0]

    cos_t, sin_t = _rope_tables(N_META + seq)
    btab, mtab = _bias_tables(rel_bias)

    x2 = x.reshape(nb * seq, D_MODEL)
    meta = meta_tokens.astype(x.dtype)

    qd_scale = LOG2E * DIFF_QK ** -0.5
    reps = D_MODEL // DIFF_QK
    gain_qk = jnp.concatenate([jnp.tile(gqd * qd_scale, (1, reps)), jnp.tile(gkd, (1, reps))], axis=-1)

    q, k, vt, u, wbig = _mla_proj(x, gin, wsm, gqa, gkva, wuq, wuk, wvt, gq, gk, cos_t[N_META:], sin_t[N_META:],
                                  tm=256, w_wide=w_in_t, wide_row0=n_small)
    u2 = u.reshape(nb * seq, D_MODEL)
    zs, wa = _seg_proj(u2, wbig, (SEG_ZA, SEG_ZB), "silu", 1024, D_MODEL, cast_src=w_branch_a[0])
    qk, wb = _seg_proj(u2, wbig, (SEG_QD, SEG_KD), "norm", 1024, D_MODEL, gain=gain_qk, cast_src=w_branch_b[0])
    vd = _seg_proj(u2, wbig, (SEG_VD,), "copy", 1024, D_MODEL)
    gs, wo = _seg_proj(u2, wbig, (SEG_GA, SEG_GB), "sigmoid", 1024, D_MODEL, cast_src=w_out[0])

    _, km, vmt, um = _mla_proj(meta[None], gin, wsm, gqa, gkva, wuq, wuk, wvt, gq, gk,
                              cos_t[:N_META], sin_t[:N_META], tm=N_META)
    km = jnp.pad(km[0], ((0, 0), (0, META_PAD - N_META), (0, 0)))
    vmt = jnp.pad(vmt[0], ((0, 0), (0, 0), (0, META_PAD - N_META)))
    k_meta = _pad_rows(_seg_proj(um[0], wbig, (SEG_KD,), "norm", N_META, 1024, gain=gain_qk[:, D_MODEL:]), META_PAD)
    vd_meta = _pad_rows(_seg_proj(um[0], wbig, (SEG_VD,), "copy", N_META, 1024), META_PAD)

    oa, ob = _attention(q, k, vt, km, vmt, zs, qk, vd, k_meta, vd_meta, btab, mtab, lam, subln, seq)
    m = _merge(oa, ob, wa, wb, gs, tm=1024, tn=1024)
    out = _out_proj(m, wo, x2, tm=512, tn=D_MODEL)
    return out.reshape(nb, seq, D_MODEL)
```

```python
import functools
import math

import jax
import jax.numpy as jnp
from jax import lax
from jax.experimental import pallas as pl
from jax.experimental.pallas import tpu as pltpu

D_MODEL = 2048
N_META = 16
EPS = 1e-6

MLA_HEADS = 16
MLA_Q_RANK = 768
MLA_KV_RANK = 512
MLA_NOPE = 128
MLA_ROPE = 64
MLA_V = 128
MLA_QK = MLA_NOPE + MLA_ROPE
ROPE_THETA = 10000.0

DIFF_HEADS = 8
DIFF_QK = 128
DIFF_V = 2 * DIFF_QK
LAM_INIT = 0.8 - 0.6 * math.exp(-0.3 * 0)

REL_BUCKETS = 32
REL_MAX_DIST = 128

LANES = 128
MXU_DIM = 256
VMEM_LIMIT_BYTES = 62 * 1024 * 1024

HEAD_PAD = MXU_DIM
SMALL_W = MLA_Q_RANK + MLA_KV_RANK + LANES
META_PAD = LANES
MASK_VALUE = -1e30
TQ = 256
KC = 256
LOG2E = math.log2(math.e)

MLA_PROJ_TM = 256
IN_PROJ_TM, IN_PROJ_TN = 1024, D_MODEL
META_TN = 1024
MERGE_TM, MERGE_TN = 1024, 1024
OUT_TM, OUT_TN = 512, D_MODEL

SEG_ZA, SEG_QD, SEG_KD, SEG_VD, SEG_ZB, SEG_GA, SEG_GB = range(7)


def _dot(a, b):
    return jnp.dot(a, b, preferred_element_type=jnp.float32)


def _dot_nt(a, b):
    return lax.dot_general(a, b, (((1,), (1,)), ((), ())), preferred_element_type=jnp.float32)


def _rms_scale(x, width):
    return lax.rsqrt(jnp.sum(x * x, axis=-1, keepdims=True) * (1.0 / width) + EPS)


def _rot_half64(x):
    return pltpu.roll(x, 32, 1) + pltpu.roll(x, 96, 1)


def _mla_proj_kernel(x_ref, gin_ref, wsm_ref, gqa_ref, gkva_ref, wuq_ref, wuk_ref, wvt_ref,
                     gq_ref, gk_ref, cos_ref, sin_ref, *rest, q_scale):
    if len(rest) == 6:
        rest[5][...] = rest[0][...].astype(jnp.bfloat16)
        rest = rest[1:5]
    q_ref, k_ref, v_ref, u_ref = rest
    x = x_ref[...]
    u = (x * _rms_scale(x, D_MODEL) * gin_ref[...]).astype(jnp.bfloat16)
    u_ref[...] = u
    p = _dot_nt(u, wsm_ref[...])
    cq = p[:, :MLA_Q_RANK]
    ckv = p[:, MLA_Q_RANK:MLA_Q_RANK + MLA_KV_RANK]
    kr = p[:, MLA_Q_RANK + MLA_KV_RANK:]
    cqn = (cq * _rms_scale(cq, MLA_Q_RANK) * gqa_ref[...]).astype(jnp.bfloat16)
    ckvn = (ckv * _rms_scale(ckv, MLA_KV_RANK) * gkva_ref[...]).astype(jnp.bfloat16)
    qf = _dot(cqn, wuq_ref[...])
    kf = _dot(ckvn, wuk_ref[...])
    vt = _dot_nt(wvt_ref[...], ckvn).astype(jnp.bfloat16)

    cos = cos_ref[...]
    sin = sin_ref[...]
    gq = gq_ref[...]
    gk = gk_ref[...]
    gq_nope, gq_rope = gq[:, :LANES], gq[:, LANES:]
    gk_nope, gk_rope = gk[:, :LANES], gk[:, LANES:]

    ss_kr = jnp.sum(kr * kr, axis=-1, keepdims=True)
    krg = kr * gk_rope
    kr_roped = krg * cos + _rot_half64(krg) * sin

    for h in range(MLA_HEADS):
        qh = qf[:, h * HEAD_PAD:(h + 1) * HEAD_PAD]
        rq = _rms_scale(qh, MLA_QK)
        q_nope = qh[:, :LANES] * rq * gq_nope
        q_r = qh[:, LANES:] * rq * gq_rope
        q_r = q_r * cos + _rot_half64(q_r) * sin
        q_ref[0, h, :, :LANES] = (q_nope * q_scale).astype(jnp.bfloat16)
        q_ref[0, h, :, LANES:] = (q_r * q_scale).astype(jnp.bfloat16)

        k_nope = kf[:, h * MLA_NOPE:(h + 1) * MLA_NOPE]
        rk = lax.rsqrt((jnp.sum(k_nope * k_nope, axis=-1, keepdims=True) + ss_kr) * (1.0 / MLA_QK) + EPS)
        k_ref[0, h, :, :LANES] = (k_nope * rk * gk_nope).astype(jnp.bfloat16)
        k_ref[0, h, :, LANES:] = (kr_roped * rk).astype(jnp.bfloat16)
        v_ref[0, h] = vt[h * MLA_V:(h + 1) * MLA_V, :]


def _mla_proj(x3, gin, wsm, gqa, gkva, wuq, wuk, wvt, gq, gk, cos, sin, tm, w_wide=None, wide_row0=0):
    nb, rows, _ = x3.shape
    const = lambda b, i: (0, 0)
    tpb = rows // tm
    kern = functools.partial(_mla_proj_kernel, q_scale=LOG2E * MLA_QK ** -0.5)
    extra_in, extra_out_specs, extra_out_shape, extra_args = [], [], [], []
    if w_wide is not None:
        n_wide = w_wide.shape[0] - wide_row0
        slab = n_wide // (nb * tpb)
        sub = 8
        assert wide_row0 % sub == 0 and slab % sub == 0
        extra_in = [pl.BlockSpec((pl.Element(slab), pl.Element(D_MODEL)),
                                 lambda b, i: ((wide_row0 // sub + (b * tpb + i) * (slab // sub)) * sub, 0))]
        extra_out_specs = [pl.BlockSpec((slab, D_MODEL), lambda b, i: (b * tpb + i, 0))]
        extra_out_shape = [jax.ShapeDtypeStruct((n_wide, D_MODEL), jnp.bfloat16)]
        extra_args = [w_wide]
    return pl.pallas_call(
        kern,
        grid=(nb, rows // tm),
        in_specs=[
            pl.BlockSpec((None, tm, D_MODEL), lambda b, i: (b, i, 0)),
            pl.BlockSpec((1, D_MODEL), const),
            pl.BlockSpec((SMALL_W, D_MODEL), const),
            pl.BlockSpec((1, MLA_Q_RANK), const),
            pl.BlockSpec((1, MLA_KV_RANK), const),
            pl.BlockSpec((MLA_Q_RANK, MLA_HEADS * HEAD_PAD), const),
            pl.BlockSpec((MLA_KV_RANK, MLA_HEADS * MLA_NOPE), const),
            pl.BlockSpec((MLA_HEADS * MLA_V, MLA_KV_RANK), const),
            pl.BlockSpec((1, HEAD_PAD), const),
            pl.BlockSpec((1, HEAD_PAD), const),
            pl.BlockSpec((tm, LANES), lambda b, i: (i, 0)),
            pl.BlockSpec((tm, LANES), lambda b, i: (i, 0)),
        ] + extra_in,
        out_specs=[
            pl.BlockSpec((1, MLA_HEADS, tm, HEAD_PAD), lambda b, i: (b, 0, i, 0)),
            pl.BlockSpec((1, MLA_HEADS, tm, HEAD_PAD), lambda b, i: (b, 0, i, 0)),
            pl.BlockSpec((1, MLA_HEADS, MLA_V, tm), lambda b, i: (b, 0, 0, i)),
            pl.BlockSpec((None, tm, D_MODEL), lambda b, i: (b, i, 0)),
        ] + extra_out_specs,
        out_shape=[
            jax.ShapeDtypeStruct((nb, MLA_HEADS, rows, HEAD_PAD), jnp.bfloat16),
            jax.ShapeDtypeStruct((nb, MLA_HEADS, rows, HEAD_PAD), jnp.bfloat16),
            jax.ShapeDtypeStruct((nb, MLA_HEADS, MLA_V, rows), jnp.bfloat16),
            jax.ShapeDtypeStruct((nb, rows, D_MODEL), jnp.bfloat16),
        ] + extra_out_shape,
        compiler_params=pltpu.CompilerParams(
            dimension_semantics=("arbitrary", "arbitrary"), vmem_limit_bytes=VMEM_LIMIT_BYTES),
        name="mla_proj",
    )(x3, gin, wsm, gqa, gkva, wuq, wuk, wvt, gq, gk, cos, sin, *extra_args)


def _seg_proj_kernel(u_ref, w_ref, *rest, kind, has_gain, has_cast):
    rest = list(rest)
    gain_ref = rest.pop(0) if has_gain else None
    cast_src_ref = rest.pop(0) if has_cast else None
    o_ref = rest.pop(0)
    if has_cast:
        rest.pop(0)[...] = cast_src_ref[...].astype(jnp.bfloat16)
    acc = _dot_nt(u_ref[...], w_ref[...])
    if kind == "silu":
        o_ref[...] = (acc / (1.0 + jnp.exp(-acc))).astype(o_ref.dtype)
    elif kind == "sigmoid":
        o_ref[...] = (1.0 / (1.0 + jnp.exp(-acc))).astype(o_ref.dtype)
    elif kind == "copy":
        o_ref[...] = acc.astype(o_ref.dtype)
    else:
        gain = gain_ref[...]
        for c in range(acc.shape[1] // DIFF_QK):
            cols = slice(c * DIFF_QK, (c + 1) * DIFF_QK)
            g = acc[:, cols]
            o_ref[:, cols] = (g * _rms_scale(g, DIFF_QK) * gain[:, cols]).astype(o_ref.dtype)


def _seg_proj(u2, wbig, segs, kind, tm, tn, gain=None, cast_src=None):
    rows = u2.shape[0]
    tps = D_MODEL // tn
    nj = len(segs) * tps
    if len(segs) == 1:
        w_map = lambda i, j: (segs[0] * tps + j, 0)
    else:
        w_map = lambda i, j: (jnp.where(j < tps, segs[0] * tps + j, segs[1] * tps + j - tps), 0)
    in_specs = [pl.BlockSpec((tm, D_MODEL), lambda i, j: (i, 0)), pl.BlockSpec((tn, D_MODEL), w_map)]
    args = [u2, wbig]
    out_specs = [pl.BlockSpec((tm, tn), lambda i, j: (i, j))]
    out_shape = [jax.ShapeDtypeStruct((rows, len(segs) * D_MODEL), jnp.bfloat16)]
    if gain is not None:
        in_specs.append(pl.BlockSpec((1, tn), lambda i, j: (0, j)))
        args.append(gain)
    if cast_src is not None:
        slab = cast_src.shape[0] // ((rows // tm) * nj)
        slab_spec = pl.BlockSpec((slab, cast_src.shape[1]), lambda i, j: (i * nj + j, 0))
        in_specs.append(slab_spec)
        args.append(cast_src)
        out_specs.append(slab_spec)
        out_shape.append(jax.ShapeDtypeStruct(cast_src.shape, jnp.bfloat16))
    res = pl.pallas_call(
        functools.partial(_seg_proj_kernel, kind=kind, has_gain=gain is not None, has_cast=cast_src is not None),
        grid=(rows // tm, nj),
        in_specs=in_specs,
        out_specs=out_specs,
        out_shape=out_shape,
        compiler_params=pltpu.CompilerParams(
            dimension_semantics=("arbitrary", "arbitrary"), vmem_limit_bytes=VMEM_LIMIT_BYTES),
        name="in_proj_" + kind,
    )(*args)
    return res if cast_src is not None else res[0]


MLA_PER_STEP = MLA_HEADS // DIFF_HEADS


def _lane_fold_max(x):
    blocks = [x[:, i * LANES:(i + 1) * LANES] for i in range(x.shape[1] // LANES)]
    return functools.reduce(jnp.maximum, blocks)


ONES_ROWS = 16


def _mla_tile_fns(q_ref, k_ref, vt_ref, km_ref, vmt_ref, za_ref, qn_ref, kn_ref, kmn_ref,
                  o_ref, s_ref, sm_ref, mx_ref):
    n_t = q_ref.shape[2] // TQ

    def scores_into(slot, qt, k, km):
        s = _dot_nt(k, qt)
        sm = _dot_nt(km, qt)
        key = lax.broadcasted_iota(jnp.int32, sm.shape, 0)
        sm = jnp.where(key < N_META, sm, MASK_VALUE)
        s_ref[slot] = s
        sm_ref[slot] = sm
        mx_ref[slot] = jnp.maximum(jnp.max(s, axis=0, keepdims=True), jnp.max(sm, axis=0, keepdims=True))

    def scores(u, slot):
        hh, t = divmod(u, n_t)
        scores_into(slot, q_ref[0, hh, t * TQ:(t + 1) * TQ, :], k_ref[0, hh], km_ref[hh])

    def scores_next(slot):
        scores_into(slot, qn_ref[0, 0], kn_ref[0, 0], kmn_ref[0])

    def finish(u, slot):
        hh, t = divmod(u, n_t)
        rows = slice(t * TQ, (t + 1) * TQ)
        cols = slice(hh * MLA_V, (hh + 1) * MLA_V)
        m = mx_ref[slot]
        p = jnp.exp2(s_ref[slot] - m).astype(jnp.bfloat16)
        pm = jnp.exp2(sm_ref[slot] - m).astype(jnp.bfloat16)
        ones = jnp.ones((ONES_ROWS, vt_ref.shape[3]), jnp.bfloat16)
        ones_m = jnp.ones((ONES_ROWS, META_PAD), jnp.bfloat16)
        o = (_dot(jnp.concatenate([vt_ref[0, hh], ones], axis=0), p)
             + _dot(jnp.concatenate([vmt_ref[hh], ones_m], axis=0), pm))
        l = o[MLA_V:MLA_V + 1, :]
        y = (o[:MLA_V, :] * (1.0 / l)).T
        o_ref[rows, cols] = (y * za_ref[rows, cols].astype(jnp.float32)).astype(o_ref.dtype)

    return scores, scores_next, finish


def _diff_tile_fns(qd_ref, kd_ref, vd_ref, kdm_ref, vdm_ref, btab_ref, mtab_ref, lam_ref, subln_ref, zb_ref,
                   qdn_ref, kdn_ref, kdmn_ref, btabn_ref, mtabn_ref, o_ref, s_ref, sm_ref, mx_ref):
    n_chunks = kd_ref.shape[0] // KC
    lv = lam_ref[...]
    lam = (jnp.exp(jnp.sum(lv[0:1] * lv[1:2], axis=-1, keepdims=True))
           - jnp.exp(jnp.sum(lv[2:3] * lv[3:4], axis=-1, keepdims=True)) + LAM_INIT)

    c_neg = btab_ref[0, 0, 0:1, 0:1]
    c_pos = btab_ref[0, 4, 0:1, 0:1]

    def scores_into(t, slot, q_tile_ref, k_ref_, km_ref_, bt_ref, mt_ref):
        cn, cp = bt_ref[0, 0, 0:1, 0:1], bt_ref[0, 4, 0:1, 0:1]
        for half in range(2):
            lo = half * DIFF_QK
            qm = q_tile_ref[:, lo:lo + DIFF_QK]
            sm = _dot_nt(qm, km_ref_[:, lo:lo + DIFF_QK]) + mt_ref[0, min(t, 1)]
            sm_ref[slot, half] = sm
            mx = {"band": sm, "neg": None, "pos": None}
            for c in range(n_chunks):
                sc = _dot_nt(qm, k_ref_[c * KC:(c + 1) * KC, lo:lo + DIFF_QK])
                if abs(c - t) <= 1:
                    sc = sc + bt_ref[0, c - t + 2]
                    grp = "band"
                else:
                    grp = "neg" if c < t else "pos"
                s_ref[slot, half, :, c * KC:(c + 1) * KC] = sc
                fold = _lane_fold_max(sc)
                mx[grp] = fold if mx[grp] is None else jnp.maximum(mx[grp], fold)
            acc = mx["band"]
            if mx["neg"] is not None:
                acc = jnp.maximum(acc, mx["neg"] + cn)
            if mx["pos"] is not None:
                acc = jnp.maximum(acc, mx["pos"] + cp)
            mx_ref[slot, half] = acc

    def scores(t, slot):
        scores_into(t, slot, qd_ref.at[t * TQ:(t + 1) * TQ], kd_ref, kdm_ref, btab_ref, mtab_ref)

    def scores_next(slot):
        scores_into(0, slot, qdn_ref, kdn_ref, kdmn_ref, btabn_ref, mtabn_ref)

    def softmax_parts(t, slot, half):
        m = jnp.max(mx_ref[slot, half], axis=-1, keepdims=True)
        m_neg = m - c_neg
        m_pos = m - c_pos
        chunks = []
        for c in range(n_chunks):
            mc = m if abs(c - t) <= 1 else (m_neg if c < t else m_pos)
            chunks.append(jnp.exp2(s_ref[slot, half, :, c * KC:(c + 1) * KC] - mc))
        p = jnp.concatenate(chunks, axis=-1)
        pm = jnp.exp2(sm_ref[slot, half] - m)
        l = jnp.sum(p, axis=-1, keepdims=True) + jnp.sum(pm, axis=-1, keepdims=True)
        return p, pm, l

    def finish(t, slot):
        rows = slice(t * TQ, (t + 1) * TQ)
        p1, pm1, l1 = softmax_parts(t, slot, 0)
        p2, pm2, l2 = softmax_parts(t, slot, 1)
        r = lam * l1 * (1.0 / l2)
        a = (p1 - p2 * r).astype(jnp.bfloat16)
        am = (pm1 - pm2 * r).astype(jnp.bfloat16)
        o = (_dot(a, vd_ref[...]) + _dot(am, vdm_ref[...])) * (1.0 / l1)
        y = o * _rms_scale(o, DIFF_V) * subln_ref[...] * (1.0 - LAM_INIT)
        y = y * zb_ref[rows, :].astype(jnp.float32)
        o_ref[rows, :] = y.astype(o_ref.dtype)

    return scores, scores_next, finish


def _attn_kernel(q_ref, k_ref, vt_ref, km_ref, vmt_ref, za_ref,
                 qd_ref, kd_ref, vd_ref, kdm_ref, vdm_ref, btab_ref, mtab_ref, lam_ref, subln_ref, zb_ref,
                 qn_ref, kn_ref, kmn_ref, qdn_ref, kdn_ref, kdmn_ref, btabn_ref, mtabn_ref,
                 oa_ref, ob_ref, ms_ref, msm_ref, mmx_ref, ds_ref, dsm_ref, dmx_ref):
    m_scores, m_scores_next, m_finish = _mla_tile_fns(
        q_ref, k_ref, vt_ref, km_ref, vmt_ref, za_ref, qn_ref, kn_ref, kmn_ref, oa_ref,
        ms_ref, msm_ref, mmx_ref)
    d_scores, d_scores_next, d_finish = _diff_tile_fns(
        qd_ref, kd_ref, vd_ref, kdm_ref, vdm_ref, btab_ref, mtab_ref, lam_ref, subln_ref, zb_ref,
        qdn_ref, kdn_ref, kdmn_ref, btabn_ref, mtabn_ref, ob_ref, ds_ref, dsm_ref, dmx_ref)
    n_t = qd_ref.shape[0] // TQ
    n_u = n_t * MLA_PER_STEP

    @pl.when((pl.program_id(0) == 0) & (pl.program_id(1) == 0))
    def _():
        d_scores(0, 0)
        m_scores(0, 0)

    for t in range(n_t):
        if t + 1 < n_t:
            d_scores(t + 1, (t + 1) % 2)
        else:
            d_scores_next(0)
        for u in range(t * MLA_PER_STEP, (t + 1) * MLA_PER_STEP):
            if u + 1 < n_u:
                m_scores(u + 1, (u + 1) % 2)
            else:
                m_scores_next(0)
            m_finish(u, u % 2)
        d_finish(t, t % 2)


def _attention(q, k, vt, km, vmt, zs, qk, vd, qk_meta, vd_meta, btab, mtab, lam, subln, seq):
    nb = q.shape[0]
    cps = D_MODEL // DIFF_V
    mps = MLA_PER_STEP
    n_steps = nb * DIFF_HEADS

    def nxt(b, h):
        g = jnp.minimum(b * DIFF_HEADS + h + 1, n_steps - 1)
        return g // DIFF_HEADS, g % DIFF_HEADS

    def nb_(b, h):
        return nxt(b, h)[0]

    def nh_(b, h):
        return nxt(b, h)[1]

    return pl.pallas_call(
        _attn_kernel,
        grid=(nb, DIFF_HEADS),
        in_specs=[
            pl.BlockSpec((1, mps, seq, HEAD_PAD), lambda b, h: (b, h, 0, 0)),
            pl.BlockSpec((1, mps, seq, HEAD_PAD), lambda b, h: (b, h, 0, 0)),
            pl.BlockSpec((1, mps, MLA_V, seq), lambda b, h: (b, h, 0, 0)),
            pl.BlockSpec((mps, META_PAD, HEAD_PAD), lambda b, h: (h, 0, 0)),
            pl.BlockSpec((mps, MLA_V, META_PAD), lambda b, h: (h, 0, 0)),
            pl.BlockSpec((seq, mps * MLA_V), lambda b, h: (b, h)),
            pl.BlockSpec((seq, DIFF_V), lambda b, h: (b, h)),
            pl.BlockSpec((seq, DIFF_V), lambda b, h: (b, cps + h)),
            pl.BlockSpec((seq, DIFF_V), lambda b, h: (b, h)),
            pl.BlockSpec((META_PAD, DIFF_V), lambda b, h: (0, h)),
            pl.BlockSpec((META_PAD, DIFF_V), lambda b, h: (0, h)),
            pl.BlockSpec((1, 5, TQ, KC), lambda b, h: (h, 0, 0, 0)),
            pl.BlockSpec((1, 2, TQ, META_PAD), lambda b, h: (h, 0, 0, 0)),
            pl.BlockSpec((4, DIFF_QK), lambda b, h: (0, 0)),
            pl.BlockSpec((1, DIFF_V), lambda b, h: (0, 0)),
            pl.BlockSpec((seq, DIFF_V), lambda b, h: (b, cps + h)),
            pl.BlockSpec((1, 1, TQ, HEAD_PAD), lambda b, h: (nb_(b, h), mps * nh_(b, h), 0, 0)),
            pl.BlockSpec((1, 1, seq, HEAD_PAD), lambda b, h: (nb_(b, h), mps * nh_(b, h), 0, 0)),
            pl.BlockSpec((1, META_PAD, HEAD_PAD), lambda b, h: (mps * nh_(b, h), 0, 0)),
            pl.BlockSpec((TQ, DIFF_V), lambda b, h: (nb_(b, h) * (seq // TQ), nh_(b, h))),
            pl.BlockSpec((seq, DIFF_V), lambda b, h: (nb_(b, h), cps + nh_(b, h))),
            pl.BlockSpec((META_PAD, DIFF_V), lambda b, h: (0, nh_(b, h))),
            pl.BlockSpec((1, 5, TQ, KC), lambda b, h: (nh_(b, h), 0, 0, 0)),
            pl.BlockSpec((1, 2, TQ, META_PAD), lambda b, h: (nh_(b, h), 0, 0, 0)),
        ],
        out_specs=[
            pl.BlockSpec((seq, mps * MLA_V), lambda b, h: (b, h)),
            pl.BlockSpec((seq, DIFF_V), lambda b, h: (b, h)),
        ],
        out_shape=[
            jax.ShapeDtypeStruct((nb * seq, MLA_HEADS * MLA_V), jnp.bfloat16),
            jax.ShapeDtypeStruct((nb * seq, DIFF_HEADS * DIFF_V), jnp.bfloat16),
        ],
        scratch_shapes=[
            pltpu.VMEM((2, seq, TQ), jnp.float32), pltpu.VMEM((2, META_PAD, TQ), jnp.float32),
            pltpu.VMEM((2, 1, TQ), jnp.float32),
            pltpu.VMEM((2, 2, TQ, seq), jnp.float32), pltpu.VMEM((2, 2, TQ, META_PAD), jnp.float32),
            pltpu.VMEM((2, 2, TQ, LANES), jnp.float32),
        ],
        compiler_params=pltpu.CompilerParams(
            dimension_semantics=("arbitrary", "arbitrary"), vmem_limit_bytes=VMEM_LIMIT_BYTES),
        name="attention",
    )(q, k, vt, km, vmt, zs, qk, qk, vd, qk_meta, vd_meta, btab, mtab, lam, subln, zs,
      q, k, km, qk, qk, qk_meta, btab, mtab)


def _merge_kernel(oa_ref, ob_ref, wa_ref, wb_ref, ga_ref, gb_ref, m_ref):
    ya = _dot(oa_ref[...], wa_ref[...])
    yb = _dot(ob_ref[...], wb_ref[...])
    m = ga_ref[...].astype(jnp.float32) * ya + gb_ref[...].astype(jnp.float32) * yb
    m_ref[...] = m.astype(m_ref.dtype)


def _merge(oa, ob, wa, wb, gs, tm, tn):
    rows = oa.shape[0]
    tps = D_MODEL // tn
    return pl.pallas_call(
        _merge_kernel,
        grid=(rows // tm, D_MODEL // tn),
        in_specs=[
            pl.BlockSpec((tm, D_MODEL), lambda i, j: (i, 0)),
            pl.BlockSpec((tm, D_MODEL), lambda i, j: (i, 0)),
            pl.BlockSpec((D_MODEL, tn), lambda i, j: (0, j)),
            pl.BlockSpec((D_MODEL, tn), lambda i, j: (0, j)),
            pl.BlockSpec((tm, tn), lambda i, j: (i, j)),
            pl.BlockSpec((tm, tn), lambda i, j: (i, tps + j)),
        ],
        out_specs=pl.BlockSpec((tm, tn), lambda i, j: (i, j)),
        out_shape=jax.ShapeDtypeStruct((rows, D_MODEL), jnp.bfloat16),
        compiler_params=pltpu.CompilerParams(
            dimension_semantics=("arbitrary", "arbitrary"), vmem_limit_bytes=VMEM_LIMIT_BYTES),
        name="merge",
    )(oa, ob, wa, wb, gs, gs)


def _out_kernel(m_ref, w_ref, x_ref, o_ref):
    o_ref[...] = x_ref[...] + _dot(m_ref[...], w_ref[...])


def _out_proj(m, wout, x2, tm, tn):
    rows = m.shape[0]
    return pl.pallas_call(
        _out_kernel,
        grid=(rows // tm, D_MODEL // tn),
        in_specs=[
            pl.BlockSpec((tm, D_MODEL), lambda i, j: (i, 0)),
            pl.BlockSpec((D_MODEL, tn), lambda i, j: (0, j)),
            pl.BlockSpec((tm, tn), lambda i, j: (i, j)),
        ],
        out_specs=pl.BlockSpec((tm, tn), lambda i, j: (i, j)),
        out_shape=jax.ShapeDtypeStruct((rows, D_MODEL), jnp.float32),
        compiler_params=pltpu.CompilerParams(
            dimension_semantics=("arbitrary", "arbitrary"), vmem_limit_bytes=VMEM_LIMIT_BYTES),
        name="out_proj",
    )(m, wout, x2)


def _t5_bucket(rel):
    nb = REL_BUCKETS // 2
    max_exact = nb // 2
    ret = jnp.where(rel > 0, nb, 0)
    n = jnp.abs(rel)
    nf = jnp.maximum(n, 1).astype(jnp.float32)
    large = max_exact + (jnp.log(nf / max_exact) / math.log(REL_MAX_DIST / max_exact)
                         * (nb - max_exact)).astype(jnp.int32)
    large = jnp.minimum(large, nb - 1)
    return ret + jnp.where(n < max_exact, n, large)


SEG_W = 2 * KC


def _bias_tab_kernel(seg_ref, btab_ref, mtab_ref):
    def expand(row):
        x = jnp.broadcast_to(seg_ref[0, row:row + 1, :], (TQ, SEG_W))
        return pltpu.roll(x, 0, 1, stride=1, stride_axis=0) * LOG2E

    for t in range(5):
        btab_ref[0, t] = expand(t)[:, KC:]
    lane = lax.broadcasted_iota(jnp.int32, (TQ, META_PAD), 1)
    for t in range(2):
        mtab_ref[0, t] = jnp.where(lane < N_META, expand(5 + t)[:, KC:KC + META_PAD], MASK_VALUE)


def _bias_tables(rel_bias):
    i = jnp.arange(SEG_W, dtype=jnp.int32)[None, :]
    dd = jnp.arange(-2, 3, dtype=jnp.int32)[:, None]
    rel_seq = dd * KC - KC + i
    rel_meta = i - KC - N_META - jnp.arange(2, dtype=jnp.int32)[:, None] * TQ
    rel = jnp.concatenate([rel_seq, rel_meta, rel_meta[:1]], axis=0)
    seg = jnp.transpose(rel_bias[_t5_bucket(rel)], (2, 0, 1)).astype(jnp.float32)
    return pl.pallas_call(
        _bias_tab_kernel,
        grid=(DIFF_HEADS,),
        in_specs=[pl.BlockSpec((1, 8, SEG_W), lambda h: (h, 0, 0))],
        out_specs=[
            pl.BlockSpec((1, 5, TQ, KC), lambda h: (h, 0, 0, 0)),
            pl.BlockSpec((1, 2, TQ, META_PAD), lambda h: (h, 0, 0, 0)),
        ],
        out_shape=[
            jax.ShapeDtypeStruct((DIFF_HEADS, 5, TQ, KC), jnp.float32),
            jax.ShapeDtypeStruct((DIFF_HEADS, 2, TQ, META_PAD), jnp.float32),
        ],
        compiler_params=pltpu.CompilerParams(dimension_semantics=("arbitrary",)),
        name="bias_tables",
    )(seg)


def _rope_tables(n_pos):
    half = MLA_ROPE // 2
    inv = ROPE_THETA ** (-jnp.arange(half, dtype=jnp.float32) / half)
    ang = jnp.arange(n_pos, dtype=jnp.int32).astype(jnp.float32)[:, None] * inv[None, :]
    c, s = jnp.cos(ang), jnp.sin(ang)
    z = jnp.zeros((n_pos, LANES - MLA_ROPE), jnp.float32)
    cos_t = jnp.concatenate([c, c, z], axis=-1)
    sin_t = jnp.concatenate([-s, s, z], axis=-1)
    return cos_t, sin_t


def _pad_rows(a, rows):
    return jnp.pad(a, [(0, rows - a.shape[0])] + [(0, 0)] * (a.ndim - 1))


def kernel(x, meta_tokens, rel_bias, norm_in, w_in, q_a_norm, kv_a_norm, w_uq, w_ukv, mla_q_norm, mla_k_norm,
           diff_q_norm, diff_k_norm, diff_lambda, diff_subln, w_branch_a, w_branch_b, w_out):
    nb, seq, _ = x.shape
    bf = jnp.bfloat16
    n_small = MLA_Q_RANK + MLA_KV_RANK + MLA_ROPE

    w_in_t = jnp.swapaxes(w_in[0], 0, 1)
    wsm = jnp.pad(w_in_t[:n_small], ((0, SMALL_W - n_small), (0, 0))).astype(bf)
    wuq = jnp.pad(w_uq[0].reshape(MLA_Q_RANK, MLA_HEADS, MLA_QK),
                  ((0, 0), (0, 0), (0, HEAD_PAD - MLA_QK))).reshape(MLA_Q_RANK, MLA_HEADS * HEAD_PAD).astype(bf)
    wukv = w_ukv[0].reshape(MLA_KV_RANK, MLA_HEADS, MLA_NOPE + MLA_V)
    wuk = wukv[:, :, :MLA_NOPE].reshape(MLA_KV_RANK, MLA_HEADS * MLA_NOPE).astype(bf)
    wvt = jnp.swapaxes(wukv[:, :, MLA_NOPE:].reshape(MLA_KV_RANK, MLA_HEADS * MLA_V), 0, 1).astype(bf)

    gin = norm_in[0][None]
    gqa, gkva = q_a_norm[0][None], kv_a_norm[0][None]
    gq = jnp.pad(mla_q_norm[0], (0, HEAD_PAD - MLA_QK))[None]
    gk = jnp.pad(mla_k_norm[0], (0, HEAD_PAD - MLA_QK))[None]
    gqd, gkd = diff_q_norm[0][None], diff_k_norm[0][None]
    subln = diff_subln[0][None]
    lam = diff_lambda[0]

    cos_t, sin_t = _rope_tables(N_META + seq)
    btab, mtab = _bias_tables(rel_bias)

    x2 = x.reshape(nb * seq, D_MODEL)
    meta = meta_tokens.astype(x.dtype)

    qd_scale = LOG2E * DIFF_QK ** -0.5
    reps = D_MODEL // DIFF_QK
    gain_qk = jnp.concatenate([jnp.tile(gqd * qd_scale, (1, reps)), jnp.tile(gkd, (1, reps))], axis=-1)

    q, k, vt, u, wbig = _mla_proj(x, gin, wsm, gqa, gkva, wuq, wuk, wvt, gq, gk, cos_t[N_META:], sin_t[N_META:],
                                  tm=MLA_PROJ_TM, w_wide=w_in_t, wide_row0=n_small)
    u2 = u.reshape(nb * seq, D_MODEL)
    zs, wa = _seg_proj(u2, wbig, (SEG_ZA, SEG_ZB), "silu", IN_PROJ_TM, IN_PROJ_TN, cast_src=w_branch_a[0])
    qk, wb = _seg_proj(u2, wbig, (SEG_QD, SEG_KD), "norm", IN_PROJ_TM, IN_PROJ_TN, gain=gain_qk,
                       cast_src=w_branch_b[0])
    vd = _seg_proj(u2, wbig, (SEG_VD,), "copy", IN_PROJ_TM, IN_PROJ_TN)
    gs, wo = _seg_proj(u2, wbig, (SEG_GA, SEG_GB), "sigmoid", IN_PROJ_TM, IN_PROJ_TN, cast_src=w_out[0])

    _, km, vmt, um = _mla_proj(meta[None], gin, wsm, gqa, gkva, wuq, wuk, wvt, gq, gk,
                              cos_t[:N_META], sin_t[:N_META], tm=N_META)
    km = jnp.pad(km[0], ((0, 0), (0, META_PAD - N_META), (0, 0)))
    vmt = jnp.pad(vmt[0], ((0, 0), (0, 0), (0, META_PAD - N_META)))
    k_meta = _pad_rows(_seg_proj(um[0], wbig, (SEG_KD,), "norm", N_META, META_TN, gain=gain_qk[:, D_MODEL:]),
                       META_PAD)
    vd_meta = _pad_rows(_seg_proj(um[0], wbig, (SEG_VD,), "copy", N_META, META_TN), META_PAD)

    oa, ob = _attention(q, k, vt, km, vmt, zs, qk, vd, k_meta, vd_meta, btab, mtab, lam, subln, seq)
    m = _merge(oa, ob, wa, wb, gs, tm=MERGE_TM, tn=MERGE_TN)
    out = _out_proj(m, wo, x2, tm=OUT_TM, tn=OUT_TN)
    return out.reshape(nb, seq, D_MODEL)
```

```python
import functools
import math

import jax
import jax.numpy as jnp
from jax import lax
from jax.experimental import pallas as pl
from jax.experimental.pallas import tpu as pltpu

D_MODEL = 2048
N_META = 16
EPS = 1e-6

MLA_HEADS = 16
MLA_Q_RANK = 768
MLA_KV_RANK = 512
MLA_NOPE = 128
MLA_ROPE = 64
MLA_V = 128
MLA_QK = MLA_NOPE + MLA_ROPE
ROPE_THETA = 10000.0

DIFF_HEADS = 8
DIFF_QK = 128
DIFF_V = 2 * DIFF_QK
LAM_INIT = 0.8 - 0.6 * math.exp(-0.3 * 0)

REL_BUCKETS = 32
REL_MAX_DIST = 128

LANES = 128
MXU_DIM = 256
VMEM_LIMIT_BYTES = 62 * 1024 * 1024

HEAD_PAD = MXU_DIM
SMALL_W = MLA_Q_RANK + MLA_KV_RANK + LANES
META_PAD = LANES
MASK_VALUE = -1e30
TQ = 256
KC = 256
LOG2E = math.log2(math.e)
ONES_ROWS = 16

MLA_PROJ_TM = 256
IN_PROJ_TM, IN_PROJ_TN = 1024, D_MODEL
META_TN = 1024
MERGE_TM, MERGE_TN = 1024, 1024
OUT_TM, OUT_TN = 512, D_MODEL

SEG_ZA, SEG_QD, SEG_KD, SEG_VD, SEG_ZB, SEG_GA, SEG_GB = range(7)


def _dot(a, b):
    return jnp.dot(a, b, preferred_element_type=jnp.float32)


def _dot_nt(a, b):
    return lax.dot_general(a, b, (((1,), (1,)), ((), ())), preferred_element_type=jnp.float32)


def _rms_scale(x, width):
    return lax.rsqrt(jnp.sum(x * x, axis=-1, keepdims=True) * (1.0 / width) + EPS)


def _rot_half64(x):
    return pltpu.roll(x, 32, 1) + pltpu.roll(x, 96, 1)


def _mla_proj_kernel(x_ref, gin_ref, wsm_ref, gqa_ref, gkva_ref, wuq_ref, wuk_ref, wvt_ref,
                     gq_ref, gk_ref, cos_ref, sin_ref, *rest, q_scale):
    if len(rest) == 6:
        rest[5][...] = rest[0][...].astype(jnp.bfloat16)
        rest = rest[1:5]
    q_ref, k_ref, v_ref, u_ref = rest
    x = x_ref[...]
    u = (x * _rms_scale(x, D_MODEL) * gin_ref[...]).astype(jnp.bfloat16)
    u_ref[...] = u
    p = _dot_nt(u, wsm_ref[...])
    cq = p[:, :MLA_Q_RANK]
    ckv = p[:, MLA_Q_RANK:MLA_Q_RANK + MLA_KV_RANK]
    kr = p[:, MLA_Q_RANK + MLA_KV_RANK:]
    cqn = (cq * _rms_scale(cq, MLA_Q_RANK) * gqa_ref[...]).astype(jnp.bfloat16)
    ckvn = (ckv * _rms_scale(ckv, MLA_KV_RANK) * gkva_ref[...]).astype(jnp.bfloat16)
    qf = _dot(cqn, wuq_ref[...])
    kf = _dot(ckvn, wuk_ref[...])
    vt = _dot_nt(wvt_ref[...], ckvn).astype(jnp.bfloat16)

    cos = cos_ref[...]
    sin = sin_ref[...]
    gq = gq_ref[...]
    gk = gk_ref[...]
    gq_nope, gq_rope = gq[:, :LANES], gq[:, LANES:]
    gk_nope, gk_rope = gk[:, :LANES], gk[:, LANES:]

    ss_kr = jnp.sum(kr * kr, axis=-1, keepdims=True)
    krg = kr * gk_rope
    kr_roped = krg * cos + _rot_half64(krg) * sin

    for h in range(MLA_HEADS):
        qh = qf[:, h * HEAD_PAD:(h + 1) * HEAD_PAD]
        rq = _rms_scale(qh, MLA_QK)
        q_nope = qh[:, :LANES] * rq * gq_nope
        q_r = qh[:, LANES:] * rq * gq_rope
        q_r = q_r * cos + _rot_half64(q_r) * sin
        q_ref[0, h, :, :LANES] = (q_nope * q_scale).astype(jnp.bfloat16)
        q_ref[0, h, :, LANES:] = (q_r * q_scale).astype(jnp.bfloat16)

        k_nope = kf[:, h * MLA_NOPE:(h + 1) * MLA_NOPE]
        rk = lax.rsqrt((jnp.sum(k_nope * k_nope, axis=-1, keepdims=True) + ss_kr) * (1.0 / MLA_QK) + EPS)
        k_ref[0, h, :, :LANES] = (k_nope * rk * gk_nope).astype(jnp.bfloat16)
        k_ref[0, h, :, LANES:] = (kr_roped * rk).astype(jnp.bfloat16)
        v_ref[0, h, :MLA_V, :] = vt[h * MLA_V:(h + 1) * MLA_V, :]
        v_ref[0, h, MLA_V:, :] = jnp.ones((ONES_ROWS, vt.shape[1]), jnp.bfloat16)


def _mla_proj(x3, gin, wsm, gqa, gkva, wuq, wuk, wvt, gq, gk, cos, sin, tm, w_wide=None, wide_row0=0):
    nb, rows, _ = x3.shape
    const = lambda b, i: (0, 0)
    tpb = rows // tm
    kern = functools.partial(_mla_proj_kernel, q_scale=LOG2E * MLA_QK ** -0.5)
    extra_in, extra_out_specs, extra_out_shape, extra_args = [], [], [], []
    if w_wide is not None:
        n_wide = w_wide.shape[0] - wide_row0
        slab = n_wide // (nb * tpb)
        sub = 8
        assert wide_row0 % sub == 0 and slab % sub == 0
        extra_in = [pl.BlockSpec((pl.Element(slab), pl.Element(D_MODEL)),
                                 lambda b, i: ((wide_row0 // sub + (b * tpb + i) * (slab // sub)) * sub, 0))]
        extra_out_specs = [pl.BlockSpec((slab, D_MODEL), lambda b, i: (b * tpb + i, 0))]
        extra_out_shape = [jax.ShapeDtypeStruct((n_wide, D_MODEL), jnp.bfloat16)]
        extra_args = [w_wide]
    return pl.pallas_call(
        kern,
        grid=(nb, rows // tm),
        in_specs=[
            pl.BlockSpec((None, tm, D_MODEL), lambda b, i: (b, i, 0)),
            pl.BlockSpec((1, D_MODEL), const),
            pl.BlockSpec((SMALL_W, D_MODEL), const),
            pl.BlockSpec((1, MLA_Q_RANK), const),
            pl.BlockSpec((1, MLA_KV_RANK), const),
            pl.BlockSpec((MLA_Q_RANK, MLA_HEADS * HEAD_PAD), const),
            pl.BlockSpec((MLA_KV_RANK, MLA_HEADS * MLA_NOPE), const),
            pl.BlockSpec((MLA_HEADS * MLA_V, MLA_KV_RANK), const),
            pl.BlockSpec((1, HEAD_PAD), const),
            pl.BlockSpec((1, HEAD_PAD), const),
            pl.BlockSpec((tm, LANES), lambda b, i: (i, 0)),
            pl.BlockSpec((tm, LANES), lambda b, i: (i, 0)),
        ] + extra_in,
        out_specs=[
            pl.BlockSpec((1, MLA_HEADS, tm, HEAD_PAD), lambda b, i: (b, 0, i, 0)),
            pl.BlockSpec((1, MLA_HEADS, tm, HEAD_PAD), lambda b, i: (b, 0, i, 0)),
            pl.BlockSpec((1, MLA_HEADS, MLA_V + ONES_ROWS, tm), lambda b, i: (b, 0, 0, i)),
            pl.BlockSpec((None, tm, D_MODEL), lambda b, i: (b, i, 0)),
        ] + extra_out_specs,
        out_shape=[
            jax.ShapeDtypeStruct((nb, MLA_HEADS, rows, HEAD_PAD), jnp.bfloat16),
            jax.ShapeDtypeStruct((nb, MLA_HEADS, rows, HEAD_PAD), jnp.bfloat16),
            jax.ShapeDtypeStruct((nb, MLA_HEADS, MLA_V + ONES_ROWS, rows), jnp.bfloat16),
            jax.ShapeDtypeStruct((nb, rows, D_MODEL), jnp.bfloat16),
        ] + extra_out_shape,
        compiler_params=pltpu.CompilerParams(
            dimension_semantics=("arbitrary", "arbitrary"), vmem_limit_bytes=VMEM_LIMIT_BYTES),
        name="mla_proj",
    )(x3, gin, wsm, gqa, gkva, wuq, wuk, wvt, gq, gk, cos, sin, *extra_args)


def _seg_proj_kernel(u_ref, w_ref, *rest, kind, has_gain, has_cast):
    rest = list(rest)
    gain_ref = rest.pop(0) if has_gain else None
    cast_src_ref = rest.pop(0) if has_cast else None
    o_ref = rest.pop(0)
    if has_cast:
        rest.pop(0)[...] = cast_src_ref[...].astype(jnp.bfloat16)
    acc = _dot_nt(u_ref[...], w_ref[...])
    if kind == "silu":
        o_ref[...] = (acc / (1.0 + jnp.exp(-acc))).astype(o_ref.dtype)
    elif kind == "sigmoid":
        o_ref[...] = (1.0 / (1.0 + jnp.exp(-acc))).astype(o_ref.dtype)
    elif kind == "copy":
        o_ref[...] = acc.astype(o_ref.dtype)
    else:
        gain = gain_ref[...]
        for c in range(acc.shape[1] // DIFF_QK):
            cols = slice(c * DIFF_QK, (c + 1) * DIFF_QK)
            g = acc[:, cols]
            o_ref[:, cols] = (g * _rms_scale(g, DIFF_QK) * gain[:, cols]).astype(o_ref.dtype)


def _seg_proj(u2, wbig, segs, kind, tm, tn, gain=None, cast_src=None):
    rows = u2.shape[0]
    tps = D_MODEL // tn
    nj = len(segs) * tps
    if len(segs) == 1:
        w_map = lambda i, j: (segs[0] * tps + j, 0)
    else:
        w_map = lambda i, j: (jnp.where(j < tps, segs[0] * tps + j, segs[1] * tps + j - tps), 0)
    in_specs = [pl.BlockSpec((tm, D_MODEL), lambda i, j: (i, 0)), pl.BlockSpec((tn, D_MODEL), w_map)]
    args = [u2, wbig]
    out_specs = [pl.BlockSpec((tm, tn), lambda i, j: (i, j))]
    out_shape = [jax.ShapeDtypeStruct((rows, len(segs) * D_MODEL), jnp.bfloat16)]
    if gain is not None:
        in_specs.append(pl.BlockSpec((1, tn), lambda i, j: (0, j)))
        args.append(gain)
    if cast_src is not None:
        slab = cast_src.shape[0] // ((rows // tm) * nj)
        slab_spec = pl.BlockSpec((slab, cast_src.shape[1]), lambda i, j: (i * nj + j, 0))
        in_specs.append(slab_spec)
        args.append(cast_src)
        out_specs.append(slab_spec)
        out_shape.append(jax.ShapeDtypeStruct(cast_src.shape, jnp.bfloat16))
    res = pl.pallas_call(
        functools.partial(_seg_proj_kernel, kind=kind, has_gain=gain is not None, has_cast=cast_src is not None),
        grid=(rows // tm, nj),
        in_specs=in_specs,
        out_specs=out_specs,
        out_shape=out_shape,
        compiler_params=pltpu.CompilerParams(
            dimension_semantics=("arbitrary", "arbitrary"), vmem_limit_bytes=VMEM_LIMIT_BYTES),
        name="in_proj_" + kind,
    )(*args)
    return res if cast_src is not None else res[0]


MLA_PER_STEP = MLA_HEADS // DIFF_HEADS


def _lane_fold_max(x):
    blocks = [x[:, i * LANES:(i + 1) * LANES] for i in range(x.shape[1] // LANES)]
    return functools.reduce(jnp.maximum, blocks)


def _mla_tile_fns(q_ref, k_ref, vt_ref, km_ref, vmt_ref, za_ref, qn_ref, kn_ref, kmn_ref,
                  o_ref, s_ref, sm_ref, mx_ref):
    n_t = q_ref.shape[2] // TQ

    def scores_into(slot, qt, k, km):
        s = _dot_nt(k, qt)
        sm = _dot_nt(km, qt)
        key = lax.broadcasted_iota(jnp.int32, sm.shape, 0)
        sm = jnp.where(key < N_META, sm, MASK_VALUE)
        s_ref[slot] = s
        sm_ref[slot] = sm
        mx_ref[slot] = jnp.maximum(jnp.max(s, axis=0, keepdims=True), jnp.max(sm, axis=0, keepdims=True))

    def scores(u, slot):
        hh, t = divmod(u, n_t)
        scores_into(slot, q_ref[0, hh, t * TQ:(t + 1) * TQ, :], k_ref[0, hh], km_ref[hh])

    def scores_next(slot):
        scores_into(slot, qn_ref[0, 0], kn_ref[0, 0], kmn_ref[0])

    def finish(u, slot):
        hh, t = divmod(u, n_t)
        rows = slice(t * TQ, (t + 1) * TQ)
        cols = slice(hh * MLA_V, (hh + 1) * MLA_V)
        m = mx_ref[slot]
        p = jnp.exp2(s_ref[slot] - m).astype(jnp.bfloat16)
        pm = jnp.exp2(sm_ref[slot] - m).astype(jnp.bfloat16)
        o = _dot(vt_ref[0, hh], p) + _dot(vmt_ref[hh], pm)
        l = o[MLA_V:MLA_V + 1, :]
        y = (o[:MLA_V, :] * (1.0 / l)).T
        o_ref[rows, cols] = (y * za_ref[rows, cols].astype(jnp.float32)).astype(o_ref.dtype)

    return scores, scores_next, finish


def _diff_tile_fns(qd_ref, kd_ref, vd_ref, kdm_ref, vdm_ref, btab_ref, mtab_ref, lam_ref, subln_ref, zb_ref,
                   qdn_ref, kdn_ref, kdmn_ref, btabn_ref, mtabn_ref, o_ref, s_ref, sm_ref, mx_ref):
    n_chunks = kd_ref.shape[0] // KC
    lv = lam_ref[...]
    lam = (jnp.exp(jnp.sum(lv[0:1] * lv[1:2], axis=-1, keepdims=True))
           - jnp.exp(jnp.sum(lv[2:3] * lv[3:4], axis=-1, keepdims=True)) + LAM_INIT)

    c_neg = btab_ref[0, 0, 0:1, 0:1]
    c_pos = btab_ref[0, 4, 0:1, 0:1]

    def scores_into(t, slot, q_tile_ref, k_ref_, km_ref_, bt_ref, mt_ref):
        cn, cp = bt_ref[0, 0, 0:1, 0:1], bt_ref[0, 4, 0:1, 0:1]
        for half in range(2):
            lo = half * DIFF_QK
            qm = q_tile_ref[:, lo:lo + DIFF_QK]
            sm = _dot_nt(qm, km_ref_[:, lo:lo + DIFF_QK]) + mt_ref[0, min(t, 1)]
            sm_ref[slot, half] = sm
            mx = {"band": sm, "neg": None, "pos": None}
            for c in range(n_chunks):
                sc = _dot_nt(qm, k_ref_[c * KC:(c + 1) * KC, lo:lo + DIFF_QK])
                if abs(c - t) <= 1:
                    sc = sc + bt_ref[0, c - t + 2]
                    grp = "band"
                else:
                    grp = "neg" if c < t else "pos"
                s_ref[slot, half, :, c * KC:(c + 1) * KC] = sc
                fold = _lane_fold_max(sc)
                mx[grp] = fold if mx[grp] is None else jnp.maximum(mx[grp], fold)
            acc = mx["band"]
            if mx["neg"] is not None:
                acc = jnp.maximum(acc, mx["neg"] + cn)
            if mx["pos"] is not None:
                acc = jnp.maximum(acc, mx["pos"] + cp)
            mx_ref[slot, half] = acc

    def scores(t, slot):
        scores_into(t, slot, qd_ref.at[t * TQ:(t + 1) * TQ], kd_ref, kdm_ref, btab_ref, mtab_ref)

    def scores_next(slot):
        scores_into(0, slot, qdn_ref, kdn_ref, kdmn_ref, btabn_ref, mtabn_ref)

    def softmax_parts(t, slot, half):
        m = jnp.max(mx_ref[slot, half], axis=-1, keepdims=True)
        m_neg = m - c_neg
        m_pos = m - c_pos
        chunks = []
        for c in range(n_chunks):
            mc = m if abs(c - t) <= 1 else (m_neg if c < t else m_pos)
            chunks.append(jnp.exp2(s_ref[slot, half, :, c * KC:(c + 1) * KC] - mc))
        p = jnp.concatenate(chunks, axis=-1)
        pm = jnp.exp2(sm_ref[slot, half] - m)
        l = jnp.sum(p, axis=-1, keepdims=True) + jnp.sum(pm, axis=-1, keepdims=True)
        return p, pm, l

    def finish(t, slot):
        rows = slice(t * TQ, (t + 1) * TQ)
        p1, pm1, l1 = softmax_parts(t, slot, 0)
        p2, pm2, l2 = softmax_parts(t, slot, 1)
        r = lam * l1 * (1.0 / l2)
        a = (p1 - p2 * r).astype(jnp.bfloat16)
        am = (pm1 - pm2 * r).astype(jnp.bfloat16)
        o = (_dot(a, vd_ref[...]) + _dot(am, vdm_ref[...])) * (1.0 / l1)
        y = o * _rms_scale(o, DIFF_V) * subln_ref[...] * (1.0 - LAM_INIT)
        y = y * zb_ref[rows, :].astype(jnp.float32)
        o_ref[rows, :] = y.astype(o_ref.dtype)

    return scores, scores_next, finish


def _attn_kernel(q_ref, k_ref, vt_ref, km_ref, vmt_ref, za_ref,
                 qd_ref, kd_ref, vd_ref, kdm_ref, vdm_ref, btab_ref, mtab_ref, lam_ref, subln_ref, zb_ref,
                 qn_ref, kn_ref, kmn_ref, qdn_ref, kdn_ref, kdmn_ref, btabn_ref, mtabn_ref,
                 oa_ref, ob_ref, ms_ref, msm_ref, mmx_ref, ds_ref, dsm_ref, dmx_ref):
    m_scores, m_scores_next, m_finish = _mla_tile_fns(
        q_ref, k_ref, vt_ref, km_ref, vmt_ref, za_ref, qn_ref, kn_ref, kmn_ref, oa_ref,
        ms_ref, msm_ref, mmx_ref)
    d_scores, d_scores_next, d_finish = _diff_tile_fns(
        qd_ref, kd_ref, vd_ref, kdm_ref, vdm_ref, btab_ref, mtab_ref, lam_ref, subln_ref, zb_ref,
        qdn_ref, kdn_ref, kdmn_ref, btabn_ref, mtabn_ref, ob_ref, ds_ref, dsm_ref, dmx_ref)
    n_t = qd_ref.shape[0] // TQ
    n_u = n_t * MLA_PER_STEP

    @pl.when((pl.program_id(0) == 0) & (pl.program_id(1) == 0))
    def _():
        d_scores(0, 0)
        m_scores(0, 0)

    for t in range(n_t):
        if t + 1 < n_t:
            d_scores(t + 1, (t + 1) % 2)
        else:
            d_scores_next(0)
        for u in range(t * MLA_PER_STEP, (t + 1) * MLA_PER_STEP):
            if u + 1 < n_u:
                m_scores(u + 1, (u + 1) % 2)
            else:
                m_scores_next(0)
            m_finish(u, u % 2)
        d_finish(t, t % 2)


def _attention(q, k, vt, km, vmt, zs, qk, vd, qk_meta, vd_meta, btab, mtab, lam, subln, seq):
    nb = q.shape[0]
    cps = D_MODEL // DIFF_V
    mps = MLA_PER_STEP
    n_steps = nb * DIFF_HEADS

    def nxt(b, h):
        g = jnp.minimum(b * DIFF_HEADS + h + 1, n_steps - 1)
        return g // DIFF_HEADS, g % DIFF_HEADS

    def nb_(b, h):
        return nxt(b, h)[0]

    def nh_(b, h):
        return nxt(b, h)[1]

    return pl.pallas_call(
        _attn_kernel,
        grid=(nb, DIFF_HEADS),
        in_specs=[
            pl.BlockSpec((1, mps, seq, HEAD_PAD), lambda b, h: (b, h, 0, 0)),
            pl.BlockSpec((1, mps, seq, HEAD_PAD), lambda b, h: (b, h, 0, 0)),
            pl.BlockSpec((1, mps, MLA_V + ONES_ROWS, seq), lambda b, h: (b, h, 0, 0)),
            pl.BlockSpec((mps, META_PAD, HEAD_PAD), lambda b, h: (h, 0, 0)),
            pl.BlockSpec((mps, MLA_V + ONES_ROWS, META_PAD), lambda b, h: (h, 0, 0)),
            pl.BlockSpec((seq, mps * MLA_V), lambda b, h: (b, h)),
            pl.BlockSpec((seq, DIFF_V), lambda b, h: (b, h)),
            pl.BlockSpec((seq, DIFF_V), lambda b, h: (b, cps + h)),
            pl.BlockSpec((seq, DIFF_V), lambda b, h: (b, h)),
            pl.BlockSpec((META_PAD, DIFF_V), lambda b, h: (0, h)),
            pl.BlockSpec((META_PAD, DIFF_V), lambda b, h: (0, h)),
            pl.BlockSpec((1, 5, TQ, KC), lambda b, h: (h, 0, 0, 0)),
            pl.BlockSpec((1, 2, TQ, META_PAD), lambda b, h: (h, 0, 0, 0)),
            pl.BlockSpec((4, DIFF_QK), lambda b, h: (0, 0)),
            pl.BlockSpec((1, DIFF_V), lambda b, h: (0, 0)),
            pl.BlockSpec((seq, DIFF_V), lambda b, h: (b, cps + h)),
            pl.BlockSpec((1, 1, TQ, HEAD_PAD), lambda b, h: (nb_(b, h), mps * nh_(b, h), 0, 0)),
            pl.BlockSpec((1, 1, seq, HEAD_PAD), lambda b, h: (nb_(b, h), mps * nh_(b, h), 0, 0)),
            pl.BlockSpec((1, META_PAD, HEAD_PAD), lambda b, h: (mps * nh_(b, h), 0, 0)),
            pl.BlockSpec((TQ, DIFF_V), lambda b, h: (nb_(b, h) * (seq // TQ), nh_(b, h))),
            pl.BlockSpec((seq, DIFF_V), lambda b, h: (nb_(b, h), cps + nh_(b, h))),
            pl.BlockSpec((META_PAD, DIFF_V), lambda b, h: (0, nh_(b, h))),
            pl.BlockSpec((1, 5, TQ, KC), lambda b, h: (nh_(b, h), 0, 0, 0)),
            pl.BlockSpec((1, 2, TQ, META_PAD), lambda b, h: (nh_(b, h), 0, 0, 0)),
        ],
        out_specs=[
            pl.BlockSpec((seq, mps * MLA_V), lambda b, h: (b, h)),
            pl.BlockSpec((seq, DIFF_V), lambda b, h: (b, h)),
        ],
        out_shape=[
            jax.ShapeDtypeStruct((nb * seq, MLA_HEADS * MLA_V), jnp.bfloat16),
            jax.ShapeDtypeStruct((nb * seq, DIFF_HEADS * DIFF_V), jnp.bfloat16),
        ],
        scratch_shapes=[
            pltpu.VMEM((2, seq, TQ), jnp.float32), pltpu.VMEM((2, META_PAD, TQ), jnp.float32),
            pltpu.VMEM((2, 1, TQ), jnp.float32),
            pltpu.VMEM((2, 2, TQ, seq), jnp.float32), pltpu.VMEM((2, 2, TQ, META_PAD), jnp.float32),
            pltpu.VMEM((2, 2, TQ, LANES), jnp.float32),
        ],
        compiler_params=pltpu.CompilerParams(
            dimension_semantics=("arbitrary", "arbitrary"), vmem_limit_bytes=VMEM_LIMIT_BYTES),
        name="attention",
    )(q, k, vt, km, vmt, zs, qk, qk, vd, qk_meta, vd_meta, btab, mtab, lam, subln, zs,
      q, k, km, qk, qk, qk_meta, btab, mtab)


def _merge_kernel(oa_ref, ob_ref, wa_ref, wb_ref, ga_ref, gb_ref, m_ref):
    ya = _dot(oa_ref[...], wa_ref[...])
    yb = _dot(ob_ref[...], wb_ref[...])
    m = ga_ref[...].astype(jnp.float32) * ya + gb_ref[...].astype(jnp.float32) * yb
    m_ref[...] = m.astype(m_ref.dtype)


def _merge(oa, ob, wa, wb, gs, tm, tn):
    rows = oa.shape[0]
    tps = D_MODEL // tn
    return pl.pallas_call(
        _merge_kernel,
        grid=(rows // tm, D_MODEL // tn),
        in_specs=[
            pl.BlockSpec((tm, D_MODEL), lambda i, j: (i, 0)),
            pl.BlockSpec((tm, D_MODEL), lambda i, j: (i, 0)),
            pl.BlockSpec((D_MODEL, tn), lambda i, j: (0, j)),
            pl.BlockSpec((D_MODEL, tn), lambda i, j: (0, j)),
            pl.BlockSpec((tm, tn), lambda i, j: (i, j)),
            pl.BlockSpec((tm, tn), lambda i, j: (i, tps + j)),
        ],
        out_specs=pl.BlockSpec((tm, tn), lambda i, j: (i, j)),
        out_shape=jax.ShapeDtypeStruct((rows, D_MODEL), jnp.bfloat16),
        compiler_params=pltpu.CompilerParams(
            dimension_semantics=("arbitrary", "arbitrary"), vmem_limit_bytes=VMEM_LIMIT_BYTES),
        name="merge",
    )(oa, ob, wa, wb, gs, gs)


def _out_kernel(m_ref, w_ref, x_ref, o_ref):
    o_ref[...] = x_ref[...] + _dot(m_ref[...], w_ref[...])


def _out_proj(m, wout, x2, tm, tn):
    rows = m.shape[0]
    return pl.pallas_call(
        _out_kernel,
        grid=(rows // tm, D_MODEL // tn),
        in_specs=[
            pl.BlockSpec((tm, D_MODEL), lambda i, j: (i, 0)),
            pl.BlockSpec((D_MODEL, tn), lambda i, j: (0, j)),
            pl.BlockSpec((tm, tn), lambda i, j: (i, j)),
        ],
        out_specs=pl.BlockSpec((tm, tn), lambda i, j: (i, j)),
        out_shape=jax.ShapeDtypeStruct((rows, D_MODEL), jnp.float32),
        compiler_params=pltpu.CompilerParams(
            dimension_semantics=("arbitrary", "arbitrary"), vmem_limit_bytes=VMEM_LIMIT_BYTES),
        name="out_proj",
    )(m, wout, x2)


def _t5_bucket(rel):
    nb = REL_BUCKETS // 2
    max_exact = nb // 2
    ret = jnp.where(rel > 0, nb, 0)
    n = jnp.abs(rel)
    nf = jnp.maximum(n, 1).astype(jnp.float32)
    large = max_exact + (jnp.log(nf / max_exact) / math.log(REL_MAX_DIST / max_exact)
                         * (nb - max_exact)).astype(jnp.int32)
    large = jnp.minimum(large, nb - 1)
    return ret + jnp.where(n < max_exact, n, large)


SEG_W = 2 * KC


def _bias_tab_kernel(seg_ref, btab_ref, mtab_ref):
    def expand(row):
        x = jnp.broadcast_to(seg_ref[0, row:row + 1, :], (TQ, SEG_W))
        return pltpu.roll(x, 0, 1, stride=1, stride_axis=0) * LOG2E

    for t in range(5):
        btab_ref[0, t] = expand(t)[:, KC:]
    lane = lax.broadcasted_iota(jnp.int32, (TQ, META_PAD), 1)
    for t in range(2):
        mtab_ref[0, t] = jnp.where(lane < N_META, expand(5 + t)[:, KC:KC + META_PAD], MASK_VALUE)


def _bias_tables(rel_bias):
    i = jnp.arange(SEG_W, dtype=jnp.int32)[None, :]
    dd = jnp.arange(-2, 3, dtype=jnp.int32)[:, None]
    rel_seq = dd * KC - KC + i
    rel_meta = i - KC - N_META - jnp.arange(2, dtype=jnp.int32)[:, None] * TQ
    rel = jnp.concatenate([rel_seq, rel_meta, rel_meta[:1]], axis=0)
    seg = jnp.transpose(rel_bias[_t5_bucket(rel)], (2, 0, 1)).astype(jnp.float32)
    return pl.pallas_call(
        _bias_tab_kernel,
        grid=(DIFF_HEADS,),
        in_specs=[pl.BlockSpec((1, 8, SEG_W), lambda h: (h, 0, 0))],
        out_specs=[
            pl.BlockSpec((1, 5, TQ, KC), lambda h: (h, 0, 0, 0)),
            pl.BlockSpec((1, 2, TQ, META_PAD), lambda h: (h, 0, 0, 0)),
        ],
        out_shape=[
            jax.ShapeDtypeStruct((DIFF_HEADS, 5, TQ, KC), jnp.float32),
            jax.ShapeDtypeStruct((DIFF_HEADS, 2, TQ, META_PAD), jnp.float32),
        ],
        compiler_params=pltpu.CompilerParams(dimension_semantics=("arbitrary",)),
        name="bias_tables",
    )(seg)


def _rope_tables(n_pos):
    half = MLA_ROPE // 2
    inv = ROPE_THETA ** (-jnp.arange(half, dtype=jnp.float32) / half)
    ang = jnp.arange(n_pos, dtype=jnp.int32).astype(jnp.float32)[:, None] * inv[None, :]
    c, s = jnp.cos(ang), jnp.sin(ang)
    z = jnp.zeros((n_pos, LANES - MLA_ROPE), jnp.float32)
    cos_t = jnp.concatenate([c, c, z], axis=-1)
    sin_t = jnp.concatenate([-s, s, z], axis=-1)
    return cos_t, sin_t


def _pad_rows(a, rows):
    return jnp.pad(a, [(0, rows - a.shape[0])] + [(0, 0)] * (a.ndim - 1))


def kernel(x, meta_tokens, rel_bias, norm_in, w_in, q_a_norm, kv_a_norm, w_uq, w_ukv, mla_q_norm, mla_k_norm,
           diff_q_norm, diff_k_norm, diff_lambda, diff_subln, w_branch_a, w_branch_b, w_out):
    nb, seq, _ = x.shape
    bf = jnp.bfloat16
    n_small = MLA_Q_RANK + MLA_KV_RANK + MLA_ROPE

    w_in_t = jnp.swapaxes(w_in[0], 0, 1)
    wsm = jnp.pad(w_in_t[:n_small], ((0, SMALL_W - n_small), (0, 0))).astype(bf)
    wuq = jnp.pad(w_uq[0].reshape(MLA_Q_RANK, MLA_HEADS, MLA_QK),
                  ((0, 0), (0, 0), (0, HEAD_PAD - MLA_QK))).reshape(MLA_Q_RANK, MLA_HEADS * HEAD_PAD).astype(bf)
    wukv = w_ukv[0].reshape(MLA_KV_RANK, MLA_HEADS, MLA_NOPE + MLA_V)
    wuk = wukv[:, :, :MLA_NOPE].reshape(MLA_KV_RANK, MLA_HEADS * MLA_NOPE).astype(bf)
    wvt = jnp.swapaxes(wukv[:, :, MLA_NOPE:].reshape(MLA_KV_RANK, MLA_HEADS * MLA_V), 0, 1).astype(bf)

    gin = norm_in[0][None]
    gqa, gkva = q_a_norm[0][None], kv_a_norm[0][None]
    gq = jnp.pad(mla_q_norm[0], (0, HEAD_PAD - MLA_QK))[None]
    gk = jnp.pad(mla_k_norm[0], (0, HEAD_PAD - MLA_QK))[None]
    gqd, gkd = diff_q_norm[0][None], diff_k_norm[0][None]
    subln = diff_subln[0][None]
    lam = diff_lambda[0]

    cos_t, sin_t = _rope_tables(N_META + seq)
    btab, mtab = _bias_tables(rel_bias)

    x2 = x.reshape(nb * seq, D_MODEL)
    meta = meta_tokens.astype(x.dtype)

    qd_scale = LOG2E * DIFF_QK ** -0.5
    reps = D_MODEL // DIFF_QK
    gain_qk = jnp.concatenate([jnp.tile(gqd * qd_scale, (1, reps)), jnp.tile(gkd, (1, reps))], axis=-1)

    q, k, vt, u, wbig = _mla_proj(x, gin, wsm, gqa, gkva, wuq, wuk, wvt, gq, gk, cos_t[N_META:], sin_t[N_META:],
                                  tm=MLA_PROJ_TM, w_wide=w_in_t, wide_row0=n_small)
    u2 = u.reshape(nb * seq, D_MODEL)
    zs, wa = _seg_proj(u2, wbig, (SEG_ZA, SEG_ZB), "silu", IN_PROJ_TM, IN_PROJ_TN, cast_src=w_branch_a[0])
    qk, wb = _seg_proj(u2, wbig, (SEG_QD, SEG_KD), "norm", IN_PROJ_TM, IN_PROJ_TN, gain=gain_qk,
                       cast_src=w_branch_b[0])
    vd = _seg_proj(u2, wbig, (SEG_VD,), "copy", IN_PROJ_TM, IN_PROJ_TN)
    gs, wo = _seg_proj(u2, wbig, (SEG_GA, SEG_GB), "sigmoid", IN_PROJ_TM, IN_PROJ_TN, cast_src=w_out[0])

    _, km, vmt, um = _mla_proj(meta[None], gin, wsm, gqa, gkva, wuq, wuk, wvt, gq, gk,
                              cos_t[:N_META], sin_t[:N_META], tm=N_META)
    km = jnp.pad(km[0], ((0, 0), (0, META_PAD - N_META), (0, 0)))
    vmt = jnp.pad(vmt[0], ((0, 0), (0, 0), (0, META_PAD - N_META)))
    k_meta = _pad_rows(_seg_proj(um[0], wbig, (SEG_KD,), "norm", N_META, META_TN, gain=gain_qk[:, D_MODEL:]),
                       META_PAD)
    vd_meta = _pad_rows(_seg_proj(um[0], wbig, (SEG_VD,), "copy", N_META, META_TN), META_PAD)

    oa, ob = _attention(q, k, vt, km, vmt, zs, qk, vd, k_meta, vd_meta, btab, mtab, lam, subln, seq)
    m = _merge(oa, ob, wa, wb, gs, tm=MERGE_TM, tn=MERGE_TN)
    out = _out_proj(m, wo, x2, tm=OUT_TM, tn=OUT_TN)
    return out.reshape(nb, seq, D_MODEL)
```

```python
import functools
import math

import jax
import jax.numpy as jnp
from jax import lax
from jax.experimental import pallas as pl
from jax.experimental.pallas import tpu as pltpu

D_MODEL = 2048
N_META = 16
EPS = 1e-6

MLA_HEADS = 16
MLA_Q_RANK = 768
MLA_KV_RANK = 512
MLA_NOPE = 128
MLA_ROPE = 64
MLA_V = 128
MLA_QK = MLA_NOPE + MLA_ROPE
ROPE_THETA = 10000.0

DIFF_HEADS = 8
DIFF_QK = 128
DIFF_V = 2 * DIFF_QK
LAM_INIT = 0.8 - 0.6 * math.exp(-0.3 * 0)

REL_BUCKETS = 32
REL_MAX_DIST = 128

LANES = 128
MXU_DIM = 256
VMEM_LIMIT_BYTES = 62 * 1024 * 1024

HEAD_PAD = MXU_DIM
SMALL_W = MLA_Q_RANK + MLA_KV_RANK + LANES
META_PAD = LANES
MASK_VALUE = -1e30
TQ = 256
KC = 256
LOG2E = math.log2(math.e)

MLA_PROJ_TM = 256
IN_PROJ_TM, IN_PROJ_TN = 1024, D_MODEL
META_TN = 1024
MERGE_TM, MERGE_TN = 1024, 1024
OUT_TM, OUT_TN = 512, D_MODEL

SEG_ZA, SEG_QD, SEG_KD, SEG_VD, SEG_ZB, SEG_GA, SEG_GB = range(7)


def _dot(a, b):
    return jnp.dot(a, b, preferred_element_type=jnp.float32)


def _dot_nt(a, b):
    return lax.dot_general(a, b, (((1,), (1,)), ((), ())), preferred_element_type=jnp.float32)


def _rms_scale(x, width):
    return lax.rsqrt(jnp.sum(x * x, axis=-1, keepdims=True) * (1.0 / width) + EPS)


def _rot_half64(x):
    return pltpu.roll(x, 32, 1) + pltpu.roll(x, 96, 1)


def _mla_proj_kernel(x_ref, gin_ref, wsm_ref, gqa_ref, gkva_ref, wuq_ref, wuk_ref, wvt_ref,
                     gq_ref, gk_ref, cos_ref, sin_ref, *rest, q_scale):
    if len(rest) == 6:
        rest[5][...] = rest[0][...].astype(jnp.bfloat16)
        rest = rest[1:5]
    q_ref, k_ref, v_ref, u_ref = rest
    x = x_ref[...]
    u = (x * _rms_scale(x, D_MODEL) * gin_ref[...]).astype(jnp.bfloat16)
    u_ref[...] = u
    p = _dot_nt(u, wsm_ref[...])
    cq = p[:, :MLA_Q_RANK]
    ckv = p[:, MLA_Q_RANK:MLA_Q_RANK + MLA_KV_RANK]
    kr = p[:, MLA_Q_RANK + MLA_KV_RANK:]
    cqn = (cq * _rms_scale(cq, MLA_Q_RANK) * gqa_ref[...]).astype(jnp.bfloat16)
    ckvn = (ckv * _rms_scale(ckv, MLA_KV_RANK) * gkva_ref[...]).astype(jnp.bfloat16)
    qf = _dot(cqn, wuq_ref[...])
    kf = _dot(ckvn, wuk_ref[...])
    vt = _dot_nt(wvt_ref[...], ckvn).astype(jnp.bfloat16)

    cos = cos_ref[...]
    sin = sin_ref[...]
    gq = gq_ref[...]
    gk = gk_ref[...]
    gq_nope, gq_rope = gq[:, :LANES], gq[:, LANES:]
    gk_nope, gk_rope = gk[:, :LANES], gk[:, LANES:]

    ss_kr = jnp.sum(kr * kr, axis=-1, keepdims=True)
    krg = kr * gk_rope
    kr_roped = krg * cos + _rot_half64(krg) * sin

    for h in range(MLA_HEADS):
        qh = qf[:, h * HEAD_PAD:(h + 1) * HEAD_PAD]
        rq = _rms_scale(qh, MLA_QK)
        q_nope = qh[:, :LANES] * rq * gq_nope
        q_r = qh[:, LANES:] * rq * gq_rope
        q_r = q_r * cos + _rot_half64(q_r) * sin
        q_ref[0, h, :, :LANES] = (q_nope * q_scale).astype(jnp.bfloat16)
        q_ref[0, h, :, LANES:] = (q_r * q_scale).astype(jnp.bfloat16)

        k_nope = kf[:, h * MLA_NOPE:(h + 1) * MLA_NOPE]
        rk = lax.rsqrt((jnp.sum(k_nope * k_nope, axis=-1, keepdims=True) + ss_kr) * (1.0 / MLA_QK) + EPS)
        k_ref[0, h, :, :LANES] = (k_nope * rk * gk_nope).astype(jnp.bfloat16)
        k_ref[0, h, :, LANES:] = (kr_roped * rk).astype(jnp.bfloat16)
        v_ref[0, h] = vt[h * MLA_V:(h + 1) * MLA_V, :]


def _mla_proj(x3, gin, wsm, gqa, gkva, wuq, wuk, wvt, gq, gk, cos, sin, tm, w_wide=None, wide_row0=0):
    nb, rows, _ = x3.shape
    const = lambda b, i: (0, 0)
    tpb = rows // tm
    kern = functools.partial(_mla_proj_kernel, q_scale=LOG2E * MLA_QK ** -0.5)
    extra_in, extra_out_specs, extra_out_shape, extra_args = [], [], [], []
    if w_wide is not None:
        n_wide = w_wide.shape[0] - wide_row0
        slab = n_wide // (nb * tpb)
        sub = 8
        assert wide_row0 % sub == 0 and slab % sub == 0
        extra_in = [pl.BlockSpec((pl.Element(slab), pl.Element(D_MODEL)),
                                 lambda b, i: ((wide_row0 // sub + (b * tpb + i) * (slab // sub)) * sub, 0))]
        extra_out_specs = [pl.BlockSpec((slab, D_MODEL), lambda b, i: (b * tpb + i, 0))]
        extra_out_shape = [jax.ShapeDtypeStruct((n_wide, D_MODEL), jnp.bfloat16)]
        extra_args = [w_wide]
    return pl.pallas_call(
        kern,
        grid=(nb, rows // tm),
        in_specs=[
            pl.BlockSpec((None, tm, D_MODEL), lambda b, i: (b, i, 0)),
            pl.BlockSpec((1, D_MODEL), const),
            pl.BlockSpec((SMALL_W, D_MODEL), const),
            pl.BlockSpec((1, MLA_Q_RANK), const),
            pl.BlockSpec((1, MLA_KV_RANK), const),
            pl.BlockSpec((MLA_Q_RANK, MLA_HEADS * HEAD_PAD), const),
            pl.BlockSpec((MLA_KV_RANK, MLA_HEADS * MLA_NOPE), const),
            pl.BlockSpec((MLA_HEADS * MLA_V, MLA_KV_RANK), const),
            pl.BlockSpec((1, HEAD_PAD), const),
            pl.BlockSpec((1, HEAD_PAD), const),
            pl.BlockSpec((tm, LANES), lambda b, i: (i, 0)),
            pl.BlockSpec((tm, LANES), lambda b, i: (i, 0)),
        ] + extra_in,
        out_specs=[
            pl.BlockSpec((1, MLA_HEADS, tm, HEAD_PAD), lambda b, i: (b, 0, i, 0)),
            pl.BlockSpec((1, MLA_HEADS, tm, HEAD_PAD), lambda b, i: (b, 0, i, 0)),
            pl.BlockSpec((1, MLA_HEADS, MLA_V, tm), lambda b, i: (b, 0, 0, i)),
            pl.BlockSpec((None, tm, D_MODEL), lambda b, i: (b, i, 0)),
        ] + extra_out_specs,
        out_shape=[
            jax.ShapeDtypeStruct((nb, MLA_HEADS, rows, HEAD_PAD), jnp.bfloat16),
            jax.ShapeDtypeStruct((nb, MLA_HEADS, rows, HEAD_PAD), jnp.bfloat16),
            jax.ShapeDtypeStruct((nb, MLA_HEADS, MLA_V, rows), jnp.bfloat16),
            jax.ShapeDtypeStruct((nb, rows, D_MODEL), jnp.bfloat16),
        ] + extra_out_shape,
        compiler_params=pltpu.CompilerParams(
            dimension_semantics=("arbitrary", "arbitrary"), vmem_limit_bytes=VMEM_LIMIT_BYTES),
        name="mla_proj",
    )(x3, gin, wsm, gqa, gkva, wuq, wuk, wvt, gq, gk, cos, sin, *extra_args)


def _seg_proj_kernel(u_ref, w_ref, *rest, kind, has_gain, has_cast):
    rest = list(rest)
    gain_ref = rest.pop(0) if has_gain else None
    cast_src_ref = rest.pop(0) if has_cast else None
    o_ref = rest.pop(0)
    if has_cast:
        rest.pop(0)[...] = cast_src_ref[...].astype(jnp.bfloat16)
    acc = _dot_nt(u_ref[...], w_ref[...])
    if kind == "silu":
        o_ref[...] = (acc * (0.5 * jnp.tanh(0.5 * acc) + 0.5)).astype(o_ref.dtype)
    elif kind == "sigmoid":
        o_ref[...] = (0.5 * jnp.tanh(0.5 * acc) + 0.5).astype(o_ref.dtype)
    elif kind == "copy":
        o_ref[...] = acc.astype(o_ref.dtype)
    else:
        gain = gain_ref[...]
        for c in range(acc.shape[1] // DIFF_QK):
            cols = slice(c * DIFF_QK, (c + 1) * DIFF_QK)
            g = acc[:, cols]
            o_ref[:, cols] = (g * _rms_scale(g, DIFF_QK) * gain[:, cols]).astype(o_ref.dtype)


def _seg_proj(u2, wbig, segs, kind, tm, tn, gain=None, cast_src=None):
    rows = u2.shape[0]
    tps = D_MODEL // tn
    nj = len(segs) * tps
    if len(segs) == 1:
        w_map = lambda i, j: (segs[0] * tps + j, 0)
    else:
        w_map = lambda i, j: (jnp.where(j < tps, segs[0] * tps + j, segs[1] * tps + j - tps), 0)
    in_specs = [pl.BlockSpec((tm, D_MODEL), lambda i, j: (i, 0)), pl.BlockSpec((tn, D_MODEL), w_map)]
    args = [u2, wbig]
    out_specs = [pl.BlockSpec((tm, tn), lambda i, j: (i, j))]
    out_shape = [jax.ShapeDtypeStruct((rows, len(segs) * D_MODEL), jnp.bfloat16)]
    if gain is not None:
        in_specs.append(pl.BlockSpec((1, tn), lambda i, j: (0, j)))
        args.append(gain)
    if cast_src is not None:
        slab = cast_src.shape[0] // ((rows // tm) * nj)
        slab_spec = pl.BlockSpec((slab, cast_src.shape[1]), lambda i, j: (i * nj + j, 0))
        in_specs.append(slab_spec)
        args.append(cast_src)
        out_specs.append(slab_spec)
        out_shape.append(jax.ShapeDtypeStruct(cast_src.shape, jnp.bfloat16))
    res = pl.pallas_call(
        functools.partial(_seg_proj_kernel, kind=kind, has_gain=gain is not None, has_cast=cast_src is not None),
        grid=(rows // tm, nj),
        in_specs=in_specs,
        out_specs=out_specs,
        out_shape=out_shape,
        compiler_params=pltpu.CompilerParams(
            dimension_semantics=("arbitrary", "arbitrary"), vmem_limit_bytes=VMEM_LIMIT_BYTES),
        name="in_proj_" + kind,
    )(*args)
    return res if cast_src is not None else res[0]


MLA_PER_STEP = MLA_HEADS // DIFF_HEADS


def _lane_fold_max(x):
    blocks = [x[:, i * LANES:(i + 1) * LANES] for i in range(x.shape[1] // LANES)]
    return functools.reduce(jnp.maximum, blocks)


ONES_ROWS = 16


def _mla_tile_fns(q_ref, k_ref, vt_ref, km_ref, vmt_ref, za_ref, qn_ref, kn_ref, kmn_ref,
                  o_ref, s_ref, sm_ref, mx_ref):
    n_t = q_ref.shape[2] // TQ

    def scores_into(slot, qt, k, km):
        s = _dot_nt(k, qt)
        sm = _dot_nt(km, qt)
        key = lax.broadcasted_iota(jnp.int32, sm.shape, 0)
        sm = jnp.where(key < N_META, sm, MASK_VALUE)
        s_ref[slot] = s
        sm_ref[slot] = sm
        mx_ref[slot] = jnp.maximum(jnp.max(s, axis=0, keepdims=True), jnp.max(sm, axis=0, keepdims=True))

    def scores(u, slot):
        hh, t = divmod(u, n_t)
        scores_into(slot, q_ref[0, hh, t * TQ:(t + 1) * TQ, :], k_ref[0, hh], km_ref[hh])

    def scores_next(slot):
        scores_into(slot, qn_ref[0, 0], kn_ref[0, 0], kmn_ref[0])

    def finish(u, slot):
        hh, t = divmod(u, n_t)
        rows = slice(t * TQ, (t + 1) * TQ)
        cols = slice(hh * MLA_V, (hh + 1) * MLA_V)
        m = mx_ref[slot]
        p = jnp.exp2(s_ref[slot] - m).astype(jnp.bfloat16)
        pm = jnp.exp2(sm_ref[slot] - m).astype(jnp.bfloat16)
        ones = jnp.ones((ONES_ROWS, vt_ref.shape[3]), jnp.bfloat16)
        ones_m = jnp.ones((ONES_ROWS, META_PAD), jnp.bfloat16)
        o = (_dot(jnp.concatenate([vt_ref[0, hh], ones], axis=0), p)
             + _dot(jnp.concatenate([vmt_ref[hh], ones_m], axis=0), pm))
        l = o[MLA_V:MLA_V + 1, :]
        y = (o[:MLA_V, :] * (1.0 / l)).T
        o_ref[rows, cols] = (y * za_ref[rows, cols].astype(jnp.float32)).astype(o_ref.dtype)

    return scores, scores_next, finish


def _diff_tile_fns(qd_ref, kd_ref, vd_ref, kdm_ref, vdm_ref, btab_ref, mtab_ref, lam_ref, subln_ref, zb_ref,
                   qdn_ref, kdn_ref, kdmn_ref, btabn_ref, mtabn_ref, o_ref, s_ref, sm_ref, mx_ref):
    n_chunks = kd_ref.shape[0] // KC
    lv = lam_ref[...]
    lam = (jnp.exp(jnp.sum(lv[0:1] * lv[1:2], axis=-1, keepdims=True))
           - jnp.exp(jnp.sum(lv[2:3] * lv[3:4], axis=-1, keepdims=True)) + LAM_INIT)

    c_neg = btab_ref[0, 0, 0:1, 0:1]
    c_pos = btab_ref[0, 4, 0:1, 0:1]

    def scores_into(t, slot, q_tile_ref, k_ref_, km_ref_, bt_ref, mt_ref):
        cn, cp = bt_ref[0, 0, 0:1, 0:1], bt_ref[0, 4, 0:1, 0:1]
        for half in range(2):
            lo = half * DIFF_QK
            qm = q_tile_ref[:, lo:lo + DIFF_QK]
            sm = _dot_nt(qm, km_ref_[:, lo:lo + DIFF_QK]) + mt_ref[0, min(t, 1)]
            sm_ref[slot, half] = sm
            mx = {"band": sm, "neg": None, "pos": None}
            for c in range(n_chunks):
                sc = _dot_nt(qm, k_ref_[c * KC:(c + 1) * KC, lo:lo + DIFF_QK])
                if abs(c - t) <= 1:
                    sc = sc + bt_ref[0, c - t + 2]
                    grp = "band"
                else:
                    grp = "neg" if c < t else "pos"
                s_ref[slot, half, :, c * KC:(c + 1) * KC] = sc
                fold = _lane_fold_max(sc)
                mx[grp] = fold if mx[grp] is None else jnp.maximum(mx[grp], fold)
            acc = mx["band"]
            if mx["neg"] is not None:
                acc = jnp.maximum(acc, mx["neg"] + cn)
            if mx["pos"] is not None:
                acc = jnp.maximum(acc, mx["pos"] + cp)
            mx_ref[slot, half] = acc

    def scores(t, slot):
        scores_into(t, slot, qd_ref.at[t * TQ:(t + 1) * TQ], kd_ref, kdm_ref, btab_ref, mtab_ref)

    def scores_next(slot):
        scores_into(0, slot, qdn_ref, kdn_ref, kdmn_ref, btabn_ref, mtabn_ref)

    def softmax_parts(t, slot, half):
        m = jnp.max(mx_ref[slot, half], axis=-1, keepdims=True)
        m_neg = m - c_neg
        m_pos = m - c_pos
        chunks = []
        for c in range(n_chunks):
            mc = m if abs(c - t) <= 1 else (m_neg if c < t else m_pos)
            chunks.append(jnp.exp2(s_ref[slot, half, :, c * KC:(c + 1) * KC] - mc))
        p = jnp.concatenate(chunks, axis=-1)
        pm = jnp.exp2(sm_ref[slot, half] - m)
        l = jnp.sum(p, axis=-1, keepdims=True) + jnp.sum(pm, axis=-1, keepdims=True)
        return p, pm, l

    def finish(t, slot):
        rows = slice(t * TQ, (t + 1) * TQ)
        p1, pm1, l1 = softmax_parts(t, slot, 0)
        p2, pm2, l2 = softmax_parts(t, slot, 1)
        r = lam * l1 * (1.0 / l2)
        a = (p1 - p2 * r).astype(jnp.bfloat16)
        am = (pm1 - pm2 * r).astype(jnp.bfloat16)
        o = (_dot(a, vd_ref[...]) + _dot(am, vdm_ref[...])) * (1.0 / l1)
        y = o * _rms_scale(o, DIFF_V) * subln_ref[...] * (1.0 - LAM_INIT)
        y = y * zb_ref[rows, :].astype(jnp.float32)
        o_ref[rows, :] = y.astype(o_ref.dtype)

    return scores, scores_next, finish


def _attn_kernel(q_ref, k_ref, vt_ref, km_ref, vmt_ref, za_ref,
                 qd_ref, kd_ref, vd_ref, kdm_ref, vdm_ref, btab_ref, mtab_ref, lam_ref, subln_ref, zb_ref,
                 qn_ref, kn_ref, kmn_ref, qdn_ref, kdn_ref, kdmn_ref, btabn_ref, mtabn_ref,
                 oa_ref, ob_ref, ms_ref, msm_ref, mmx_ref, ds_ref, dsm_ref, dmx_ref):
    m_scores, m_scores_next, m_finish = _mla_tile_fns(
        q_ref, k_ref, vt_ref, km_ref, vmt_ref, za_ref, qn_ref, kn_ref, kmn_ref, oa_ref,
        ms_ref, msm_ref, mmx_ref)
    d_scores, d_scores_next, d_finish = _diff_tile_fns(
        qd_ref, kd_ref, vd_ref, kdm_ref, vdm_ref, btab_ref, mtab_ref, lam_ref, subln_ref, zb_ref,
        qdn_ref, kdn_ref, kdmn_ref, btabn_ref, mtabn_ref, ob_ref, ds_ref, dsm_ref, dmx_ref)
    n_t = qd_ref.shape[0] // TQ
    n_u = n_t * MLA_PER_STEP

    @pl.when((pl.program_id(0) == 0) & (pl.program_id(1) == 0))
    def _():
        d_scores(0, 0)
        m_scores(0, 0)

    for t in range(n_t):
        if t + 1 < n_t:
            d_scores(t + 1, (t + 1) % 2)
        else:
            d_scores_next(0)
        for u in range(t * MLA_PER_STEP, (t + 1) * MLA_PER_STEP):
            if u + 1 < n_u:
                m_scores(u + 1, (u + 1) % 2)
            else:
                m_scores_next(0)
            m_finish(u, u % 2)
        d_finish(t, t % 2)


def _attention(q, k, vt, km, vmt, zs, qk, vd, qk_meta, vd_meta, btab, mtab, lam, subln, seq):
    nb = q.shape[0]
    cps = D_MODEL // DIFF_V
    mps = MLA_PER_STEP
    n_steps = nb * DIFF_HEADS

    def nxt(b, h):
        g = jnp.minimum(b * DIFF_HEADS + h + 1, n_steps - 1)
        return g // DIFF_HEADS, g % DIFF_HEADS

    def nb_(b, h):
        return nxt(b, h)[0]

    def nh_(b, h):
        return nxt(b, h)[1]

    return pl.pallas_call(
        _attn_kernel,
        grid=(nb, DIFF_HEADS),
        in_specs=[
            pl.BlockSpec((1, mps, seq, HEAD_PAD), lambda b, h: (b, h, 0, 0)),
            pl.BlockSpec((1, mps, seq, HEAD_PAD), lambda b, h: (b, h, 0, 0)),
            pl.BlockSpec((1, mps, MLA_V, seq), lambda b, h: (b, h, 0, 0)),
            pl.BlockSpec((mps, META_PAD, HEAD_PAD), lambda b, h: (h, 0, 0)),
            pl.BlockSpec((mps, MLA_V, META_PAD), lambda b, h: (h, 0, 0)),
            pl.BlockSpec((seq, mps * MLA_V), lambda b, h: (b, h)),
            pl.BlockSpec((seq, DIFF_V), lambda b, h: (b, h)),
            pl.BlockSpec((seq, DIFF_V), lambda b, h: (b, cps + h)),
            pl.BlockSpec((seq, DIFF_V), lambda b, h: (b, h)),
            pl.BlockSpec((META_PAD, DIFF_V), lambda b, h: (0, h)),
            pl.BlockSpec((META_PAD, DIFF_V), lambda b, h: (0, h)),
            pl.BlockSpec((1, 5, TQ, KC), lambda b, h: (h, 0, 0, 0)),
            pl.BlockSpec((1, 2, TQ, META_PAD), lambda b, h: (h, 0, 0, 0)),
            pl.BlockSpec((4, DIFF_QK), lambda b, h: (0, 0)),
            pl.BlockSpec((1, DIFF_V), lambda b, h: (0, 0)),
            pl.BlockSpec((seq, DIFF_V), lambda b, h: (b, cps + h)),
            pl.BlockSpec((1, 1, TQ, HEAD_PAD), lambda b, h: (nb_(b, h), mps * nh_(b, h), 0, 0)),
            pl.BlockSpec((1, 1, seq, HEAD_PAD), lambda b, h: (nb_(b, h), mps * nh_(b, h), 0, 0)),
            pl.BlockSpec((1, META_PAD, HEAD_PAD), lambda b, h: (mps * nh_(b, h), 0, 0)),
            pl.BlockSpec((TQ, DIFF_V), lambda b, h: (nb_(b, h) * (seq // TQ), nh_(b, h))),
            pl.BlockSpec((seq, DIFF_V), lambda b, h: (nb_(b, h), cps + nh_(b, h))),
            pl.BlockSpec((META_PAD, DIFF_V), lambda b, h: (0, nh_(b, h))),
            pl.BlockSpec((1, 5, TQ, KC), lambda b, h: (nh_(b, h), 0, 0, 0)),
            pl.BlockSpec((1, 2, TQ, META_PAD), lambda b, h: (nh_(b, h), 0, 0, 0)),
        ],
        out_specs=[
            pl.BlockSpec((seq, mps * MLA_V), lambda b, h: (b, h)),
            pl.BlockSpec((seq, DIFF_V), lambda b, h: (b, h)),
        ],
        out_shape=[
            jax.ShapeDtypeStruct((nb * seq, MLA_HEADS * MLA_V), jnp.bfloat16),
            jax.ShapeDtypeStruct((nb * seq, DIFF_HEADS * DIFF_V), jnp.bfloat16),
        ],
        scratch_shapes=[
            pltpu.VMEM((2, seq, TQ), jnp.float32), pltpu.VMEM((2, META_PAD, TQ), jnp.float32),
            pltpu.VMEM((2, 1, TQ), jnp.float32),
            pltpu.VMEM((2, 2, TQ, seq), jnp.float32), pltpu.VMEM((2, 2, TQ, META_PAD), jnp.float32),
            pltpu.VMEM((2, 2, TQ, LANES), jnp.float32),
        ],
        compiler_params=pltpu.CompilerParams(
            dimension_semantics=("arbitrary", "arbitrary"), vmem_limit_bytes=VMEM_LIMIT_BYTES),
        name="attention",
    )(q, k, vt, km, vmt, zs, qk, qk, vd, qk_meta, vd_meta, btab, mtab, lam, subln, zs,
      q, k, km, qk, qk, qk_meta, btab, mtab)


def _merge_kernel(oa_ref, ob_ref, wa_ref, wb_ref, ga_ref, gb_ref, m_ref):
    ya = _dot(oa_ref[...], wa_ref[...])
    yb = _dot(ob_ref[...], wb_ref[...])
    m = ga_ref[...].astype(jnp.float32) * ya + gb_ref[...].astype(jnp.float32) * yb
    m_ref[...] = m.astype(m_ref.dtype)


def _merge(oa, ob, wa, wb, gs, tm, tn):
    rows = oa.shape[0]
    tps = D_MODEL // tn
    return pl.pallas_call(
        _merge_kernel,
        grid=(rows // tm, D_MODEL // tn),
        in_specs=[
            pl.BlockSpec((tm, D_MODEL), lambda i, j: (i, 0)),
            pl.BlockSpec((tm, D_MODEL), lambda i, j: (i, 0)),
            pl.BlockSpec((D_MODEL, tn), lambda i, j: (0, j)),
            pl.BlockSpec((D_MODEL, tn), lambda i, j: (0, j)),
            pl.BlockSpec((tm, tn), lambda i, j: (i, j)),
            pl.BlockSpec((tm, tn), lambda i, j: (i, tps + j)),
        ],
        out_specs=pl.BlockSpec((tm, tn), lambda i, j: (i, j)),
        out_shape=jax.ShapeDtypeStruct((rows, D_MODEL), jnp.bfloat16),
        compiler_params=pltpu.CompilerParams(
            dimension_semantics=("arbitrary", "arbitrary"), vmem_limit_bytes=VMEM_LIMIT_BYTES),
        name="merge",
    )(oa, ob, wa, wb, gs, gs)


def _out_kernel(m_ref, w_ref, x_ref, o_ref):
    o_ref[...] = x_ref[...] + _dot(m_ref[...], w_ref[...])


def _out_proj(m, wout, x2, tm, tn):
    rows = m.shape[0]
    return pl.pallas_call(
        _out_kernel,
        grid=(rows // tm, D_MODEL // tn),
        in_specs=[
            pl.BlockSpec((tm, D_MODEL), lambda i, j: (i, 0)),
            pl.BlockSpec((D_MODEL, tn), lambda i, j: (0, j)),
            pl.BlockSpec((tm, tn), lambda i, j: (i, j)),
        ],
        out_specs=pl.BlockSpec((tm, tn), lambda i, j: (i, j)),
        out_shape=jax.ShapeDtypeStruct((rows, D_MODEL), jnp.float32),
        compiler_params=pltpu.CompilerParams(
            dimension_semantics=("arbitrary", "arbitrary"), vmem_limit_bytes=VMEM_LIMIT_BYTES),
        name="out_proj",
    )(m, wout, x2)


def _t5_bucket(rel):
    nb = REL_BUCKETS // 2
    max_exact = nb // 2
    ret = jnp.where(rel > 0, nb, 0)
    n = jnp.abs(rel)
    nf = jnp.maximum(n, 1).astype(jnp.float32)
    large = max_exact + (jnp.log(nf / max_exact) / math.log(REL_MAX_DIST / max_exact)
                         * (nb - max_exact)).astype(jnp.int32)
    large = jnp.minimum(large, nb - 1)
    return ret + jnp.where(n < max_exact, n, large)


SEG_W = 2 * KC


def _bias_tab_kernel(seg_ref, btab_ref, mtab_ref):
    def expand(row):
        x = jnp.broadcast_to(seg_ref[0, row:row + 1, :], (TQ, SEG_W))
        return pltpu.roll(x, 0, 1, stride=1, stride_axis=0) * LOG2E

    for t in range(5):
        btab_ref[0, t] = expand(t)[:, KC:]
    lane = lax.broadcasted_iota(jnp.int32, (TQ, META_PAD), 1)
    for t in range(2):
        mtab_ref[0, t] = jnp.where(lane < N_META, expand(5 + t)[:, KC:KC + META_PAD], MASK_VALUE)


def _bias_tables(rel_bias):
    i = jnp.arange(SEG_W, dtype=jnp.int32)[None, :]
    dd = jnp.arange(-2, 3, dtype=jnp.int32)[:, None]
    rel_seq = dd * KC - KC + i
    rel_meta = i - KC - N_META - jnp.arange(2, dtype=jnp.int32)[:, None] * TQ
    rel = jnp.concatenate([rel_seq, rel_meta, rel_meta[:1]], axis=0)
    seg = jnp.transpose(rel_bias[_t5_bucket(rel)], (2, 0, 1)).astype(jnp.float32)
    return pl.pallas_call(
        _bias_tab_kernel,
        grid=(DIFF_HEADS,),
        in_specs=[pl.BlockSpec((1, 8, SEG_W), lambda h: (h, 0, 0))],
        out_specs=[
            pl.BlockSpec((1, 5, TQ, KC), lambda h: (h, 0, 0, 0)),
            pl.BlockSpec((1, 2, TQ, META_PAD), lambda h: (h, 0, 0, 0)),
        ],
        out_shape=[
            jax.ShapeDtypeStruct((DIFF_HEADS, 5, TQ, KC), jnp.float32),
            jax.ShapeDtypeStruct((DIFF_HEADS, 2, TQ, META_PAD), jnp.float32),
        ],
        compiler_params=pltpu.CompilerParams(dimension_semantics=("arbitrary",)),
        name="bias_tables",
    )(seg)


def _rope_tables(n_pos):
    half = MLA_ROPE // 2
    inv = ROPE_THETA ** (-jnp.arange(half, dtype=jnp.float32) / half)
    ang = jnp.arange(n_pos, dtype=jnp.int32).astype(jnp.float32)[:, None] * inv[None, :]
    c, s = jnp.cos(ang), jnp.sin(ang)
    z = jnp.zeros((n_pos, LANES - MLA_ROPE), jnp.float32)
    cos_t = jnp.concatenate([c, c, z], axis=-1)
    sin_t = jnp.concatenate([-s, s, z], axis=-1)
    return cos_t, sin_t


def _pad_rows(a, rows):
    return jnp.pad(a, [(0, rows - a.shape[0])] + [(0, 0)] * (a.ndim - 1))


def kernel(x, meta_tokens, rel_bias, norm_in, w_in, q_a_norm, kv_a_norm, w_uq, w_ukv, mla_q_norm, mla_k_norm,
           diff_q_norm, diff_k_norm, diff_lambda, diff_subln, w_branch_a, w_branch_b, w_out):
    nb, seq, _ = x.shape
    bf = jnp.bfloat16
    n_small = MLA_Q_RANK + MLA_KV_RANK + MLA_ROPE

    w_in_t = jnp.swapaxes(w_in[0], 0, 1)
    wsm = jnp.pad(w_in_t[:n_small], ((0, SMALL_W - n_small), (0, 0))).astype(bf)
    wuq = jnp.pad(w_uq[0].reshape(MLA_Q_RANK, MLA_HEADS, MLA_QK),
                  ((0, 0), (0, 0), (0, HEAD_PAD - MLA_QK))).reshape(MLA_Q_RANK, MLA_HEADS * HEAD_PAD).astype(bf)
    wukv = w_ukv[0].reshape(MLA_KV_RANK, MLA_HEADS, MLA_NOPE + MLA_V)
    wuk = wukv[:, :, :MLA_NOPE].reshape(MLA_KV_RANK, MLA_HEADS * MLA_NOPE).astype(bf)
    wvt = jnp.swapaxes(wukv[:, :, MLA_NOPE:].reshape(MLA_KV_RANK, MLA_HEADS * MLA_V), 0, 1).astype(bf)

    gin = norm_in[0][None]
    gqa, gkva = q_a_norm[0][None], kv_a_norm[0][None]
    gq = jnp.pad(mla_q_norm[0], (0, HEAD_PAD - MLA_QK))[None]
    gk = jnp.pad(mla_k_norm[0], (0, HEAD_PAD - MLA_QK))[None]
    gqd, gkd = diff_q_norm[0][None], diff_k_norm[0][None]
    subln = diff_subln[0][None]
    lam = diff_lambda[0]

    cos_t, sin_t = _rope_tables(N_META + seq)
    btab, mtab = _bias_tables(rel_bias)

    x2 = x.reshape(nb * seq, D_MODEL)
    meta = meta_tokens.astype(x.dtype)

    qd_scale = LOG2E * DIFF_QK ** -0.5
    reps = D_MODEL // DIFF_QK
    gain_qk = jnp.concatenate([jnp.tile(gqd * qd_scale, (1, reps)), jnp.tile(gkd, (1, reps))], axis=-1)

    q, k, vt, u, wbig = _mla_proj(x, gin, wsm, gqa, gkva, wuq, wuk, wvt, gq, gk, cos_t[N_META:], sin_t[N_META:],
                                  tm=MLA_PROJ_TM, w_wide=w_in_t, wide_row0=n_small)
    u2 = u.reshape(nb * seq, D_MODEL)
    zs, wa = _seg_proj(u2, wbig, (SEG_ZA, SEG_ZB), "silu", IN_PROJ_TM, IN_PROJ_TN, cast_src=w_branch_a[0])
    qk, wb = _seg_proj(u2, wbig, (SEG_QD, SEG_KD), "norm", IN_PROJ_TM, IN_PROJ_TN, gain=gain_qk,
                       cast_src=w_branch_b[0])
    vd = _seg_proj(u2, wbig, (SEG_VD,), "copy", IN_PROJ_TM, IN_PROJ_TN)
    gs, wo = _seg_proj(u2, wbig, (SEG_GA, SEG_GB), "sigmoid", IN_PROJ_TM, IN_PROJ_TN, cast_src=w_out[0])

    _, km, vmt, um = _mla_proj(meta[None], gin, wsm, gqa, gkva, wuq, wuk, wvt, gq, gk,
                              cos_t[:N_META], sin_t[:N_META], tm=N_META)
    km = jnp.pad(km[0], ((0, 0), (0, META_PAD - N_META), (0, 0)))
    vmt = jnp.pad(vmt[0], ((0, 0), (0, 0), (0, META_PAD - N_META)))
    k_meta = _pad_rows(_seg_proj(um[0], wbig, (SEG_KD,), "norm", N_META, META_TN, gain=gain_qk[:, D_MODEL:]),
                       META_PAD)
    vd_meta = _pad_rows(_seg_proj(um[0], wbig, (SEG_VD,), "copy", N_META, META_TN), META_PAD)

    oa, ob = _attention(q, k, vt, km, vmt, zs, qk, vd, k_meta, vd_meta, btab, mtab, lam, subln, seq)
    m = _merge(oa, ob, wa, wb, gs, tm=MERGE_TM, tn=MERGE_TN)
    out = _out_proj(m, wo, x2, tm=OUT_TM, tn=OUT_TN)
    return out.reshape(nb, seq, D_MODEL)
```
